```python
import math
import jax
import jax.numpy as jnp
from jax import lax
import numpy as np

D_MODEL = 1024
BATCH = 4
SEQ = 8192
DEPTH = 1

N_META = 16
BLOCK = 128
N_LEAD = ((N_META + BLOCK - 1) // BLOCK) * BLOCK
N_PAD = N_LEAD - N_META
EPS = 1e-5

SSD_EXPAND = 2
D_INNER = SSD_EXPAND * D_MODEL
SSD_HEADDIM = 64
SSD_HEADS = D_INNER // SSD_HEADDIM
SSD_GROUPS = 4
SSD_HEADS_PER_GROUP = SSD_HEADS // SSD_GROUPS
SSD_STATE = 128
CONV_K = 4
CONV_DIM = D_INNER + 2 * SSD_GROUPS * SSD_STATE

ATT_HEADDIM = 64
ATT_HEADS = D_MODEL // ATT_HEADDIM
ATT_DIM = ATT_HEADS * ATT_HEADDIM

N_EXPERTS = 32
TOP_K = 4
D_FF = D_MODEL
SWIGLU_LIMIT = 7.0
SWIGLU_ALPHA = 1.702
MOE_BLOCK = 256

IN_SIZES = (D_INNER, CONV_DIM, SSD_HEADS, ATT_DIM, ATT_DIM, ATT_DIM, D_MODEL, D_MODEL)
IN_COLS = sum(IN_SIZES)
IN_SPLITS = tuple(int(s) for s in np.cumsum(IN_SIZES)[:-1])

kernel_name = 'hybrid_ssd_stickbreak_moe_block'


def rms_norm(x, w):
    xf = x.astype(jnp.float32)
    xf = xf * lax.rsqrt(jnp.mean(xf * xf, axis=-1, keepdims=True) + EPS)
    return (xf * w.astype(jnp.float32)).astype(x.dtype)


def causal_depthwise_conv(x, w, b):
    y = lax.conv_general_dilated(
        x, w[:, None, :].astype(x.dtype), window_strides=(1,),
        padding=((CONV_K - 1, 0),), dimension_numbers=('NWC', 'WIO', 'NWC'),
        feature_group_count=x.shape[-1])
    return y + b.astype(x.dtype)


def gated_rms_norm(y, z, w):
    g = y.astype(jnp.float32) * jax.nn.silu(z.astype(jnp.float32))
    shp = g.shape
    g = g.reshape(shp[:-1] + (SSD_GROUPS, D_INNER // SSD_GROUPS))
    g = g * lax.rsqrt(jnp.mean(g * g, axis=-1, keepdims=True) + EPS)
    return (g.reshape(shp) * w.astype(jnp.float32)).astype(z.dtype)


def ssd_chunked_scan(xdt, a, b_mat, c_mat):
    bsz, seq_len = xdt.shape[0], xdt.shape[1]
    nc = seq_len // BLOCK
    X = xdt.reshape(bsz, nc, BLOCK, SSD_GROUPS, SSD_HEADS_PER_GROUP, SSD_HEADDIM)
    a = a.reshape(bsz, nc, BLOCK, SSD_GROUPS, SSD_HEADS_PER_GROUP).transpose(0, 3, 4, 1, 2)
    Bc = b_mat.reshape(bsz, nc, BLOCK, SSD_GROUPS, SSD_STATE)
    Cc = c_mat.reshape(bsz, nc, BLOCK, SSD_GROUPS, SSD_STATE)
    a_cs = jnp.cumsum(a, axis=-1)
    causal = jnp.tril(jnp.ones((BLOCK, BLOCK), dtype=bool))
    seg = a_cs[..., :, None] - a_cs[..., None, :]
    decay_in = jnp.exp(jnp.where(causal, seg, -jnp.inf))
    cb = jnp.einsum('bclgn,bcsgn->bgcls', Cc, Bc)
    y_diag = jnp.einsum('bgcls,bgjcls,bcsgjp->bclgjp', cb, decay_in, X)
    decay_states = jnp.exp(a_cs[..., -1:] - a_cs)
    states = jnp.einsum('bclgn,bgjcl,bclgjp->cbgjpn', Bc, decay_states, X)
    chunk_decay = jnp.exp(a_cs[..., -1]).transpose(3, 0, 1, 2)

    def step(h, inp):
        s_c, d_c = inp
        return d_c[..., None, None] * h + s_c, h

    _, prev_states = lax.scan(step, jnp.zeros_like(states[0]), (states, chunk_decay))
    y_off = jnp.einsum('bclgn,cbgjpn,bgjcl->bclgjp', Cc, prev_states, jnp.exp(a_cs))
    return (y_diag + y_off).reshape(bsz, seq_len, SSD_HEADS, SSD_HEADDIM)


def ssd_branch(z, xbc, dt_raw, conv_w, conv_b, dt_bias, a_log, d_skip, norm_w, valid):
    bsz, seq_len, _ = xbc.shape
    vmask = valid[None, :, None]
    xbc = jax.nn.silu(causal_depthwise_conv(xbc, conv_w, conv_b)) * vmask.astype(xbc.dtype)
    xs, b_mat, c_mat = jnp.split(xbc, (D_INNER, D_INNER + SSD_GROUPS * SSD_STATE), axis=-1)
    dt = jax.nn.softplus(dt_raw.astype(jnp.float32) + dt_bias.astype(jnp.float32)) * vmask
    a = -jnp.exp(a_log.astype(jnp.float32)) * dt
    xs = xs.reshape(bsz, seq_len, SSD_HEADS, SSD_HEADDIM)
    y = ssd_chunked_scan(
        xs * dt[..., None], a,
        b_mat.reshape(bsz, seq_len, SSD_GROUPS, SSD_STATE),
        c_mat.reshape(bsz, seq_len, SSD_GROUPS, SSD_STATE))
    y = y + d_skip.astype(jnp.float32)[:, None] * xs
    return gated_rms_norm(y.reshape(bsz, seq_len, D_INNER), z, norm_w)


def stick_breaking_attention(q, k, v, key_valid):
    bsz, seq_len, _ = q.shape
    nb = seq_len // BLOCK
    to_heads = lambda t: t.reshape(bsz, seq_len, ATT_HEADS, ATT_HEADDIM).transpose(0, 2, 1, 3)
    q, k, v = to_heads(q), to_heads(k), to_heads(v)
    q_blocks = q.reshape(bsz, ATT_HEADS, nb, BLOCK, ATT_HEADDIM).transpose(2, 0, 1, 3, 4)
    key_pos = jnp.arange(seq_len)
    scale = ATT_HEADDIM ** -0.5

    def block(args):
        qb, i = args
        q_pos = i * BLOCK + jnp.arange(BLOCK)
        allowed = (key_pos[None, :] < q_pos[:, None]) & key_valid[None, :]
        logits = jnp.einsum('bhtd,bhsd->bhts', qb, k).astype(jnp.float32) * scale
        log_beta = jax.nn.log_sigmoid(logits)
        log_stay = jnp.where(allowed, jax.nn.log_sigmoid(-logits), 0.0)
        later = lax.cumsum(log_stay, axis=3, reverse=True) - log_stay
        w = jnp.where(allowed, jnp.exp(log_beta + later), 0.0)
        return jnp.einsum('bhts,bhsd->bhtd', w.astype(v.dtype), v)

    out = lax.map(block, (q_blocks, jnp.arange(nb)))
    return out.transpose(1, 0, 3, 2, 4).reshape(bsz, seq_len, ATT_DIM)


def hybrid_mixer(u, valid, w_in, conv_w, conv_b, dt_bias, a_log, d_skip, ssd_norm_w,
                 w_ssd_out, w_att_out, w_out):
    proj = u @ w_in
    z, xbc, dt_raw, q, k, v, g_ssd, g_att = jnp.split(proj, IN_SPLITS, axis=-1)
    y_ssd = ssd_branch(z, xbc, dt_raw, conv_w, conv_b, dt_bias, a_log, d_skip, ssd_norm_w, valid) @ w_ssd_out
    y_att = stick_breaking_attention(q, k, v, valid) @ w_att_out
    merged = jax.nn.sigmoid(g_ssd) * y_ssd + jax.nn.sigmoid(g_att) * y_att
    return (merged @ w_out) * valid[None, :, None].astype(u.dtype)


def moe_ffn(u, w_router, b_router, w_gate_up, b_gate_up, w_down, b_down):
    bsz, seq_len, d = u.shape
    n_tok = bsz * seq_len
    n_assign = n_tok * TOP_K
    flat = u.reshape(n_tok, d)
    logits = (flat @ w_router + b_router).astype(jnp.float32)
    top_logits, top_idx = lax.top_k(logits, TOP_K)
    gates = jax.nn.softmax(top_logits, axis=-1)
    expert_of = top_idx.reshape(-1).astype(jnp.int32)
    token_of = jnp.repeat(jnp.arange(n_tok, dtype=jnp.int32), TOP_K)
    order = jnp.argsort(expert_of, stable=True)
    expert_sorted = expert_of[order]
    counts = jnp.bincount(expert_of, length=N_EXPERTS).astype(jnp.int32)
    starts = jnp.cumsum(counts) - counts
    padded = (counts + MOE_BLOCK - 1) // MOE_BLOCK * MOE_BLOCK
    padded_ends = jnp.cumsum(padded)
    padded_starts = padded_ends - padded
    rank = jnp.arange(n_assign, dtype=jnp.int32) - starts[expert_sorted]
    dest_sorted = (padded_starts[expert_sorted] + rank).astype(jnp.int32)
    n_blocks = (n_assign + MOE_BLOCK - 1) // MOE_BLOCK + N_EXPERTS
    src = jnp.full((n_blocks * MOE_BLOCK,), n_tok, jnp.int32).at[dest_sorted].set(token_of[order])
    block_expert = jnp.clip(
        jnp.searchsorted(padded_ends, jnp.arange(n_blocks) * MOE_BLOCK, side='right'),
        0, N_EXPERTS - 1)
    x_ext = jnp.concatenate([flat, jnp.zeros((1, d), flat.dtype)], axis=0)

    def expert_block(args):
        rows, e = args
        h = x_ext[rows] @ w_gate_up[e] + b_gate_up[e]
        gate = jnp.minimum(h[:, 0::2], SWIGLU_LIMIT)
        lin = jnp.clip(h[:, 1::2], -SWIGLU_LIMIT, SWIGLU_LIMIT)
        act = gate * jax.nn.sigmoid(SWIGLU_ALPHA * gate) * (lin + 1.0)
        return act @ w_down[e] + b_down[e]

    y_blocks = lax.map(expert_block, (src.reshape(n_blocks, MOE_BLOCK), block_expert))
    y_pad = y_blocks.reshape(n_blocks * MOE_BLOCK, d)
    dest = jnp.zeros((n_assign,), jnp.int32).at[order].set(dest_sorted)
    y_assign = y_pad[dest].reshape(n_tok, TOP_K, d)
    out = jnp.einsum('nk,nkd->nd', gates.astype(y_assign.dtype), y_assign)
    return out.reshape(bsz, seq_len, d)


def setup_inputs(seed: int = 0) -> dict:
    key = jax.random.key(seed)
    ks = jax.random.split(key, 24)
    nrm = lambda k, shape, s: jax.random.normal(k, shape, jnp.float32) * s
    dt_init = jnp.exp(jax.random.uniform(ks[6], (DEPTH, SSD_HEADS), jnp.float32)
                      * (math.log(0.1) - math.log(0.001)) + math.log(0.001))
    return {
        'x': nrm(ks[0], (BATCH, SEQ, D_MODEL), 1.0),
        'meta_tokens': nrm(ks[1], (N_META, D_MODEL), 1.0),
        'mix_norm_w': 1.0 + nrm(ks[2], (DEPTH, D_MODEL), 0.05),
        'w_in': nrm(ks[3], (DEPTH, D_MODEL, IN_COLS), D_MODEL ** -0.5),
        'conv_w': nrm(ks[4], (DEPTH, CONV_K, CONV_DIM), CONV_K ** -0.5),
        'conv_b': nrm(ks[5], (DEPTH, CONV_DIM), 0.02),
        'dt_bias': dt_init + jnp.log(-jnp.expm1(-dt_init)),
        'a_log': jnp.log(jax.random.uniform(ks[7], (DEPTH, SSD_HEADS), jnp.float32, 1.0, 16.0)),
        'd_skip': 1.0 + nrm(ks[8], (DEPTH, SSD_HEADS), 0.1),
        'ssd_norm_w': 1.0 + nrm(ks[9], (DEPTH, D_INNER), 0.05),
        'w_ssd_out': nrm(ks[10], (DEPTH, D_INNER, D_MODEL), D_INNER ** -0.5),
        'w_att_out': nrm(ks[11], (DEPTH, ATT_DIM, D_MODEL), ATT_DIM ** -0.5),
        'w_out': nrm(ks[12], (DEPTH, D_MODEL, D_MODEL), D_MODEL ** -0.5),
        'ffn_norm_w': 1.0 + nrm(ks[13], (DEPTH, D_MODEL), 0.05),
        'w_router': nrm(ks[14], (DEPTH, D_MODEL, N_EXPERTS), D_MODEL ** -0.5),
        'b_router': nrm(ks[15], (DEPTH, N_EXPERTS), 0.01),
        'w_gate_up': nrm(ks[16], (DEPTH, N_EXPERTS, D_MODEL, 2 * D_FF), D_MODEL ** -0.5),
        'b_gate_up': nrm(ks[17], (DEPTH, N_EXPERTS, 2 * D_FF), 0.02),
        'w_down': nrm(ks[18], (DEPTH, N_EXPERTS, D_FF, D_MODEL), D_FF ** -0.5),
        'b_down': nrm(ks[19], (DEPTH, N_EXPERTS, D_MODEL), 0.02),
        'final_norm_w': 1.0 + nrm(ks[20], (D_MODEL,), 0.05),
    }


def reference(x, meta_tokens, mix_norm_w, w_in, conv_w, conv_b, dt_bias, a_log, d_skip,
              ssd_norm_w, w_ssd_out, w_att_out, w_out, ffn_norm_w, w_router, b_router,
              w_gate_up, b_gate_up, w_down, b_down, final_norm_w):
    bsz = x.shape[0]
    lead = jnp.concatenate([jnp.zeros((N_PAD, D_MODEL), x.dtype), meta_tokens.astype(x.dtype)], axis=0)
    h = jnp.concatenate([jnp.broadcast_to(lead[None], (bsz, N_LEAD, D_MODEL)), x], axis=1)
    valid = jnp.arange(h.shape[1]) >= N_PAD
    for layer in range(DEPTH):
        u = rms_norm(h, mix_norm_w[layer])
        h = h + hybrid_mixer(u, valid, w_in[layer], conv_w[layer], conv_b[layer], dt_bias[layer],
                             a_log[layer], d_skip[layer], ssd_norm_w[layer], w_ssd_out[layer],
                             w_att_out[layer], w_out[layer])
        u = rms_norm(h[:, N_PAD:], ffn_norm_w[layer])
        f = moe_ffn(u, w_router[layer], b_router[layer], w_gate_up[layer], b_gate_up[layer],
                    w_down[layer], b_down[layer])
        h = h + jnp.pad(f, ((0, 0), (N_PAD, 0), (0, 0)))
    return rms_norm(h[:, N_LEAD:], final_norm_w)
```

```python
import functools

import jax
import jax.numpy as jnp
from jax import lax
from jax.experimental import pallas as pl
from jax.experimental.pallas import tpu as pltpu

F32 = jnp.float32
BF16 = jnp.bfloat16
HIGHEST = lax.Precision.HIGHEST

N_META = 16
BLOCK = 128
N_LEAD = BLOCK
N_PAD = N_LEAD - N_META
EPS = 1e-5
SSD_HEADDIM = 64
SSD_GROUPS = 4
SSD_STATE = 128
CONV_K = 4
ATT_HEADDIM = 64
TOP_K = 4
SWIGLU_LIMIT = 7.0
SWIGLU_ALPHA = 1.702
MOE_BLOCK = 256

LANES = 128
SUBLANES = 8
ROW_TILE = 640
EXP_UNDERFLOW = -104.0
VMEM_LIMIT = 56 * 1024 * 1024


def _nt_dot(a, b, precision=None):
    return lax.dot_general(a, b, (((1,), (1,)), ((), ())),
                           preferred_element_type=F32, precision=precision)


def _tn_dot(a, b):
    return lax.dot_general(a, b, (((0,), (0,)), ((), ())), preferred_element_type=F32)


def _silu(x):
    return x * jax.nn.sigmoid(x)


def _inproj_kernel(h_ref, nw_ref, w_ref, wdt_ref, out_ref, dt_ref, u_scr):
    @pl.when(pl.program_id(1) == 0)
    def _():
        x = h_ref[...]
        u = x * lax.rsqrt(jnp.mean(x * x, axis=-1, keepdims=True) + EPS) * nw_ref[...]
        ub = u.astype(BF16)
        u_scr[...] = ub
        dt_ref[...] = jnp.dot(ub, wdt_ref[...], preferred_element_type=F32)

    out_ref[...] = jnp.dot(u_scr[...], w_ref[...], preferred_element_type=F32).astype(out_ref.dtype)


def _inproj(h, norm_w, w_main, w_dt, tn=1024):
    m, d = h.shape
    n = w_main.shape[1]
    return pl.pallas_call(
        _inproj_kernel,
        grid=(m // ROW_TILE, n // tn),
        in_specs=[
            pl.BlockSpec((ROW_TILE, d), lambda i, j: (i, 0)),
            pl.BlockSpec((1, d), lambda i, j: (0, 0)),
            pl.BlockSpec((d, tn), lambda i, j: (0, j)),
            pl.BlockSpec((d, LANES), lambda i, j: (0, 0)),
        ],
        out_specs=[
            pl.BlockSpec((ROW_TILE, tn), lambda i, j: (i, j)),
            pl.BlockSpec((ROW_TILE, LANES), lambda i, j: (i, 0)),
        ],
        out_shape=[jax.ShapeDtypeStruct((m, n), BF16), jax.ShapeDtypeStruct((m, LANES), F32)],
        scratch_shapes=[pltpu.VMEM((ROW_TILE, d), BF16)],
        compiler_params=pltpu.CompilerParams(
            dimension_semantics=("parallel", "arbitrary"), vmem_limit_bytes=VMEM_LIMIT),
        name="inproj",
    )(h, norm_w, w_main, w_dt)


def _ssd_kernel(z_ref, xs_ref, bc_ref, dt_ref, cwx_ref, cbx_ref, cwb_ref, cbb_ref,
                dtb_ref, alog_ref, dskip_ref, nw_ref, expand_ref, tril_ref,
                out_ref, extx_scr, extb_scr, state_scr, *, n_heads):
    c = pl.program_id(1)
    d_inner = xs_ref.shape[1]
    gw = d_inner // SSD_GROUPS
    hpg = n_heads // SSD_GROUPS

    @pl.when(c == 0)
    def _():
        extx_scr[0:SUBLANES, :] = jnp.zeros((SUBLANES, extx_scr.shape[1]), F32)
        extb_scr[0:SUBLANES, :] = jnp.zeros((SUBLANES, extb_scr.shape[1]), F32)
        state_scr[...] = jnp.zeros(state_scr.shape, F32)

    @pl.when(c > 0)
    def _():
        extx_scr[0:SUBLANES, :] = extx_scr[BLOCK:BLOCK + SUBLANES, :]
        extb_scr[0:SUBLANES, :] = extb_scr[BLOCK:BLOCK + SUBLANES, :]

    extx_scr[SUBLANES:SUBLANES + BLOCK, :] = xs_ref[...].astype(F32)
    extb_scr[SUBLANES:SUBLANES + BLOCK, :] = bc_ref[...].astype(F32)

    row = lax.broadcasted_iota(jnp.int32, (BLOCK, 1), 0)
    vmask = (c * BLOCK + row >= N_PAD).astype(F32)

    def conv(ext_scr, w_ref, b_ref):
        acc = b_ref[...]
        for i in range(CONV_K):
            start = SUBLANES - (CONV_K - 1) + i
            acc = acc + w_ref[i:i + 1, :] * ext_scr[start:start + BLOCK, :]
        return _silu(acc) * vmask

    xs = conv(extx_scr, cwx_ref, cbx_ref)
    bc = conv(extb_scr, cwb_ref, cbb_ref)
    gn = SSD_GROUPS * SSD_STATE
    b_all = bc[:, :gn].astype(BF16)
    c_all = bc[:, gn:].astype(BF16)

    dt = jax.nn.softplus(dt_ref[:, :n_heads] + dtb_ref[...]) * vmask
    a = -jnp.exp(alog_ref[...]) * dt
    a_cs = jnp.dot(tril_ref[...], a, preferred_element_type=F32, precision=HIGHEST)
    a_cs_t = a_cs.T
    a_last = a_cs[BLOCK - 1:BLOCK, :]
    decay_states = jnp.exp(a_last - a_cs)
    decay_out = jnp.exp(a_cs)
    chunk_decay = jnp.broadcast_to(jnp.exp(a_last), (SUBLANES, n_heads))
    stacked = jnp.concatenate([dt, decay_states, decay_out, chunk_decay], axis=0)
    expanded = jnp.dot(stacked, expand_ref[...], preferred_element_type=F32, precision=HIGHEST)
    dt_x = expanded[0:BLOCK]
    ds_x = expanded[BLOCK:2 * BLOCK]
    do_x = expanded[2 * BLOCK:3 * BLOCK]
    cd_x = expanded[3 * BLOCK:3 * BLOCK + 1]

    x_dt = xs * dt_x
    x_dt_b = x_dt.astype(BF16)
    x_ds_b = (x_dt * ds_x).astype(BF16)

    li = lax.broadcasted_iota(jnp.int32, (BLOCK, BLOCK), 0)
    si = lax.broadcasted_iota(jnp.int32, (BLOCK, BLOCK), 1)
    causal = si <= li
    first_half = lax.broadcasted_iota(jnp.int32, (BLOCK, 2 * SSD_HEADDIM), 1) < SSD_HEADDIM

    y_groups = []
    for g in range(SSD_GROUPS):
        bg = b_all[:, g * SSD_STATE:(g + 1) * SSD_STATE]
        cg = c_all[:, g * SSD_STATE:(g + 1) * SSD_STATE]
        cols = slice(g * gw, (g + 1) * gw)
        cb = _nt_dot(cg, bg)
        prev = state_scr[g]
        y_off = jnp.dot(cg, prev.astype(BF16), preferred_element_type=F32) * do_x[:, cols]
        new_state = _tn_dot(bg, x_ds_b[:, cols])
        state_scr[g] = cd_x[:, cols] * prev + new_state
        pairs = []
        for jp in range(hpg // 2):
            h0 = g * hpg + 2 * jp
            pc = slice(g * gw + jp * 2 * SSD_HEADDIM, g * gw + (jp + 1) * 2 * SSD_HEADDIM)
            xp = x_dt_b[:, pc]
            ys = []
            for h in (h0, h0 + 1):
                seg = a_cs[:, h:h + 1] - a_cs_t[h:h + 1, :]
                m = (cb * jnp.exp(jnp.where(causal, seg, -jnp.inf))).astype(BF16)
                ys.append(jnp.dot(m, xp, preferred_element_type=F32))
            pairs.append(jnp.where(first_half, ys[0], ys[1]))
        y_groups.append(jnp.concatenate(pairs, axis=1) + y_off)
    y = jnp.concatenate(y_groups, axis=1) + dskip_ref[...] * xs

    gated = y * _silu(z_ref[...].astype(F32))
    outs = []
    for g in range(SSD_GROUPS):
        gg = gated[:, g * gw:(g + 1) * gw]
        outs.append(gg * lax.rsqrt(jnp.mean(gg * gg, axis=-1, keepdims=True) + EPS))
    out_ref[...] = (jnp.concatenate(outs, axis=1) * nw_ref[...]).astype(out_ref.dtype)


def _ssd(proj, dt_raw, conv_w, conv_b, dt_bias, a_log, d_skip, norm_w, bsz, nc, d_inner):
    m = proj.shape[0]
    n_heads = a_log.shape[0]
    bc_w = 2 * SSD_GROUPS * SSD_STATE
    gw = d_inner // SSD_GROUPS
    cwx, cwb = conv_w[:, :d_inner], conv_w[:, d_inner:]
    cbx, cbb = conv_b[None, :d_inner], conv_b[None, d_inner:]
    expand = jnp.repeat(jnp.eye(n_heads, dtype=F32), SSD_HEADDIM, axis=1)
    tril = jnp.tril(jnp.ones((BLOCK, BLOCK), F32))
    dskip_x = jnp.repeat(d_skip.astype(F32), SSD_HEADDIM)[None, :]
    full = lambda shape: pl.BlockSpec(shape, lambda b, c: (0,) * len(shape))
    row_blk = lambda b, c: b * nc + c
    return pl.pallas_call(
        functools.partial(_ssd_kernel, n_heads=n_heads),
        grid=(bsz, nc),
        in_specs=[
            pl.BlockSpec((BLOCK, d_inner), lambda b, c: (row_blk(b, c), 0)),
            pl.BlockSpec((BLOCK, d_inner), lambda b, c: (row_blk(b, c), 1)),
            pl.BlockSpec((BLOCK, bc_w), lambda b, c: (row_blk(b, c), 2 * d_inner // bc_w)),
            pl.BlockSpec((BLOCK, LANES), lambda b, c: (row_blk(b, c), 0)),
            full((CONV_K, d_inner)), full((1, d_inner)), full((CONV_K, bc_w)), full((1, bc_w)),
            full((1, n_heads)), full((1, n_heads)), full((1, d_inner)), full((1, d_inner)),
            full((n_heads, d_inner)), full((BLOCK, BLOCK)),
        ],
        out_specs=pl.BlockSpec((BLOCK, d_inner), lambda b, c: (row_blk(b, c), 0)),
        out_shape=jax.ShapeDtypeStruct((m, d_inner), BF16),
        scratch_shapes=[
            pltpu.VMEM((BLOCK + SUBLANES, d_inner), F32),
            pltpu.VMEM((BLOCK + SUBLANES, bc_w), F32),
            pltpu.VMEM((SSD_GROUPS, SSD_STATE, gw), F32),
        ],
        compiler_params=pltpu.CompilerParams(
            dimension_semantics=("parallel", "arbitrary"), vmem_limit_bytes=VMEM_LIMIT),
        name="ssd",
    )(proj, proj, proj, dt_raw, cwx, cbx, cwb, cbb, dt_bias[None, :].astype(F32),
      a_log[None, :].astype(F32), dskip_x, norm_w[None, :].astype(F32), expand, tril)


def _attn_kernel(q_ref, k_ref, v_ref, upper_ref, out_ref):
    i = pl.program_id(2)
    q = q_ref[...]
    lane = lax.broadcasted_iota(jnp.int32, (BLOCK, 2 * ATT_HEADDIM), 1)
    qpos = i * BLOCK + lax.broadcasted_iota(jnp.int32, (BLOCK, BLOCK), 0)
    kin = lax.broadcasted_iota(jnp.int32, (BLOCK, BLOCK), 1)
    scale = ATT_HEADDIM ** -0.5
    upper = upper_ref[...]

    def one_head(head_mask):
        qh = jnp.where(head_mask, q, jnp.zeros_like(q))

        def cond(carry):
            kb, top, _, _ = carry
            return jnp.logical_and(kb >= 0, top > EXP_UNDERFLOW)

        def body(carry):
            kb, _, run, acc = carry
            start = pl.multiple_of(kb * BLOCK, BLOCK)
            kblk = k_ref[pl.ds(start, BLOCK), :]
            vblk = v_ref[pl.ds(start, BLOCK), :]
            z = _nt_dot(qh, kblk) * scale
            kpos = kb * BLOCK + kin
            allowed = jnp.logical_and(kpos < qpos, kpos >= N_PAD)
            l1p = jnp.log(1.0 + jnp.exp(-jnp.abs(z)))
            log_beta = jnp.minimum(z, 0.0) - l1p
            log_stay = jnp.where(allowed, -jnp.maximum(z, 0.0) - l1p, 0.0)
            hi = log_stay.astype(BF16)
            lo = (log_stay - hi.astype(F32)).astype(BF16)
            later = (jnp.dot(hi, upper, preferred_element_type=F32)
                     + jnp.dot(lo, upper, preferred_element_type=F32))
            w = jnp.where(allowed, jnp.exp(log_beta + later + run), 0.0)
            acc = acc + jnp.dot(w.astype(BF16), vblk, preferred_element_type=F32)
            run = run + jnp.sum(log_stay, axis=1, keepdims=True)
            return kb - 1, jnp.max(run), run, acc

        init = (i, jnp.float32(0.0), jnp.zeros((BLOCK, 1), F32),
                jnp.zeros((BLOCK, 2 * ATT_HEADDIM), F32))
        return lax.while_loop(cond, body, init)[3]

    o0 = one_head(lane < ATT_HEADDIM)
    o1 = one_head(lane >= ATT_HEADDIM)
    out_ref[...] = jnp.where(lane < ATT_HEADDIM, o0, o1).astype(out_ref.dtype)


def _attention(proj, bsz, nc, q_col, k_col, v_col, att_dim):
    m = proj.shape[0]
    seq = nc * BLOCK
    pw = 2 * ATT_HEADDIM
    n_pairs = att_dim // pw
    upper = jnp.triu(jnp.ones((BLOCK, BLOCK), F32), 1).T.astype(BF16)
    return pl.pallas_call(
        _attn_kernel,
        grid=(bsz, n_pairs, nc),
        in_specs=[
            pl.BlockSpec((BLOCK, pw), lambda b, p, i: (b * nc + i, q_col // pw + p)),
            pl.BlockSpec((seq, pw), lambda b, p, i: (b, k_col // pw + p)),
            pl.BlockSpec((seq, pw), lambda b, p, i: (b, v_col // pw + p)),
            pl.BlockSpec((BLOCK, BLOCK), lambda b, p, i: (0, 0)),
        ],
        out_specs=pl.BlockSpec((BLOCK, pw), lambda b, p, i: (b * nc + i, p)),
        out_shape=jax.ShapeDtypeStruct((m, att_dim), BF16),
        compiler_params=pltpu.CompilerParams(
            dimension_semantics=("parallel", "parallel", "arbitrary"),
            vmem_limit_bytes=VMEM_LIMIT),
        name="attention",
    )(proj, proj, proj, upper)


def _merge_kernel(yssd_ref, att_ref, gs_ref, ga_ref, h_ref, wso_ref, wao_ref, wo_ref, out_ref):
    t = pl.program_id(1)
    y_ssd = jnp.dot(yssd_ref[...], wso_ref[...], preferred_element_type=F32)
    y_att = jnp.dot(att_ref[...], wao_ref[...], preferred_element_type=F32)
    merged = (jax.nn.sigmoid(gs_ref[...].astype(F32)) * y_ssd
              + jax.nn.sigmoid(ga_ref[...].astype(F32)) * y_att)
    mixed = jnp.dot(merged.astype(BF16), wo_ref[...], preferred_element_type=F32)
    row = lax.broadcasted_iota(jnp.int32, (ROW_TILE, 1), 0)
    valid = (t * ROW_TILE + row >= N_PAD).astype(F32)
    out_ref[...] = h_ref[...] + mixed * valid


def _merge(yssd, att, proj, h, w_ssd_out, w_att_out, w_out, bsz, seq, gs_col, ga_col):
    m, d = h.shape
    d_inner = yssd.shape[1]
    att_dim = att.shape[1]
    per = seq // ROW_TILE
    rb = lambda b, t: b * per + t
    full = lambda shape: pl.BlockSpec(shape, lambda b, t: (0,) * len(shape))
    return pl.pallas_call(
        _merge_kernel,
        grid=(bsz, per),
        in_specs=[
            pl.BlockSpec((ROW_TILE, d_inner), lambda b, t: (rb(b, t), 0)),
            pl.BlockSpec((ROW_TILE, att_dim), lambda b, t: (rb(b, t), 0)),
            pl.BlockSpec((ROW_TILE, d), lambda b, t: (rb(b, t), gs_col // d)),
            pl.BlockSpec((ROW_TILE, d), lambda b, t: (rb(b, t), ga_col // d)),
            pl.BlockSpec((ROW_TILE, d), lambda b, t: (rb(b, t), 0)),
            full((d_inner, d)), full((att_dim, d)), full((d, d)),
        ],
        out_specs=pl.BlockSpec((ROW_TILE, d), lambda b, t: (rb(b, t), 0)),
        out_shape=jax.ShapeDtypeStruct((m, d), F32),
        compiler_params=pltpu.CompilerParams(
            dimension_semantics=("parallel", "parallel"), vmem_limit_bytes=VMEM_LIMIT),
        name="merge",
    )(yssd, att, proj, proj, h, w_ssd_out, w_att_out, w_out)


def _router_kernel(h_ref, nw_ref, wr_ref, br_ref, before_ref,
                   ut_ref, idx_ref, gate_ref, rank_ref, cnt_ref, base_scr):
    step = pl.program_id(0)
    n_exp = wr_ref.shape[0]
    tm = h_ref.shape[0]

    @pl.when(step == 0)
    def _():
        base_scr[...] = jnp.zeros(base_scr.shape, F32)

    x = h_ref[...]
    u = x * lax.rsqrt(jnp.mean(x * x, axis=-1, keepdims=True) + EPS) * nw_ref[...]
    for c in range(SUBLANES):
        ut_ref[pl.ds(c, tm, stride=SUBLANES), :] = u[:, c * LANES:(c + 1) * LANES]

    logits = _nt_dot(wr_ref[...], u, precision=HIGHEST) + br_ref[...]
    eidx = lax.broadcasted_iota(jnp.int32, (n_exp, tm), 0)
    work = logits
    tops, idxs, onehots = [], [], []
    for _ in range(TOP_K):
        top = jnp.max(work, axis=0, keepdims=True)
        idx = jnp.min(jnp.where(work == top, eidx, n_exp), axis=0, keepdims=True)
        hot = eidx == idx
        work = jnp.where(hot, -jnp.inf, work)
        tops.append(top)
        idxs.append(idx)
        onehots.append(hot.astype(F32))
    exps = [jnp.exp(t - tops[0]) for t in tops]
    denom = exps[0] + exps[1] + exps[2] + exps[3]
    cnt = onehots[0] + onehots[1] + onehots[2] + onehots[3]
    before = jnp.dot(cnt.astype(BF16), before_ref[...], preferred_element_type=F32) + base_scr[...]
    ranks = [jnp.sum(hot * before, axis=0, keepdims=True) for hot in onehots]
    pad_rows = SUBLANES - TOP_K
    idx_ref[...] = jnp.concatenate(idxs + [jnp.zeros((pad_rows, tm), jnp.int32)], axis=0)
    gate_ref[...] = jnp.concatenate([e / denom for e in exps] + [jnp.zeros((pad_rows, tm), F32)], axis=0)
    rank_ref[...] = jnp.concatenate(
        [r.astype(jnp.int32) for r in ranks] + [jnp.zeros((pad_rows, tm), jnp.int32)], axis=0)
    base_scr[...] = base_scr[...] + jnp.sum(cnt, axis=1, keepdims=True)
    cnt_ref[...] = jnp.broadcast_to(base_scr[...], cnt_ref.shape)


def _router(h1, norm_w, w_router, b_router):
    m, d = h1.shape
    n_exp = w_router.shape[1]
    before = jnp.triu(jnp.ones((ROW_TILE, ROW_TILE), F32), 1).astype(BF16)
    full = lambda shape: pl.BlockSpec(shape, lambda i: (0,) * len(shape))
    return pl.pallas_call(
        _router_kernel,
        grid=(m // ROW_TILE,),
        in_specs=[
            pl.BlockSpec((ROW_TILE, d), lambda i: (i, 0)),
            full((1, d)), full((n_exp, d)), full((n_exp, 1)), full((ROW_TILE, ROW_TILE)),
        ],
        out_specs=[
            pl.BlockSpec((ROW_TILE * SUBLANES, LANES), lambda i: (i, 0)),
            pl.BlockSpec((SUBLANES, ROW_TILE), lambda i: (0, i)),
            pl.BlockSpec((SUBLANES, ROW_TILE), lambda i: (0, i)),
            pl.BlockSpec((SUBLANES, ROW_TILE), lambda i: (0, i)),
            full((n_exp, LANES)),
        ],
        out_shape=[
            jax.ShapeDtypeStruct((m * SUBLANES, LANES), F32),
            jax.ShapeDtypeStruct((SUBLANES, m), jnp.int32),
            jax.ShapeDtypeStruct((SUBLANES, m), F32),
            jax.ShapeDtypeStruct((SUBLANES, m), jnp.int32),
            jax.ShapeDtypeStruct((n_exp, LANES), F32),
        ],
        scratch_shapes=[pltpu.VMEM((n_exp, 1), F32)],
        compiler_params=pltpu.CompilerParams(
            dimension_semantics=("arbitrary",), vmem_limit_bytes=VMEM_LIMIT),
        name="router",
    )(h1, norm_w, w_router.T, b_router[:, None], before)


def _dispatch_kernel(zstart_ref, zpad_ref, nused_ref, dest_ref, ut_ref, xs_ref, zero_scr, sem, zsem,
                     *, n_exp):
    step = pl.program_id(0)
    tb = dest_ref.shape[1]
    slot_rows = MOE_BLOCK * SUBLANES
    n_blocks = xs_ref.shape[0] // slot_rows

    @pl.when(step == 0)
    def _():
        zero_scr[...] = jnp.zeros(zero_scr.shape, F32)

        def zero_rows(row_start, n_rows):
            return pltpu.make_async_copy(zero_scr.at[pl.ds(0, n_rows)],
                                         xs_ref.at[pl.ds(row_start, n_rows)], zsem)

        def pad_fill(wait):
            def per_expert(e, carry):
                n_pad = zpad_ref[e]
                pos = zstart_ref[e]
                size = MOE_BLOCK // 2
                while size >= 1:
                    take = n_pad & size

                    @pl.when(take != 0)
                    def _(pos=pos, size=size):
                        copy = zero_rows(pl.multiple_of(pos * SUBLANES, SUBLANES), size * SUBLANES)
                        copy.wait() if wait else copy.start()

                    pos = pos + take
                    size //= 2
                return carry

            lax.fori_loop(0, n_exp, per_expert, 0)

        def tail_fill(wait):
            def per_block(b, carry):
                copy = zero_rows(pl.multiple_of(b * slot_rows, slot_rows), slot_rows)
                copy.wait() if wait else copy.start()
                return carry

            lax.fori_loop(nused_ref[0], n_blocks, per_block, 0)

        pad_fill(False)
        tail_fill(False)
        pad_fill(True)
        tail_fill(True)

    def issue(t, carry):
        src = pl.multiple_of((step * tb + t) * SUBLANES, SUBLANES)
        for k in range(TOP_K):
            dst = pl.multiple_of(dest_ref[k, t] * SUBLANES, SUBLANES)
            pltpu.make_async_copy(ut_ref.at[pl.ds(src, SUBLANES)],
                                  xs_ref.at[pl.ds(dst, SUBLANES)], sem).start()
        return carry

    lax.fori_loop(0, tb, issue, 0)
    n_rows = tb * TOP_K * SUBLANES
    pltpu.make_async_copy(ut_ref.at[pl.ds(0, n_rows)], xs_ref.at[pl.ds(0, n_rows)], sem).wait()


def _dispatch(ut, dest, zstart, zpad, n_used, n_slots):
    m = dest.shape[1]
    n_exp = zstart.shape[0]
    return pl.pallas_call(
        functools.partial(_dispatch_kernel, n_exp=n_exp),
        grid_spec=pltpu.PrefetchScalarGridSpec(
            num_scalar_prefetch=3,
            grid=(m // ROW_TILE,),
            in_specs=[
                pl.BlockSpec((SUBLANES, ROW_TILE), lambda i, *_: (0, i), memory_space=pltpu.SMEM),
                pl.BlockSpec(memory_space=pl.ANY),
            ],
            out_specs=pl.BlockSpec(memory_space=pl.ANY),
            scratch_shapes=[
                pltpu.VMEM((MOE_BLOCK * SUBLANES, LANES), F32),
                pltpu.SemaphoreType.DMA(()),
                pltpu.SemaphoreType.DMA(()),
            ],
        ),
        out_shape=jax.ShapeDtypeStruct((n_slots * SUBLANES, LANES), F32),
        compiler_params=pltpu.CompilerParams(
            dimension_semantics=("arbitrary",), vmem_limit_bytes=VMEM_LIMIT),
        name="dispatch",
    )(zstart, zpad, n_used, dest, ut)


def _expert_kernel(be_ref, nused_ref, xs_ref, wg_ref, bg_ref, wl_ref, bl_ref, wd_ref, bd_ref, y_ref):
    b = pl.program_id(0)

    @pl.when(b < nused_ref[0])
    def _():
        x = jnp.concatenate(
            [xs_ref[pl.ds(c, MOE_BLOCK, stride=SUBLANES), :] for c in range(SUBLANES)],
            axis=1).astype(BF16)
        gate = jnp.dot(x, wg_ref[0], preferred_element_type=F32) + bg_ref[0]
        lin = jnp.dot(x, wl_ref[0], preferred_element_type=F32) + bl_ref[0]
        gate = jnp.minimum(gate, SWIGLU_LIMIT)
        lin = jnp.clip(lin, -SWIGLU_LIMIT, SWIGLU_LIMIT)
        act = gate * jax.nn.sigmoid(SWIGLU_ALPHA * gate) * (lin + 1.0)
        y = jnp.dot(act.astype(BF16), wd_ref[0], preferred_element_type=F32) + bd_ref[0]
        for c in range(SUBLANES):
            y_ref[pl.ds(c, MOE_BLOCK, stride=SUBLANES), :] = y[:, c * LANES:(c + 1) * LANES]

    @pl.when(b >= nused_ref[0])
    def _():
        y_ref[...] = jnp.zeros(y_ref.shape, F32)


def _experts(xs, block_expert, n_used, w_gate, b_gate, w_lin, b_lin, w_down, b_down, n_blocks):
    n_exp, d, d_ff = w_gate.shape
    rows = MOE_BLOCK * SUBLANES
    blk = lambda b, be, nu: (jnp.minimum(b, nu[0] - 1), 0)
    wsel = lambda b, be, nu: (be[b], 0, 0)
    return pl.pallas_call(
        _expert_kernel,
        grid_spec=pltpu.PrefetchScalarGridSpec(
            num_scalar_prefetch=2,
            grid=(n_blocks,),
            in_specs=[
                pl.BlockSpec((rows, LANES), blk),
                pl.BlockSpec((1, d, d_ff), wsel), pl.BlockSpec((1, 1, d_ff), wsel),
                pl.BlockSpec((1, d, d_ff), wsel), pl.BlockSpec((1, 1, d_ff), wsel),
                pl.BlockSpec((1, d_ff, d), wsel), pl.BlockSpec((1, 1, d), wsel),
            ],
            out_specs=pl.BlockSpec((rows, LANES), lambda b, be, nu: (b, 0)),
        ),
        out_shape=jax.ShapeDtypeStruct((n_blocks * rows, LANES), F32),
        compiler_params=pltpu.CompilerParams(
            dimension_semantics=("arbitrary",), vmem_limit_bytes=VMEM_LIMIT),
        name="experts",
    )(block_expert, n_used, xs, w_gate, b_gate, w_lin, b_lin, w_down, b_down)


def _combine_kernel(dest_ref, gate_ref, h_ref, nw_ref, eye_ref, ypad_ref, out_ref, buf, sem, *, n_steps):
    s = pl.program_id(0)
    rows = BLOCK * SUBLANES

    @pl.when(s < n_steps)
    def _():
        slot = s % 2

        def issue(t, carry):
            for k in range(TOP_K):
                src = pl.multiple_of(dest_ref[k, t] * SUBLANES, SUBLANES)
                dst = pl.multiple_of(k * rows + t * SUBLANES, SUBLANES)
                pltpu.make_async_copy(ypad_ref.at[pl.ds(src, SUBLANES)],
                                      buf.at[slot, pl.ds(dst, SUBLANES)], sem.at[slot]).start()
            return carry

        lax.fori_loop(0, BLOCK, issue, 0)

    @pl.when(s > 0)
    def _():
        slot = (s - 1) % 2
        pltpu.make_async_copy(ypad_ref.at[pl.ds(0, TOP_K * rows)], buf.at[slot], sem.at[slot]).wait()
        gates_t = _nt_dot(eye_ref[...], gate_ref[...], precision=HIGHEST)
        acc = h_ref[...]
        for k in range(TOP_K):
            yk = jnp.concatenate(
                [buf[slot, pl.ds(k * rows + c, BLOCK, stride=SUBLANES), :] for c in range(SUBLANES)],
                axis=1)
            acc = acc + gates_t[:, k:k + 1] * yk
        out = acc * lax.rsqrt(jnp.mean(acc * acc, axis=-1, keepdims=True) + EPS) * nw_ref[...]
        out_ref[...] = out


def _combine(ypad, dest, gates, h1, final_norm_w, bsz, nc):
    m, d = h1.shape
    per = nc - 1
    n_steps = bsz * per
    eye = jnp.eye(BLOCK, dtype=F32)

    def tok_blk(s):
        s = jnp.clip(s, 0, n_steps - 1)
        return (s // per) * nc + 1 + s % per

    return pl.pallas_call(
        functools.partial(_combine_kernel, n_steps=n_steps),
        grid=(n_steps + 1,),
        in_specs=[
            pl.BlockSpec((SUBLANES, BLOCK), lambda s: (0, tok_blk(s)), memory_space=pltpu.SMEM),
            pl.BlockSpec((SUBLANES, BLOCK), lambda s: (0, tok_blk(s - 1))),
            pl.BlockSpec((BLOCK, d), lambda s: (tok_blk(s - 1), 0)),
            pl.BlockSpec((1, d), lambda s: (0, 0)),
            pl.BlockSpec((BLOCK, BLOCK), lambda s: (0, 0)),
            pl.BlockSpec(memory_space=pl.ANY),
        ],
        out_specs=pl.BlockSpec((BLOCK, d), lambda s: (jnp.clip(s - 1, 0, n_steps - 1), 0)),
        out_shape=jax.ShapeDtypeStruct((n_steps * BLOCK, d), F32),
        scratch_shapes=[
            pltpu.VMEM((2, TOP_K * BLOCK * SUBLANES, LANES), F32),
            pltpu.SemaphoreType.DMA((2,)),
        ],
        compiler_params=pltpu.CompilerParams(
            dimension_semantics=("arbitrary",), vmem_limit_bytes=VMEM_LIMIT),
        name="combine",
    )(dest, gates, h1, final_norm_w[None, :], eye, ypad)


def _layer(h, bsz, nc, mix_norm_w, w_in, conv_w, conv_b, dt_bias, a_log, d_skip, ssd_norm_w,
           w_ssd_out, w_att_out, w_out, ffn_norm_w, w_router, b_router, w_gate_up, b_gate_up,
           w_down, b_down):
    m, d = h.shape
    seq = nc * BLOCK
    n_heads = a_log.shape[0]
    d_inner = n_heads * SSD_HEADDIM
    conv_dim = d_inner + 2 * SSD_GROUPS * SSD_STATE
    att_dim = w_att_out.shape[0]
    n_exp = w_router.shape[1]

    o_xbc = d_inner
    o_dt = o_xbc + conv_dim
    o_q = o_dt + n_heads
    w_main = jnp.concatenate([w_in[:, :o_dt], w_in[:, o_q:]], axis=1).astype(BF16)
    w_dt = jnp.pad(w_in[:, o_dt:o_q], ((0, 0), (0, LANES - n_heads))).astype(BF16)
    q_col = o_dt
    k_col = q_col + att_dim
    v_col = k_col + att_dim
    gs_col = v_col + att_dim
    ga_col = gs_col + d

    proj, dt_raw = _inproj(h, mix_norm_w[None, :], w_main, w_dt)
    yssd = _ssd(proj, dt_raw, conv_w, conv_b, dt_bias, a_log, d_skip, ssd_norm_w, bsz, nc, d_inner)
    att = _attention(proj, bsz, nc, q_col, k_col, v_col, att_dim)
    h1 = _merge(yssd, att, proj, h, w_ssd_out.astype(BF16), w_att_out.astype(BF16),
                w_out.astype(BF16), bsz, seq, gs_col, ga_col)

    ut, idx, gates, rank, cnt = _router(h1, ffn_norm_w[None, :], w_router, b_router)

    counts = cnt[:, 0].astype(jnp.int32)
    padded = (counts + MOE_BLOCK - 1) // MOE_BLOCK * MOE_BLOCK
    padded_ends = jnp.cumsum(padded)
    padded_starts = padded_ends - padded
    n_blocks = (m * TOP_K + MOE_BLOCK - 1) // MOE_BLOCK + n_exp
    dest = padded_starts[idx] + rank
    block_expert = jnp.clip(
        jnp.searchsorted(padded_ends, jnp.arange(n_blocks, dtype=jnp.int32) * MOE_BLOCK, side='right'),
        0, n_exp - 1).astype(jnp.int32)
    n_used = (padded_ends[-1:] // MOE_BLOCK).astype(jnp.int32)
    zstart = (padded_starts + counts).astype(jnp.int32)

    zpad = (padded - counts).astype(jnp.int32)
    xs = _dispatch(ut, dest, zstart, zpad, n_used, n_blocks * MOE_BLOCK)
    w_gate = w_gate_up[:, :, 0::2].astype(BF16)
    w_lin = w_gate_up[:, :, 1::2].astype(BF16)
    b_gate = b_gate_up[:, None, 0::2]
    b_lin = b_gate_up[:, None, 1::2]
    ypad = _experts(xs, block_expert, n_used, w_gate, b_gate, w_lin, b_lin,
                    w_down.astype(BF16), b_down[:, None, :], n_blocks)
    return h1, ypad, dest, gates


def kernel(x, meta_tokens, mix_norm_w, w_in, conv_w, conv_b, dt_bias, a_log, d_skip, ssd_norm_w,
           w_ssd_out, w_att_out, w_out, ffn_norm_w, w_router, b_router, w_gate_up, b_gate_up,
           w_down, b_down, final_norm_w):
    bsz, seq_x, d = x.shape
    depth = mix_norm_w.shape[0]
    assert depth == 1 and seq_x % BLOCK == 0 and (seq_x + N_LEAD) % ROW_TILE == 0
    nc = (seq_x + N_LEAD) // BLOCK
    lead = jnp.concatenate([jnp.zeros((N_PAD, d), x.dtype), meta_tokens.astype(x.dtype)], axis=0)
    h = jnp.concatenate([jnp.broadcast_to(lead[None], (bsz, N_LEAD, d)), x], axis=1)
    h = h.reshape(bsz * nc * BLOCK, d)
    layer = 0
    h1, ypad, dest, gates = _layer(
        h, bsz, nc, mix_norm_w[layer], w_in[layer], conv_w[layer], conv_b[layer], dt_bias[layer],
        a_log[layer], d_skip[layer], ssd_norm_w[layer], w_ssd_out[layer], w_att_out[layer],
        w_out[layer], ffn_norm_w[layer], w_router[layer], b_router[layer], w_gate_up[layer],
        b_gate_up[layer], w_down[layer], b_down[layer])
    out = _combine(ypad, dest, gates, h1, final_norm_w, bsz, nc)
    return out.reshape(bsz, seq_x, d)
```

```python
import functools

import jax
import jax.numpy as jnp
from jax import lax
from jax.experimental import pallas as pl
from jax.experimental.pallas import tpu as pltpu

F32 = jnp.float32
BF16 = jnp.bfloat16
HIGHEST = lax.Precision.HIGHEST

N_META = 16
BLOCK = 128
N_LEAD = BLOCK
N_PAD = N_LEAD - N_META
EPS = 1e-5
SSD_HEADDIM = 64
SSD_GROUPS = 4
SSD_STATE = 128
CONV_K = 4
ATT_HEADDIM = 64
TOP_K = 4
SWIGLU_LIMIT = 7.0
SWIGLU_ALPHA = 1.702
MOE_BLOCK = 256

LANES = 128
SUBLANES = 8
ROW_TILE = 640
EXP_UNDERFLOW = -104.0
VMEM_LIMIT = 56 * 1024 * 1024


def _nt_dot(a, b, precision=None):
    return lax.dot_general(a, b, (((1,), (1,)), ((), ())),
                           preferred_element_type=F32, precision=precision)


def _tn_dot(a, b):
    return lax.dot_general(a, b, (((0,), (0,)), ((), ())), preferred_element_type=F32)


def _silu(x):
    return x * jax.nn.sigmoid(x)


def _inproj_kernel(h_ref, nw_ref, w_ref, wdt_ref, out_ref, dt_ref, u_scr):
    @pl.when(pl.program_id(1) == 0)
    def _():
        x = h_ref[...]
        u = x * lax.rsqrt(jnp.mean(x * x, axis=-1, keepdims=True) + EPS) * nw_ref[...]
        ub = u.astype(BF16)
        u_scr[...] = ub
        dt_ref[...] = jnp.dot(ub, wdt_ref[...], preferred_element_type=F32)

    out_ref[...] = jnp.dot(u_scr[...], w_ref[...], preferred_element_type=F32).astype(out_ref.dtype)


def _inproj(h, norm_w, w_main, w_dt, tn=1024):
    m, d = h.shape
    n = w_main.shape[1]
    return pl.pallas_call(
        _inproj_kernel,
        grid=(m // ROW_TILE, n // tn),
        in_specs=[
            pl.BlockSpec((ROW_TILE, d), lambda i, j: (i, 0)),
            pl.BlockSpec((1, d), lambda i, j: (0, 0)),
            pl.BlockSpec((d, tn), lambda i, j: (0, j)),
            pl.BlockSpec((d, LANES), lambda i, j: (0, 0)),
        ],
        out_specs=[
            pl.BlockSpec((ROW_TILE, tn), lambda i, j: (i, j)),
            pl.BlockSpec((ROW_TILE, LANES), lambda i, j: (i, 0)),
        ],
        out_shape=[jax.ShapeDtypeStruct((m, n), BF16), jax.ShapeDtypeStruct((m, LANES), F32)],
        scratch_shapes=[pltpu.VMEM((ROW_TILE, d), BF16)],
        compiler_params=pltpu.CompilerParams(
            dimension_semantics=("parallel", "arbitrary"), vmem_limit_bytes=VMEM_LIMIT),
        name="inproj",
    )(h, norm_w, w_main, w_dt)


def _ssd_kernel(z_ref, xs_ref, bc_ref, dt_ref, cwx_ref, cbx_ref, cwb_ref, cbb_ref,
                dtb_ref, alog_ref, dskip_ref, nw_ref, expand_ref, tril_ref,
                out_ref, extx_scr, extb_scr, state_scr, *, n_heads):
    c = pl.program_id(1)
    d_inner = xs_ref.shape[1]
    gw = d_inner // SSD_GROUPS
    hpg = n_heads // SSD_GROUPS

    @pl.when(c == 0)
    def _():
        extx_scr[0:SUBLANES, :] = jnp.zeros((SUBLANES, extx_scr.shape[1]), F32)
        extb_scr[0:SUBLANES, :] = jnp.zeros((SUBLANES, extb_scr.shape[1]), F32)
        state_scr[...] = jnp.zeros(state_scr.shape, F32)

    @pl.when(c > 0)
    def _():
        extx_scr[0:SUBLANES, :] = extx_scr[BLOCK:BLOCK + SUBLANES, :]
        extb_scr[0:SUBLANES, :] = extb_scr[BLOCK:BLOCK + SUBLANES, :]

    extx_scr[SUBLANES:SUBLANES + BLOCK, :] = xs_ref[...].astype(F32)
    extb_scr[SUBLANES:SUBLANES + BLOCK, :] = bc_ref[...].astype(F32)

    row = lax.broadcasted_iota(jnp.int32, (BLOCK, 1), 0)
    vmask = (c * BLOCK + row >= N_PAD).astype(F32)

    def conv(ext_scr, w_ref, b_ref):
        acc = b_ref[...]
        for i in range(CONV_K):
            start = SUBLANES - (CONV_K - 1) + i
            acc = acc + w_ref[i:i + 1, :] * ext_scr[start:start + BLOCK, :]
        return _silu(acc) * vmask

    xs = conv(extx_scr, cwx_ref, cbx_ref)
    bc = conv(extb_scr, cwb_ref, cbb_ref)
    gn = SSD_GROUPS * SSD_STATE
    b_all = bc[:, :gn].astype(BF16)
    c_all = bc[:, gn:].astype(BF16)

    dt = jax.nn.softplus(dt_ref[:, :n_heads] + dtb_ref[...]) * vmask
    a = -jnp.exp(alog_ref[...]) * dt
    a_cs = jnp.dot(tril_ref[...], a, preferred_element_type=F32, precision=HIGHEST)
    a_cs_t = a_cs.T
    a_last = a_cs[BLOCK - 1:BLOCK, :]
    decay_states = jnp.exp(a_last - a_cs)
    decay_out = jnp.exp(a_cs)
    chunk_decay = jnp.broadcast_to(jnp.exp(a_last), (SUBLANES, n_heads))
    stacked = jnp.concatenate([dt, decay_states, decay_out, chunk_decay], axis=0)
    expanded = jnp.dot(stacked, expand_ref[...], preferred_element_type=F32, precision=HIGHEST)
    dt_x = expanded[0:BLOCK]
    ds_x = expanded[BLOCK:2 * BLOCK]
    do_x = expanded[2 * BLOCK:3 * BLOCK]
    cd_x = expanded[3 * BLOCK:3 * BLOCK + 1]

    x_dt = xs * dt_x
    x_dt_b = x_dt.astype(BF16)
    x_ds_b = (x_dt * ds_x).astype(BF16)

    li = lax.broadcasted_iota(jnp.int32, (BLOCK, BLOCK), 0)
    si = lax.broadcasted_iota(jnp.int32, (BLOCK, BLOCK), 1)
    causal = si <= li
    first_half = lax.broadcasted_iota(jnp.int32, (BLOCK, 2 * SSD_HEADDIM), 1) < SSD_HEADDIM

    y_groups = []
    for g in range(SSD_GROUPS):
        bg = b_all[:, g * SSD_STATE:(g + 1) * SSD_STATE]
        cg = c_all[:, g * SSD_STATE:(g + 1) * SSD_STATE]
        cols = slice(g * gw, (g + 1) * gw)
        cb = _nt_dot(cg, bg)
        prev = state_scr[g]
        y_off = jnp.dot(cg, prev.astype(BF16), preferred_element_type=F32) * do_x[:, cols]
        new_state = _tn_dot(bg, x_ds_b[:, cols])
        state_scr[g] = cd_x[:, cols] * prev + new_state
        pairs = []
        for jp in range(hpg // 2):
            h0 = g * hpg + 2 * jp
            pc = slice(g * gw + jp * 2 * SSD_HEADDIM, g * gw + (jp + 1) * 2 * SSD_HEADDIM)
            xp = x_dt_b[:, pc]
            ys = []
            for h in (h0, h0 + 1):
                seg = a_cs[:, h:h + 1] - a_cs_t[h:h + 1, :]
                m = (cb * jnp.exp(jnp.where(causal, seg, -jnp.inf))).astype(BF16)
                ys.append(jnp.dot(m, xp, preferred_element_type=F32))
            pairs.append(jnp.where(first_half, ys[0], ys[1]))
        y_groups.append(jnp.concatenate(pairs, axis=1) + y_off)
    y = jnp.concatenate(y_groups, axis=1) + dskip_ref[...] * xs

    gated = y * _silu(z_ref[...].astype(F32))
    outs = []
    for g in range(SSD_GROUPS):
        gg = gated[:, g * gw:(g + 1) * gw]
        outs.append(gg * lax.rsqrt(jnp.mean(gg * gg, axis=-1, keepdims=True) + EPS))
    out_ref[...] = (jnp.concatenate(outs, axis=1) * nw_ref[...]).astype(out_ref.dtype)


def _ssd(proj, dt_raw, conv_w, conv_b, dt_bias, a_log, d_skip, norm_w, bsz, nc, d_inner):
    m = proj.shape[0]
    n_heads = a_log.shape[0]
    bc_w = 2 * SSD_GROUPS * SSD_STATE
    gw = d_inner // SSD_GROUPS
    cwx, cwb = conv_w[:, :d_inner], conv_w[:, d_inner:]
    cbx, cbb = conv_b[None, :d_inner], conv_b[None, d_inner:]
    expand = jnp.repeat(jnp.eye(n_heads, dtype=F32), SSD_HEADDIM, axis=1)
    tril = jnp.tril(jnp.ones((BLOCK, BLOCK), F32))
    dskip_x = jnp.repeat(d_skip.astype(F32), SSD_HEADDIM)[None, :]
    full = lambda shape: pl.BlockSpec(shape, lambda b, c: (0,) * len(shape))
    row_blk = lambda b, c: b * nc + c
    return pl.pallas_call(
        functools.partial(_ssd_kernel, n_heads=n_heads),
        grid=(bsz, nc),
        in_specs=[
            pl.BlockSpec((BLOCK, d_inner), lambda b, c: (row_blk(b, c), 0)),
            pl.BlockSpec((BLOCK, d_inner), lambda b, c: (row_blk(b, c), 1)),
            pl.BlockSpec((BLOCK, bc_w), lambda b, c: (row_blk(b, c), 2 * d_inner // bc_w)),
            pl.BlockSpec((BLOCK, LANES), lambda b, c: (row_blk(b, c), 0)),
            full((CONV_K, d_inner)), full((1, d_inner)), full((CONV_K, bc_w)), full((1, bc_w)),
            full((1, n_heads)), full((1, n_heads)), full((1, d_inner)), full((1, d_inner)),
            full((n_heads, d_inner)), full((BLOCK, BLOCK)),
        ],
        out_specs=pl.BlockSpec((BLOCK, d_inner), lambda b, c: (row_blk(b, c), 0)),
        out_shape=jax.ShapeDtypeStruct((m, d_inner), BF16),
        scratch_shapes=[
            pltpu.VMEM((BLOCK + SUBLANES, d_inner), F32),
            pltpu.VMEM((BLOCK + SUBLANES, bc_w), F32),
            pltpu.VMEM((SSD_GROUPS, SSD_STATE, gw), F32),
        ],
        compiler_params=pltpu.CompilerParams(
            dimension_semantics=("parallel", "arbitrary"), vmem_limit_bytes=VMEM_LIMIT),
        name="ssd",
    )(proj, proj, proj, dt_raw, cwx, cbx, cwb, cbb, dt_bias[None, :].astype(F32),
      a_log[None, :].astype(F32), dskip_x, norm_w[None, :].astype(F32), expand, tril)


def _attn_kernel(q_ref, k_ref, v_ref, upper_ref, out_ref, *, n_pairs):
    i = pl.program_id(2)
    pw = 2 * ATT_HEADDIM
    lane = lax.broadcasted_iota(jnp.int32, (BLOCK, pw), 1)
    first = lane < ATT_HEADDIM
    row2 = lax.broadcasted_iota(jnp.int32, (2 * BLOCK, BLOCK), 0)
    qpos = i * BLOCK + jnp.where(row2 < BLOCK, row2, row2 - BLOCK)
    kin = lax.broadcasted_iota(jnp.int32, (2 * BLOCK, BLOCK), 1)
    upper = upper_ref[...]
    scale = ATT_HEADDIM ** -0.5
    q2 = []
    for p in range(n_pairs):
        q = q_ref[:, p * pw:(p + 1) * pw] * jnp.asarray(scale, BF16)
        zero = jnp.zeros_like(q)
        q2.append(jnp.concatenate([jnp.where(first, q, zero), jnp.where(first, zero, q)], axis=0))

    def cond(carry):
        kb, top = carry[0], carry[1]
        return jnp.logical_and(kb >= 0, top > EXP_UNDERFLOW)

    def body(carry):
        kb = carry[0]
        runs, accs = carry[2:2 + n_pairs], carry[2 + n_pairs:]
        start = pl.multiple_of(kb * BLOCK, BLOCK)
        kpos = kb * BLOCK + kin
        allowed = jnp.logical_and(kpos < qpos, kpos >= N_PAD)
        new_runs, new_accs = [], []
        for p in range(n_pairs):
            kblk = k_ref[pl.ds(start, BLOCK), p * pw:(p + 1) * pw]
            vblk = v_ref[pl.ds(start, BLOCK), p * pw:(p + 1) * pw]
            z = _nt_dot(q2[p], kblk)
            l1p = jnp.log(1.0 + jnp.exp(-jnp.abs(z)))
            log_beta = jnp.minimum(z, 0.0) - l1p
            log_stay = jnp.where(allowed, -jnp.maximum(z, 0.0) - l1p, 0.0)
            hi = log_stay.astype(BF16)
            lo = (log_stay - hi.astype(F32)).astype(BF16)
            later = (jnp.dot(hi, upper, preferred_element_type=F32)
                     + jnp.dot(lo, upper, preferred_element_type=F32))
            w = jnp.where(allowed, jnp.exp(log_beta + later + runs[p]), 0.0).astype(BF16)
            vzero = jnp.zeros_like(vblk)
            new_accs.append(
                accs[p]
                + jnp.dot(w[:BLOCK], jnp.where(first, vblk, vzero), preferred_element_type=F32)
                + jnp.dot(w[BLOCK:], jnp.where(first, vzero, vblk), preferred_element_type=F32))
            new_runs.append(runs[p] + jnp.sum(log_stay, axis=1, keepdims=True))
        top = jnp.max(new_runs[0])
        for r in new_runs[1:]:
            top = jnp.maximum(top, jnp.max(r))
        return (kb - 1, top, *new_runs, *new_accs)

    init = ((i, jnp.float32(0.0))
            + tuple(jnp.zeros((2 * BLOCK, 1), F32) for _ in range(n_pairs))
            + tuple(jnp.zeros((BLOCK, pw), F32) for _ in range(n_pairs)))
    final = lax.while_loop(cond, body, init)
    for p in range(n_pairs):
        out_ref[:, p * pw:(p + 1) * pw] = final[2 + n_pairs + p].astype(out_ref.dtype)


def _attention(proj, bsz, nc, q_col, k_col, v_col, att_dim, pairs_per_step=4):
    m = proj.shape[0]
    seq = nc * BLOCK
    sw = pairs_per_step * 2 * ATT_HEADDIM
    upper = jnp.triu(jnp.ones((BLOCK, BLOCK), F32), 1).T.astype(BF16)
    return pl.pallas_call(
        functools.partial(_attn_kernel, n_pairs=pairs_per_step),
        grid=(bsz, att_dim // sw, nc),
        in_specs=[
            pl.BlockSpec((BLOCK, sw), lambda b, p, i: (b * nc + i, q_col // sw + p)),
            pl.BlockSpec((seq, sw), lambda b, p, i: (b, k_col // sw + p)),
            pl.BlockSpec((seq, sw), lambda b, p, i: (b, v_col // sw + p)),
            pl.BlockSpec((BLOCK, BLOCK), lambda b, p, i: (0, 0)),
        ],
        out_specs=pl.BlockSpec((BLOCK, sw), lambda b, p, i: (b * nc + i, p)),
        out_shape=jax.ShapeDtypeStruct((m, att_dim), BF16),
        compiler_params=pltpu.CompilerParams(
            dimension_semantics=("parallel", "parallel", "arbitrary"),
            vmem_limit_bytes=VMEM_LIMIT),
        name="attention",
    )(proj, proj, proj, upper)


def _merge_kernel(yssd_ref, att_ref, gs_ref, ga_ref, h_ref, wso_ref, wao_ref, wo_ref, out_ref):
    t = pl.program_id(1)
    y_ssd = jnp.dot(yssd_ref[...], wso_ref[...], preferred_element_type=F32)
    y_att = jnp.dot(att_ref[...], wao_ref[...], preferred_element_type=F32)
    merged = (jax.nn.sigmoid(gs_ref[...].astype(F32)) * y_ssd
              + jax.nn.sigmoid(ga_ref[...].astype(F32)) * y_att)
    mixed = jnp.dot(merged.astype(BF16), wo_ref[...], preferred_element_type=F32)
    row = lax.broadcasted_iota(jnp.int32, (ROW_TILE, 1), 0)
    valid = (t * ROW_TILE + row >= N_PAD).astype(F32)
    out_ref[...] = h_ref[...] + mixed * valid


def _merge(yssd, att, proj, h, w_ssd_out, w_att_out, w_out, bsz, seq, gs_col, ga_col):
    m, d = h.shape
    d_inner = yssd.shape[1]
    att_dim = att.shape[1]
    per = seq // ROW_TILE
    rb = lambda b, t: b * per + t
    full = lambda shape: pl.BlockSpec(shape, lambda b, t: (0,) * len(shape))
    return pl.pallas_call(
        _merge_kernel,
        grid=(bsz, per),
        in_specs=[
            pl.BlockSpec((ROW_TILE, d_inner), lambda b, t: (rb(b, t), 0)),
            pl.BlockSpec((ROW_TILE, att_dim), lambda b, t: (rb(b, t), 0)),
            pl.BlockSpec((ROW_TILE, d), lambda b, t: (rb(b, t), gs_col // d)),
            pl.BlockSpec((ROW_TILE, d), lambda b, t: (rb(b, t), ga_col // d)),
            pl.BlockSpec((ROW_TILE, d), lambda b, t: (rb(b, t), 0)),
            full((d_inner, d)), full((att_dim, d)), full((d, d)),
        ],
        out_specs=pl.BlockSpec((ROW_TILE, d), lambda b, t: (rb(b, t), 0)),
        out_shape=jax.ShapeDtypeStruct((m, d), F32),
        compiler_params=pltpu.CompilerParams(
            dimension_semantics=("parallel", "parallel"), vmem_limit_bytes=VMEM_LIMIT),
        name="merge",
    )(yssd, att, proj, proj, h, w_ssd_out, w_att_out, w_out)


def _router_kernel(h_ref, nw_ref, wr_ref, br_ref, before_ref,
                   ut_ref, idx_ref, gate_ref, rank_ref, cnt_ref, base_scr):
    step = pl.program_id(0)
    n_exp = wr_ref.shape[0]
    tm = h_ref.shape[0]

    @pl.when(step == 0)
    def _():
        base_scr[...] = jnp.zeros(base_scr.shape, F32)

    x = h_ref[...]
    u = x * lax.rsqrt(jnp.mean(x * x, axis=-1, keepdims=True) + EPS) * nw_ref[...]
    for c in range(SUBLANES):
        ut_ref[pl.ds(c, tm, stride=SUBLANES), :] = u[:, c * LANES:(c + 1) * LANES]

    logits = _nt_dot(wr_ref[...], u, precision=HIGHEST) + br_ref[...]
    eidx = lax.broadcasted_iota(jnp.int32, (n_exp, tm), 0)
    work = logits
    tops, idxs, onehots = [], [], []
    for _ in range(TOP_K):
        top = jnp.max(work, axis=0, keepdims=True)
        idx = jnp.min(jnp.where(work == top, eidx, n_exp), axis=0, keepdims=True)
        hot = eidx == idx
        work = jnp.where(hot, -jnp.inf, work)
        tops.append(top)
        idxs.append(idx)
        onehots.append(hot.astype(F32))
    exps = [jnp.exp(t - tops[0]) for t in tops]
    denom = exps[0] + exps[1] + exps[2] + exps[3]
    cnt = onehots[0] + onehots[1] + onehots[2] + onehots[3]
    before = jnp.dot(cnt.astype(BF16), before_ref[...], preferred_element_type=F32) + base_scr[...]
    ranks = [jnp.sum(hot * before, axis=0, keepdims=True) for hot in onehots]
    pad_rows = SUBLANES - TOP_K
    idx_ref[...] = jnp.concatenate(idxs + [jnp.zeros((pad_rows, tm), jnp.int32)], axis=0)
    gate_ref[...] = jnp.concatenate([e / denom for e in exps] + [jnp.zeros((pad_rows, tm), F32)], axis=0)
    rank_ref[...] = jnp.concatenate(
        [r.astype(jnp.int32) for r in ranks] + [jnp.zeros((pad_rows, tm), jnp.int32)], axis=0)
    base_scr[...] = base_scr[...] + jnp.sum(cnt, axis=1, keepdims=True)
    cnt_ref[...] = jnp.broadcast_to(base_scr[...], cnt_ref.shape)


def _router(h1, norm_w, w_router, b_router):
    m, d = h1.shape
    n_exp = w_router.shape[1]
    before = jnp.triu(jnp.ones((ROW_TILE, ROW_TILE), F32), 1).astype(BF16)
    full = lambda shape: pl.BlockSpec(shape, lambda i: (0,) * len(shape))
    return pl.pallas_call(
        _router_kernel,
        grid=(m // ROW_TILE,),
        in_specs=[
            pl.BlockSpec((ROW_TILE, d), lambda i: (i, 0)),
            full((1, d)), full((n_exp, d)), full((n_exp, 1)), full((ROW_TILE, ROW_TILE)),
        ],
        out_specs=[
            pl.BlockSpec((ROW_TILE * SUBLANES, LANES), lambda i: (i, 0)),
            pl.BlockSpec((SUBLANES, ROW_TILE), lambda i: (0, i)),
            pl.BlockSpec((SUBLANES, ROW_TILE), lambda i: (0, i)),
            pl.BlockSpec((SUBLANES, ROW_TILE), lambda i: (0, i)),
            full((n_exp, LANES)),
        ],
        out_shape=[
            jax.ShapeDtypeStruct((m * SUBLANES, LANES), F32),
            jax.ShapeDtypeStruct((SUBLANES, m), jnp.int32),
            jax.ShapeDtypeStruct((SUBLANES, m), F32),
            jax.ShapeDtypeStruct((SUBLANES, m), jnp.int32),
            jax.ShapeDtypeStruct((n_exp, LANES), F32),
        ],
        scratch_shapes=[pltpu.VMEM((n_exp, 1), F32)],
        compiler_params=pltpu.CompilerParams(
            dimension_semantics=("arbitrary",), vmem_limit_bytes=VMEM_LIMIT),
        name="router",
    )(h1, norm_w, w_router.T, b_router[:, None], before)


def _dispatch_kernel(zstart_ref, zpad_ref, nused_ref, dest_ref, ut_ref, xs_ref, zero_scr, sem, zsem,
                     *, n_exp):
    step = pl.program_id(0)
    tb = dest_ref.shape[1]
    slot_rows = MOE_BLOCK * SUBLANES
    n_blocks = xs_ref.shape[0] // slot_rows

    @pl.when(step == 0)
    def _():
        zero_scr[...] = jnp.zeros(zero_scr.shape, F32)

        def zero_rows(row_start, n_rows):
            return pltpu.make_async_copy(zero_scr.at[pl.ds(0, n_rows)],
                                         xs_ref.at[pl.ds(row_start, n_rows)], zsem)

        def pad_fill(wait):
            def per_expert(e, carry):
                n_pad = zpad_ref[e]
                pos = zstart_ref[e]
                size = MOE_BLOCK // 2
                while size >= 1:
                    take = n_pad & size

                    @pl.when(take != 0)
                    def _(pos=pos, size=size):
                        copy = zero_rows(pl.multiple_of(pos * SUBLANES, SUBLANES), size * SUBLANES)
                        copy.wait() if wait else copy.start()

                    pos = pos + take
                    size //= 2
                return carry

            lax.fori_loop(0, n_exp, per_expert, 0)

        def tail_fill(wait):
            def per_block(b, carry):
                copy = zero_rows(pl.multiple_of(b * slot_rows, slot_rows), slot_rows)
                copy.wait() if wait else copy.start()
                return carry

            lax.fori_loop(nused_ref[0], n_blocks, per_block, 0)

        pad_fill(False)
        tail_fill(False)
        pad_fill(True)
        tail_fill(True)

    def issue(t, carry):
        src = pl.multiple_of(t * SUBLANES, SUBLANES)
        for k in range(TOP_K):
            dst = pl.multiple_of(dest_ref[k, t] * SUBLANES, SUBLANES)
            pltpu.make_async_copy(ut_ref.at[pl.ds(src, SUBLANES)],
                                  xs_ref.at[pl.ds(dst, SUBLANES)], sem).start()
        return carry

    lax.fori_loop(0, tb, issue, 0, unroll=4)
    for _ in range(TOP_K):
        pltpu.make_async_copy(ut_ref, xs_ref.at[pl.ds(0, tb * SUBLANES)], sem).wait()


def _dispatch(ut, dest, zstart, zpad, n_used, n_slots):
    m = dest.shape[1]
    n_exp = zstart.shape[0]
    return pl.pallas_call(
        functools.partial(_dispatch_kernel, n_exp=n_exp),
        grid_spec=pltpu.PrefetchScalarGridSpec(
            num_scalar_prefetch=3,
            grid=(m // ROW_TILE,),
            in_specs=[
                pl.BlockSpec((SUBLANES, ROW_TILE), lambda i, *_: (0, i), memory_space=pltpu.SMEM),
                pl.BlockSpec((ROW_TILE * SUBLANES, LANES), lambda i, *_: (i, 0)),
            ],
            out_specs=pl.BlockSpec(memory_space=pl.ANY),
            scratch_shapes=[
                pltpu.VMEM((MOE_BLOCK * SUBLANES, LANES), F32),
                pltpu.SemaphoreType.DMA(()),
                pltpu.SemaphoreType.DMA(()),
            ],
        ),
        out_shape=jax.ShapeDtypeStruct((n_slots * SUBLANES, LANES), F32),
        compiler_params=pltpu.CompilerParams(
            dimension_semantics=("arbitrary",), vmem_limit_bytes=VMEM_LIMIT),
        name="dispatch",
    )(zstart, zpad, n_used, dest, ut)


def _expert_kernel(be_ref, nused_ref, xs_ref, wgu_ref, bg_ref, bl_ref, wd_ref, bd_ref, perm_ref,
                   y_ref, wg_scr, wl_scr, wd_scr):
    b = pl.program_id(0)
    pair = 2 * LANES
    changed = jnp.logical_or(b == 0, be_ref[b] != be_ref[jnp.maximum(b - 1, 0)])

    @pl.when(jnp.logical_and(changed, b < nused_ref[0]))
    def _():
        for cb in range(wgu_ref.shape[2] // pair):
            wb = wgu_ref[0, :, cb * pair:(cb + 1) * pair].astype(BF16)
            sep = jnp.dot(wb, perm_ref[...], preferred_element_type=F32).astype(BF16)
            wg_scr[:, cb * LANES:(cb + 1) * LANES] = sep[:, :LANES]
            wl_scr[:, cb * LANES:(cb + 1) * LANES] = sep[:, LANES:]
        wd_scr[...] = wd_ref[0].astype(BF16)

    @pl.when(b < nused_ref[0])
    def _():
        x = jnp.concatenate(
            [xs_ref[pl.ds(c, MOE_BLOCK, stride=SUBLANES), :] for c in range(SUBLANES)],
            axis=1).astype(BF16)
        gate = jnp.dot(x, wg_scr[...], preferred_element_type=F32) + bg_ref[0]
        lin = jnp.dot(x, wl_scr[...], preferred_element_type=F32) + bl_ref[0]
        gate = jnp.minimum(gate, SWIGLU_LIMIT)
        lin = jnp.clip(lin, -SWIGLU_LIMIT, SWIGLU_LIMIT)
        act = gate * jax.nn.sigmoid(SWIGLU_ALPHA * gate) * (lin + 1.0)
        y = jnp.dot(act.astype(BF16), wd_scr[...], preferred_element_type=F32) + bd_ref[0]
        for c in range(SUBLANES):
            y_ref[pl.ds(c, MOE_BLOCK, stride=SUBLANES), :] = y[:, c * LANES:(c + 1) * LANES]

    @pl.when(b >= nused_ref[0])
    def _():
        y_ref[...] = jnp.zeros(y_ref.shape, F32)


def _experts(xs, block_expert, n_used, w_gate_up, b_gate, b_lin, w_down, b_down, n_blocks):
    n_exp, d, d_ff2 = w_gate_up.shape
    d_ff = d_ff2 // 2
    rows = MOE_BLOCK * SUBLANES
    blk = lambda b, be, nu: (jnp.minimum(b, nu[0] - 1), 0)
    wsel = lambda b, be, nu: (be[b], 0, 0)
    r = jnp.arange(2 * LANES)[:, None]
    c = jnp.arange(2 * LANES)[None, :]
    perm = (r == jnp.where(c < LANES, 2 * c, 2 * (c - LANES) + 1)).astype(BF16)
    return pl.pallas_call(
        _expert_kernel,
        grid_spec=pltpu.PrefetchScalarGridSpec(
            num_scalar_prefetch=2,
            grid=(n_blocks,),
            in_specs=[
                pl.BlockSpec((rows, LANES), blk),
                pl.BlockSpec((1, d, d_ff2), wsel),
                pl.BlockSpec((1, 1, d_ff), wsel), pl.BlockSpec((1, 1, d_ff), wsel),
                pl.BlockSpec((1, d_ff, d), wsel), pl.BlockSpec((1, 1, d), wsel),
                pl.BlockSpec((2 * LANES, 2 * LANES), lambda b, be, nu: (0, 0)),
            ],
            out_specs=pl.BlockSpec((rows, LANES), lambda b, be, nu: (b, 0)),
            scratch_shapes=[
                pltpu.VMEM((d, d_ff), BF16), pltpu.VMEM((d, d_ff), BF16), pltpu.VMEM((d_ff, d), BF16),
            ],
        ),
        out_shape=jax.ShapeDtypeStruct((n_blocks * rows, LANES), F32),
        compiler_params=pltpu.CompilerParams(
            dimension_semantics=("arbitrary",), vmem_limit_bytes=VMEM_LIMIT),
        name="experts",
    )(block_expert, n_used, xs, w_gate_up, b_gate, b_lin, w_down, b_down, perm)


def _combine_kernel(dest_ref, gate_ref, h_ref, nw_ref, eye_ref, ypad_ref, out_ref, buf, sem, *, n_steps):
    s = pl.program_id(0)
    rows = BLOCK * SUBLANES

    @pl.when(s < n_steps)
    def _():
        slot = s % 2

        def issue(t, carry):
            for k in range(TOP_K):
                src = pl.multiple_of(dest_ref[k, t] * SUBLANES, SUBLANES)
                dst = pl.multiple_of(k * rows + t * SUBLANES, SUBLANES)
                pltpu.make_async_copy(ypad_ref.at[pl.ds(src, SUBLANES)],
                                      buf.at[slot, pl.ds(dst, SUBLANES)], sem.at[slot]).start()
            return carry

        lax.fori_loop(0, BLOCK, issue, 0)

    @pl.when(s > 0)
    def _():
        slot = (s - 1) % 2
        pltpu.make_async_copy(ypad_ref.at[pl.ds(0, TOP_K * rows)], buf.at[slot], sem.at[slot]).wait()
        gates_t = _nt_dot(eye_ref[...], gate_ref[...], precision=HIGHEST)
        acc = h_ref[...]
        for k in range(TOP_K):
            yk = jnp.concatenate(
                [buf[slot, pl.ds(k * rows + c, BLOCK, stride=SUBLANES), :] for c in range(SUBLANES)],
                axis=1)
            acc = acc + gates_t[:, k:k + 1] * yk
        out = acc * lax.rsqrt(jnp.mean(acc * acc, axis=-1, keepdims=True) + EPS) * nw_ref[...]
        out_ref[...] = out


def _combine(ypad, dest, gates, h1, final_norm_w, bsz, nc):
    m, d = h1.shape
    per = nc - 1
    n_steps = bsz * per
    eye = jnp.eye(BLOCK, dtype=F32)

    def tok_blk(s):
        s = jnp.clip(s, 0, n_steps - 1)
        return (s // per) * nc + 1 + s % per

    return pl.pallas_call(
        functools.partial(_combine_kernel, n_steps=n_steps),
        grid=(n_steps + 1,),
        in_specs=[
            pl.BlockSpec((SUBLANES, BLOCK), lambda s: (0, tok_blk(s)), memory_space=pltpu.SMEM),
            pl.BlockSpec((SUBLANES, BLOCK), lambda s: (0, tok_blk(s - 1))),
            pl.BlockSpec((BLOCK, d), lambda s: (tok_blk(s - 1), 0)),
            pl.BlockSpec((1, d), lambda s: (0, 0)),
            pl.BlockSpec((BLOCK, BLOCK), lambda s: (0, 0)),
            pl.BlockSpec(memory_space=pl.ANY),
        ],
        out_specs=pl.BlockSpec((BLOCK, d), lambda s: (jnp.clip(s - 1, 0, n_steps - 1), 0)),
        out_shape=jax.ShapeDtypeStruct((n_steps * BLOCK, d), F32),
        scratch_shapes=[
            pltpu.VMEM((2, TOP_K * BLOCK * SUBLANES, LANES), F32),
            pltpu.SemaphoreType.DMA((2,)),
        ],
        compiler_params=pltpu.CompilerParams(
            dimension_semantics=("arbitrary",), vmem_limit_bytes=VMEM_LIMIT),
        name="combine",
    )(dest, gates, h1, final_norm_w[None, :], eye, ypad)


def _layer(h, bsz, nc, mix_norm_w, w_in, conv_w, conv_b, dt_bias, a_log, d_skip, ssd_norm_w,
           w_ssd_out, w_att_out, w_out, ffn_norm_w, w_router, b_router, w_gate_up, b_gate_up,
           w_down, b_down):
    m, d = h.shape
    seq = nc * BLOCK
    n_heads = a_log.shape[0]
    d_inner = n_heads * SSD_HEADDIM
    conv_dim = d_inner + 2 * SSD_GROUPS * SSD_STATE
    att_dim = w_att_out.shape[0]
    n_exp = w_router.shape[1]

    o_xbc = d_inner
    o_dt = o_xbc + conv_dim
    o_q = o_dt + n_heads
    w_main = jnp.concatenate([w_in[:, :o_dt], w_in[:, o_q:]], axis=1).astype(BF16)
    w_dt = jnp.pad(w_in[:, o_dt:o_q], ((0, 0), (0, LANES - n_heads))).astype(BF16)
    q_col = o_dt
    k_col = q_col + att_dim
    v_col = k_col + att_dim
    gs_col = v_col + att_dim
    ga_col = gs_col + d

    proj, dt_raw = _inproj(h, mix_norm_w[None, :], w_main, w_dt)
    yssd = _ssd(proj, dt_raw, conv_w, conv_b, dt_bias, a_log, d_skip, ssd_norm_w, bsz, nc, d_inner)
    att = _attention(proj, bsz, nc, q_col, k_col, v_col, att_dim)
    h1 = _merge(yssd, att, proj, h, w_ssd_out.astype(BF16), w_att_out.astype(BF16),
                w_out.astype(BF16), bsz, seq, gs_col, ga_col)

    ut, idx, gates, rank, cnt = _router(h1, ffn_norm_w[None, :], w_router, b_router)

    counts = cnt[:, 0].astype(jnp.int32)
    padded = (counts + MOE_BLOCK - 1) // MOE_BLOCK * MOE_BLOCK
    padded_ends = jnp.cumsum(padded)
    padded_starts = padded_ends - padded
    n_blocks = (m * TOP_K + MOE_BLOCK - 1) // MOE_BLOCK + n_exp
    dest = rank + jnp.sum(
        jnp.where(idx[None] == jnp.arange(n_exp, dtype=jnp.int32)[:, None, None],
                  padded_starts[:, None, None], 0), axis=0)
    block_first = jnp.arange(n_blocks, dtype=jnp.int32) * MOE_BLOCK
    block_expert = jnp.minimum(
        jnp.sum((padded_ends[None, :] <= block_first[:, None]).astype(jnp.int32), axis=1), n_exp - 1)
    n_used = (padded_ends[-1:] // MOE_BLOCK).astype(jnp.int32)
    zstart = (padded_starts + counts).astype(jnp.int32)

    zpad = (padded - counts).astype(jnp.int32)
    xs = _dispatch(ut, dest, zstart, zpad, n_used, n_blocks * MOE_BLOCK)
    b_gate = b_gate_up[:, None, 0::2]
    b_lin = b_gate_up[:, None, 1::2]
    ypad = _experts(xs, block_expert, n_used, w_gate_up, b_gate, b_lin, w_down, b_down[:, None, :],
                    n_blocks)
    return h1, ypad, dest, gates


def kernel(x, meta_tokens, mix_norm_w, w_in, conv_w, conv_b, dt_bias, a_log, d_skip, ssd_norm_w,
           w_ssd_out, w_att_out, w_out, ffn_norm_w, w_router, b_router, w_gate_up, b_gate_up,
           w_down, b_down, final_norm_w):
    bsz, seq_x, d = x.shape
    depth = mix_norm_w.shape[0]
    assert depth == 1 and seq_x % BLOCK == 0 and (seq_x + N_LEAD) % ROW_TILE == 0
    nc = (seq_x + N_LEAD) // BLOCK
    lead = jnp.concatenate([jnp.zeros((N_PAD, d), x.dtype), meta_tokens.astype(x.dtype)], axis=0)
    h = jnp.concatenate([jnp.broadcast_to(lead[None], (bsz, N_LEAD, d)), x], axis=1)
    h = h.reshape(bsz * nc * BLOCK, d)
    layer = 0
    h1, ypad, dest, gates = _layer(
        h, bsz, nc, mix_norm_w[layer], w_in[layer], conv_w[layer], conv_b[layer], dt_bias[layer],
        a_log[layer], d_skip[layer], ssd_norm_w[layer], w_ssd_out[layer], w_att_out[layer],
        w_out[layer], ffn_norm_w[layer], w_router[layer], b_router[layer], w_gate_up[layer],
        b_gate_up[layer], w_down[layer], b_down[layer])
    out = _combine(ypad, dest, gates, h1, final_norm_w, bsz, nc)
    return out.reshape(bsz, seq_x, d)
```

```python
import functools

import jax
import jax.numpy as jnp
from jax import lax
from jax.experimental import pallas as pl
from jax.experimental.pallas import tpu as pltpu

F32 = jnp.float32
BF16 = jnp.bfloat16
HIGHEST = lax.Precision.HIGHEST

N_META = 16
BLOCK = 128
N_LEAD = BLOCK
N_PAD = N_LEAD - N_META
EPS = 1e-5
SSD_HEADDIM = 64
SSD_GROUPS = 4
SSD_STATE = 128
CONV_K = 4
ATT_HEADDIM = 64
TOP_K = 4
SWIGLU_LIMIT = 7.0
SWIGLU_ALPHA = 1.702
MOE_BLOCK = 256

LANES = 128
SUBLANES = 8
ROW_TILE = 640
INPROJ_ROWS = 1664
EXP_UNDERFLOW = -88.0
VMEM_LIMIT = 56 * 1024 * 1024


def _nt_dot(a, b, precision=None):
    return lax.dot_general(a, b, (((1,), (1,)), ((), ())),
                           preferred_element_type=F32, precision=precision)


def _tn_dot(a, b):
    return lax.dot_general(a, b, (((0,), (0,)), ((), ())), preferred_element_type=F32)


def _silu(x):
    return x * jax.nn.sigmoid(x)


def _inproj_kernel(h_ref, nw_ref, w_ref, wdt_ref, out_ref, dt_ref, u_scr):
    @pl.when(pl.program_id(1) == 0)
    def _():
        x = h_ref[...]
        u = x * lax.rsqrt(jnp.mean(x * x, axis=-1, keepdims=True) + EPS) * nw_ref[...]
        ub = u.astype(BF16)
        u_scr[...] = ub
        dt_ref[...] = jnp.dot(ub, wdt_ref[...], preferred_element_type=F32)

    out_ref[...] = jnp.dot(u_scr[...], w_ref[...], preferred_element_type=F32).astype(out_ref.dtype)


def _largest_row_tile(m, cap):
    return max(t for t in range(BLOCK, cap + 1, BLOCK) if m % t == 0)


def _inproj(h, norm_w, w_main, w_dt, tn=1024):
    m, d = h.shape
    n = w_main.shape[1]
    tm = _largest_row_tile(m, INPROJ_ROWS)
    return pl.pallas_call(
        _inproj_kernel,
        grid=(m // tm, n // tn),
        in_specs=[
            pl.BlockSpec((tm, d), lambda i, j: (i, 0)),
            pl.BlockSpec((1, d), lambda i, j: (0, 0)),
            pl.BlockSpec((d, tn), lambda i, j: (0, j)),
            pl.BlockSpec((d, LANES), lambda i, j: (0, 0)),
        ],
        out_specs=[
            pl.BlockSpec((tm, tn), lambda i, j: (i, j)),
            pl.BlockSpec((tm, LANES), lambda i, j: (i, 0)),
        ],
        out_shape=[jax.ShapeDtypeStruct((m, n), BF16), jax.ShapeDtypeStruct((m, LANES), F32)],
        scratch_shapes=[pltpu.VMEM((tm, d), BF16)],
        compiler_params=pltpu.CompilerParams(
            dimension_semantics=("parallel", "arbitrary"), vmem_limit_bytes=VMEM_LIMIT),
        name="inproj",
    )(h, norm_w, w_main, w_dt)


def _ssd_kernel(z_ref, xs_ref, bc_ref, dt_ref, cwx_ref, cbx_ref, cwb_ref, cbb_ref,
                dtb_ref, alog_ref, dskip_ref, nw_ref, expand_ref, tril_ref,
                out_ref, extx_scr, extb_scr, state_scr, *, n_heads):
    c = pl.program_id(1)
    d_inner = xs_ref.shape[1]
    gw = d_inner // SSD_GROUPS
    hpg = n_heads // SSD_GROUPS

    @pl.when(c == 0)
    def _():
        extx_scr[0:SUBLANES, :] = jnp.zeros((SUBLANES, extx_scr.shape[1]), F32)
        extb_scr[0:SUBLANES, :] = jnp.zeros((SUBLANES, extb_scr.shape[1]), F32)
        state_scr[...] = jnp.zeros(state_scr.shape, F32)

    @pl.when(c > 0)
    def _():
        extx_scr[0:SUBLANES, :] = extx_scr[BLOCK:BLOCK + SUBLANES, :]
        extb_scr[0:SUBLANES, :] = extb_scr[BLOCK:BLOCK + SUBLANES, :]

    extx_scr[SUBLANES:SUBLANES + BLOCK, :] = xs_ref[...].astype(F32)
    extb_scr[SUBLANES:SUBLANES + BLOCK, :] = bc_ref[...].astype(F32)

    row = lax.broadcasted_iota(jnp.int32, (BLOCK, 1), 0)
    vmask = (c * BLOCK + row >= N_PAD).astype(F32)

    def conv(ext_scr, w_ref, b_ref):
        acc = b_ref[...]
        for i in range(CONV_K):
            start = SUBLANES - (CONV_K - 1) + i
            acc = acc + w_ref[i:i + 1, :] * ext_scr[start:start + BLOCK, :]
        return _silu(acc) * vmask

    xs = conv(extx_scr, cwx_ref, cbx_ref)
    bc = conv(extb_scr, cwb_ref, cbb_ref)
    gn = SSD_GROUPS * SSD_STATE
    b_all = bc[:, :gn].astype(BF16)
    c_all = bc[:, gn:].astype(BF16)

    dt = jax.nn.softplus(dt_ref[:, :n_heads] + dtb_ref[...]) * vmask
    a = -jnp.exp(alog_ref[...]) * dt
    a_cs = jnp.dot(tril_ref[...], a, preferred_element_type=F32, precision=HIGHEST)
    a_cs_t = a_cs.T
    a_last = a_cs[BLOCK - 1:BLOCK, :]
    decay_states = jnp.exp(a_last - a_cs)
    decay_out = jnp.exp(a_cs)
    chunk_decay = jnp.broadcast_to(jnp.exp(a_last), (SUBLANES, n_heads))
    stacked = jnp.concatenate([dt, decay_states, decay_out, chunk_decay], axis=0)
    st_hi = stacked.astype(BF16)
    st_lo = (stacked - st_hi.astype(F32)).astype(BF16)
    expanded = (jnp.dot(st_hi, expand_ref[...], preferred_element_type=F32)
                + jnp.dot(st_lo, expand_ref[...], preferred_element_type=F32))
    dt_x = expanded[0:BLOCK]
    ds_x = expanded[BLOCK:2 * BLOCK]
    do_x = expanded[2 * BLOCK:3 * BLOCK]
    cd_x = expanded[3 * BLOCK:3 * BLOCK + 1]

    x_dt = xs * dt_x
    x_dt_b = x_dt.astype(BF16)
    x_ds_b = (x_dt * ds_x).astype(BF16)

    li = lax.broadcasted_iota(jnp.int32, (BLOCK, BLOCK), 0)
    si = lax.broadcasted_iota(jnp.int32, (BLOCK, BLOCK), 1)
    causal = si <= li
    first_half = lax.broadcasted_iota(jnp.int32, (BLOCK, 2 * SSD_HEADDIM), 1) < SSD_HEADDIM

    y_groups = []
    for g in range(SSD_GROUPS):
        bg = b_all[:, g * SSD_STATE:(g + 1) * SSD_STATE]
        cg = c_all[:, g * SSD_STATE:(g + 1) * SSD_STATE]
        cols = slice(g * gw, (g + 1) * gw)
        cb = _nt_dot(cg, bg)
        prev = state_scr[g]
        y_off = jnp.dot(cg, prev.astype(BF16), preferred_element_type=F32) * do_x[:, cols]
        new_state = _tn_dot(bg, x_ds_b[:, cols])
        state_scr[g] = cd_x[:, cols] * prev + new_state
        pairs = []
        for jp in range(hpg // 2):
            h0 = g * hpg + 2 * jp
            pc = slice(g * gw + jp * 2 * SSD_HEADDIM, g * gw + (jp + 1) * 2 * SSD_HEADDIM)
            xp = x_dt_b[:, pc]
            ys = []
            for h in (h0, h0 + 1):
                seg = a_cs[:, h:h + 1] - a_cs_t[h:h + 1, :]
                m = (cb * jnp.exp(jnp.where(causal, seg, -jnp.inf))).astype(BF16)
                ys.append(jnp.dot(m, xp, preferred_element_type=F32))
            pairs.append(jnp.where(first_half, ys[0], ys[1]))
        y_groups.append(jnp.concatenate(pairs, axis=1) + y_off)
    y = jnp.concatenate(y_groups, axis=1) + dskip_ref[...] * xs

    gated = y * _silu(z_ref[...].astype(F32))
    outs = []
    for g in range(SSD_GROUPS):
        gg = gated[:, g * gw:(g + 1) * gw]
        outs.append(gg * lax.rsqrt(jnp.mean(gg * gg, axis=-1, keepdims=True) + EPS))
    out_ref[...] = (jnp.concatenate(outs, axis=1) * nw_ref[...]).astype(out_ref.dtype)


def _ssd(proj, dt_raw, conv_w, conv_b, dt_bias, a_log, d_skip, norm_w, bsz, nc, d_inner):
    m = proj.shape[0]
    n_heads = a_log.shape[0]
    bc_w = 2 * SSD_GROUPS * SSD_STATE
    gw = d_inner // SSD_GROUPS
    cwx, cwb = conv_w[:, :d_inner], conv_w[:, d_inner:]
    cbx, cbb = conv_b[None, :d_inner], conv_b[None, d_inner:]
    expand = jnp.repeat(jnp.eye(n_heads, dtype=BF16), SSD_HEADDIM, axis=1)
    tril = jnp.tril(jnp.ones((BLOCK, BLOCK), F32))
    dskip_x = jnp.repeat(d_skip.astype(F32), SSD_HEADDIM)[None, :]
    full = lambda shape: pl.BlockSpec(shape, lambda b, c: (0,) * len(shape))
    row_blk = lambda b, c: b * nc + c
    return pl.pallas_call(
        functools.partial(_ssd_kernel, n_heads=n_heads),
        grid=(bsz, nc),
        in_specs=[
            pl.BlockSpec((BLOCK, d_inner), lambda b, c: (row_blk(b, c), 0)),
            pl.BlockSpec((BLOCK, d_inner), lambda b, c: (row_blk(b, c), 1)),
            pl.BlockSpec((BLOCK, bc_w), lambda b, c: (row_blk(b, c), 2 * d_inner // bc_w)),
            pl.BlockSpec((BLOCK, LANES), lambda b, c: (row_blk(b, c), 0)),
            full((CONV_K, d_inner)), full((1, d_inner)), full((CONV_K, bc_w)), full((1, bc_w)),
            full((1, n_heads)), full((1, n_heads)), full((1, d_inner)), full((1, d_inner)),
            full((n_heads, d_inner)), full((BLOCK, BLOCK)),
        ],
        out_specs=pl.BlockSpec((BLOCK, d_inner), lambda b, c: (row_blk(b, c), 0)),
        out_shape=jax.ShapeDtypeStruct((m, d_inner), BF16),
        scratch_shapes=[
            pltpu.VMEM((BLOCK + SUBLANES, d_inner), F32),
            pltpu.VMEM((BLOCK + SUBLANES, bc_w), F32),
            pltpu.VMEM((SSD_GROUPS, SSD_STATE, gw), F32),
        ],
        compiler_params=pltpu.CompilerParams(
            dimension_semantics=("parallel", "arbitrary"), vmem_limit_bytes=VMEM_LIMIT),
        name="ssd",
    )(proj, proj, proj, dt_raw, cwx, cbx, cwb, cbb, dt_bias[None, :].astype(F32),
      a_log[None, :].astype(F32), dskip_x, norm_w[None, :].astype(F32), expand, tril)


def _attn_kernel(q_ref, k_ref, v_ref, upper_ref, out_ref, *, n_pairs):
    i = pl.program_id(2)
    pw = 2 * ATT_HEADDIM
    lane = lax.broadcasted_iota(jnp.int32, (BLOCK, pw), 1)
    first = lane < ATT_HEADDIM
    row2 = lax.broadcasted_iota(jnp.int32, (2 * BLOCK, BLOCK), 0)
    qpos = i * BLOCK + jnp.where(row2 < BLOCK, row2, row2 - BLOCK)
    kin = lax.broadcasted_iota(jnp.int32, (2 * BLOCK, BLOCK), 1)
    upper = upper_ref[...]
    scale = ATT_HEADDIM ** -0.5
    q2 = []
    for p in range(n_pairs):
        q = q_ref[:, p * pw:(p + 1) * pw] * jnp.asarray(scale, BF16)
        zero = jnp.zeros_like(q)
        q2.append(jnp.concatenate([jnp.where(first, q, zero), jnp.where(first, zero, q)], axis=0))

    def cond(carry):
        kb, top = carry[0], carry[1]
        return jnp.logical_and(kb >= 0, top > EXP_UNDERFLOW)

    def body(carry):
        kb = carry[0]
        runs, accs = carry[2:2 + n_pairs], carry[2 + n_pairs:]
        start = pl.multiple_of(kb * BLOCK, BLOCK)
        kpos = kb * BLOCK + kin
        allowed = jnp.logical_and(kpos < qpos, kpos >= N_PAD)
        zs = [_nt_dot(q2[p], k_ref[pl.ds(start, BLOCK), p * pw:(p + 1) * pw])
              for p in range(n_pairs)]
        log_betas, laters, new_runs, new_accs = [], [], [], []
        for p in range(n_pairs):
            z = zs[p]
            l1p = jnp.log(1.0 + jnp.exp(-jnp.abs(z)))
            log_beta = jnp.minimum(z, 0.0) - l1p
            log_betas.append(log_beta)
            log_stay = jnp.where(allowed, log_beta - z, 0.0)
            hi = log_stay.astype(BF16)
            lo = (log_stay - hi.astype(F32)).astype(BF16)
            laters.append(jnp.dot(hi, upper, preferred_element_type=F32)
                          + jnp.dot(lo, upper, preferred_element_type=F32))
            new_runs.append(runs[p] + jnp.sum(log_stay, axis=1, keepdims=True))
        for p in range(n_pairs):
            vblk = v_ref[pl.ds(start, BLOCK), p * pw:(p + 1) * pw]
            w = jnp.where(allowed, jnp.exp(log_betas[p] + laters[p] + runs[p]), 0.0).astype(BF16)
            vzero = jnp.zeros_like(vblk)
            new_accs.append(
                accs[p]
                + jnp.dot(w[:BLOCK], jnp.where(first, vblk, vzero), preferred_element_type=F32)
                + jnp.dot(w[BLOCK:], jnp.where(first, vzero, vblk), preferred_element_type=F32))
        top = jnp.max(new_runs[0])
        for r in new_runs[1:]:
            top = jnp.maximum(top, jnp.max(r))
        return (kb - 1, top, *new_runs, *new_accs)

    init = ((i, jnp.float32(0.0))
            + tuple(jnp.zeros((2 * BLOCK, 1), F32) for _ in range(n_pairs))
            + tuple(jnp.zeros((BLOCK, pw), F32) for _ in range(n_pairs)))
    final = lax.while_loop(cond, body, init)
    for p in range(n_pairs):
        out_ref[:, p * pw:(p + 1) * pw] = final[2 + n_pairs + p].astype(out_ref.dtype)


def _attention(proj, bsz, nc, q_col, k_col, v_col, att_dim, pairs_per_step=4):
    m = proj.shape[0]
    seq = nc * BLOCK
    sw = pairs_per_step * 2 * ATT_HEADDIM
    upper = jnp.triu(jnp.ones((BLOCK, BLOCK), F32), 1).T.astype(BF16)
    return pl.pallas_call(
        functools.partial(_attn_kernel, n_pairs=pairs_per_step),
        grid=(bsz, att_dim // sw, nc),
        in_specs=[
            pl.BlockSpec((BLOCK, sw), lambda b, p, i: (b * nc + i, q_col // sw + p)),
            pl.BlockSpec((seq, sw), lambda b, p, i: (b, k_col // sw + p)),
            pl.BlockSpec((seq, sw), lambda b, p, i: (b, v_col // sw + p)),
            pl.BlockSpec((BLOCK, BLOCK), lambda b, p, i: (0, 0)),
        ],
        out_specs=pl.BlockSpec((BLOCK, sw), lambda b, p, i: (b * nc + i, p)),
        out_shape=jax.ShapeDtypeStruct((m, att_dim), BF16),
        compiler_params=pltpu.CompilerParams(
            dimension_semantics=("parallel", "parallel", "arbitrary"),
            vmem_limit_bytes=VMEM_LIMIT),
        name="attention",
    )(proj, proj, proj, upper)


def _merge_kernel(yssd_ref, att_ref, gs_ref, ga_ref, h_ref, wso_ref, wao_ref, wo_ref, out_ref):
    t = pl.program_id(1)
    y_ssd = jnp.dot(yssd_ref[...], wso_ref[...], preferred_element_type=F32)
    y_att = jnp.dot(att_ref[...], wao_ref[...], preferred_element_type=F32)
    merged = (jax.nn.sigmoid(gs_ref[...].astype(F32)) * y_ssd
              + jax.nn.sigmoid(ga_ref[...].astype(F32)) * y_att)
    mixed = jnp.dot(merged.astype(BF16), wo_ref[...], preferred_element_type=F32)
    row = lax.broadcasted_iota(jnp.int32, (ROW_TILE, 1), 0)
    valid = (t * ROW_TILE + row >= N_PAD).astype(F32)
    out_ref[...] = h_ref[...] + mixed * valid


def _merge(yssd, att, proj, h, w_ssd_out, w_att_out, w_out, bsz, seq, gs_col, ga_col):
    m, d = h.shape
    d_inner = yssd.shape[1]
    att_dim = att.shape[1]
    per = seq // ROW_TILE
    rb = lambda b, t: b * per + t
    full = lambda shape: pl.BlockSpec(shape, lambda b, t: (0,) * len(shape))
    return pl.pallas_call(
        _merge_kernel,
        grid=(bsz, per),
        in_specs=[
            pl.BlockSpec((ROW_TILE, d_inner), lambda b, t: (rb(b, t), 0)),
            pl.BlockSpec((ROW_TILE, att_dim), lambda b, t: (rb(b, t), 0)),
            pl.BlockSpec((ROW_TILE, d), lambda b, t: (rb(b, t), gs_col // d)),
            pl.BlockSpec((ROW_TILE, d), lambda b, t: (rb(b, t), ga_col // d)),
            pl.BlockSpec((ROW_TILE, d), lambda b, t: (rb(b, t), 0)),
            full((d_inner, d)), full((att_dim, d)), full((d, d)),
        ],
        out_specs=pl.BlockSpec((ROW_TILE, d), lambda b, t: (rb(b, t), 0)),
        out_shape=jax.ShapeDtypeStruct((m, d), F32),
        compiler_params=pltpu.CompilerParams(
            dimension_semantics=("parallel", "parallel"), vmem_limit_bytes=VMEM_LIMIT),
        name="merge",
    )(yssd, att, proj, proj, h, w_ssd_out, w_att_out, w_out)


def _router_kernel(h_ref, nw_ref, wr_ref, br_ref, before_ref,
                   ut_ref, idx_ref, gate_ref, rank_ref, cnt_ref, base_scr):
    step = pl.program_id(0)
    n_exp = wr_ref.shape[0]
    tm = h_ref.shape[0]

    @pl.when(step == 0)
    def _():
        base_scr[...] = jnp.zeros(base_scr.shape, F32)

    x = h_ref[...]
    u = x * lax.rsqrt(jnp.mean(x * x, axis=-1, keepdims=True) + EPS) * nw_ref[...]
    for c in range(SUBLANES):
        ut_ref[pl.ds(c, tm, stride=SUBLANES), :] = u[:, c * LANES:(c + 1) * LANES]

    logits = _nt_dot(wr_ref[...], u, precision=HIGHEST) + br_ref[...]
    eidx = lax.broadcasted_iota(jnp.int32, (n_exp, tm), 0)
    work = logits
    tops, idxs, onehots = [], [], []
    for _ in range(TOP_K):
        top = jnp.max(work, axis=0, keepdims=True)
        idx = jnp.min(jnp.where(work == top, eidx, n_exp), axis=0, keepdims=True)
        hot = eidx == idx
        work = jnp.where(hot, -jnp.inf, work)
        tops.append(top)
        idxs.append(idx)
        onehots.append(hot.astype(F32))
    exps = [jnp.exp(t - tops[0]) for t in tops]
    denom = exps[0] + exps[1] + exps[2] + exps[3]
    cnt = onehots[0] + onehots[1] + onehots[2] + onehots[3]
    before = jnp.dot(cnt.astype(BF16), before_ref[...], preferred_element_type=F32) + base_scr[...]
    ranks = [jnp.sum(hot * before, axis=0, keepdims=True) for hot in onehots]
    pad_rows = SUBLANES - TOP_K
    idx_ref[...] = jnp.concatenate(idxs + [jnp.zeros((pad_rows, tm), jnp.int32)], axis=0)
    gate_ref[...] = jnp.concatenate([e / denom for e in exps] + [jnp.zeros((pad_rows, tm), F32)], axis=0)
    rank_ref[...] = jnp.concatenate(
        [r.astype(jnp.int32) for r in ranks] + [jnp.zeros((pad_rows, tm), jnp.int32)], axis=0)
    base_scr[...] = base_scr[...] + jnp.sum(cnt, axis=1, keepdims=True)
    cnt_ref[...] = jnp.broadcast_to(base_scr[...], cnt_ref.shape)


def _router(h1, norm_w, w_router, b_router):
    m, d = h1.shape
    n_exp = w_router.shape[1]
    before = jnp.triu(jnp.ones((ROW_TILE, ROW_TILE), F32), 1).astype(BF16)
    full = lambda shape: pl.BlockSpec(shape, lambda i: (0,) * len(shape))
    return pl.pallas_call(
        _router_kernel,
        grid=(m // ROW_TILE,),
        in_specs=[
            pl.BlockSpec((ROW_TILE, d), lambda i: (i, 0)),
            full((1, d)), full((n_exp, d)), full((n_exp, 1)), full((ROW_TILE, ROW_TILE)),
        ],
        out_specs=[
            pl.BlockSpec((ROW_TILE * SUBLANES, LANES), lambda i: (i, 0)),
            pl.BlockSpec((SUBLANES, ROW_TILE), lambda i: (0, i)),
            pl.BlockSpec((SUBLANES, ROW_TILE), lambda i: (0, i)),
            pl.BlockSpec((SUBLANES, ROW_TILE), lambda i: (0, i)),
            full((n_exp, LANES)),
        ],
        out_shape=[
            jax.ShapeDtypeStruct((m * SUBLANES, LANES), F32),
            jax.ShapeDtypeStruct((SUBLANES, m), jnp.int32),
            jax.ShapeDtypeStruct((SUBLANES, m), F32),
            jax.ShapeDtypeStruct((SUBLANES, m), jnp.int32),
            jax.ShapeDtypeStruct((n_exp, LANES), F32),
        ],
        scratch_shapes=[pltpu.VMEM((n_exp, 1), F32)],
        compiler_params=pltpu.CompilerParams(
            dimension_semantics=("arbitrary",), vmem_limit_bytes=VMEM_LIMIT),
        name="router",
    )(h1, norm_w, w_router.T, b_router[:, None], before)


def _dispatch_kernel(zstart_ref, zpad_ref, nused_ref, dest_ref, ut_ref, xs_ref, zero_scr, sem, zsem,
                     *, n_exp):
    step = pl.program_id(0)
    tb = dest_ref.shape[1]
    slot_rows = MOE_BLOCK * SUBLANES
    n_blocks = xs_ref.shape[0] // slot_rows

    @pl.when(step == 0)
    def _():
        zero_scr[...] = jnp.zeros(zero_scr.shape, F32)

        def zero_rows(row_start, n_rows):
            return pltpu.make_async_copy(zero_scr.at[pl.ds(0, n_rows)],
                                         xs_ref.at[pl.ds(row_start, n_rows)], zsem)

        def pad_fill(wait):
            def per_expert(e, carry):
                n_pad = zpad_ref[e]
                pos = zstart_ref[e]
                size = MOE_BLOCK // 2
                while size >= 1:
                    take = n_pad & size

                    @pl.when(take != 0)
                    def _(pos=pos, size=size):
                        copy = zero_rows(pl.multiple_of(pos * SUBLANES, SUBLANES), size * SUBLANES)
                        copy.wait() if wait else copy.start()

                    pos = pos + take
                    size //= 2
                return carry

            lax.fori_loop(0, n_exp, per_expert, 0)

        def tail_fill(wait):
            def per_block(b, carry):
                copy = zero_rows(pl.multiple_of(b * slot_rows, slot_rows), slot_rows)
                copy.wait() if wait else copy.start()
                return carry

            lax.fori_loop(nused_ref[0], n_blocks, per_block, 0)

        pad_fill(False)
        tail_fill(False)
        pad_fill(True)
        tail_fill(True)

    def issue(t, carry):
        src = pl.multiple_of(t * SUBLANES, SUBLANES)
        for k in range(TOP_K):
            dst = pl.multiple_of(dest_ref[k, t] * SUBLANES, SUBLANES)
            pltpu.make_async_copy(ut_ref.at[pl.ds(src, SUBLANES)],
                                  xs_ref.at[pl.ds(dst, SUBLANES)], sem).start()
        return carry

    lax.fori_loop(0, tb, issue, 0, unroll=4)
    for _ in range(TOP_K):
        pltpu.make_async_copy(ut_ref, xs_ref.at[pl.ds(0, tb * SUBLANES)], sem).wait()


def _dispatch(ut, dest, zstart, zpad, n_used, n_slots):
    m = dest.shape[1]
    n_exp = zstart.shape[0]
    return pl.pallas_call(
        functools.partial(_dispatch_kernel, n_exp=n_exp),
        grid_spec=pltpu.PrefetchScalarGridSpec(
            num_scalar_prefetch=3,
            grid=(m // ROW_TILE,),
            in_specs=[
                pl.BlockSpec((SUBLANES, ROW_TILE), lambda i, *_: (0, i), memory_space=pltpu.SMEM),
                pl.BlockSpec((ROW_TILE * SUBLANES, LANES), lambda i, *_: (i, 0)),
            ],
            out_specs=pl.BlockSpec(memory_space=pl.ANY),
            scratch_shapes=[
                pltpu.VMEM((MOE_BLOCK * SUBLANES, LANES), F32),
                pltpu.SemaphoreType.DMA(()),
                pltpu.SemaphoreType.DMA(()),
            ],
        ),
        out_shape=jax.ShapeDtypeStruct((n_slots * SUBLANES, LANES), F32),
        compiler_params=pltpu.CompilerParams(
            dimension_semantics=("arbitrary",), vmem_limit_bytes=VMEM_LIMIT),
        name="dispatch",
    )(zstart, zpad, n_used, dest, ut)


def _expert_kernel(be_ref, nused_ref, xs_ref, wgu_ref, bg_ref, bl_ref, wd_ref, bd_ref, perm_ref,
                   y_ref, wg_scr, wl_scr, wd_scr):
    b = pl.program_id(0)
    pair = 2 * LANES
    changed = jnp.logical_or(b == 0, be_ref[b] != be_ref[jnp.maximum(b - 1, 0)])

    @pl.when(jnp.logical_and(changed, b < nused_ref[0]))
    def _():
        for cb in range(wgu_ref.shape[2] // pair):
            wb = wgu_ref[0, :, cb * pair:(cb + 1) * pair].astype(BF16)
            sep = jnp.dot(wb, perm_ref[...], preferred_element_type=F32).astype(BF16)
            wg_scr[:, cb * LANES:(cb + 1) * LANES] = sep[:, :LANES]
            wl_scr[:, cb * LANES:(cb + 1) * LANES] = sep[:, LANES:]
        wd_scr[...] = wd_ref[0].astype(BF16)

    @pl.when(b < nused_ref[0])
    def _():
        x = jnp.concatenate(
            [xs_ref[pl.ds(c, MOE_BLOCK, stride=SUBLANES), :] for c in range(SUBLANES)],
            axis=1).astype(BF16)
        gate = jnp.dot(x, wg_scr[...], preferred_element_type=F32) + bg_ref[0]
        lin = jnp.dot(x, wl_scr[...], preferred_element_type=F32) + bl_ref[0]
        gate = jnp.minimum(gate, SWIGLU_LIMIT)
        lin = jnp.clip(lin, -SWIGLU_LIMIT, SWIGLU_LIMIT)
        act = gate * jax.nn.sigmoid(SWIGLU_ALPHA * gate) * (lin + 1.0)
        y = jnp.dot(act.astype(BF16), wd_scr[...], preferred_element_type=F32) + bd_ref[0]
        for c in range(SUBLANES):
            y_ref[pl.ds(c, MOE_BLOCK, stride=SUBLANES), :] = y[:, c * LANES:(c + 1) * LANES]

    @pl.when(b >= nused_ref[0])
    def _():
        y_ref[...] = jnp.zeros(y_ref.shape, F32)


def _experts(xs, block_expert, n_used, w_gate_up, b_gate, b_lin, w_down, b_down, n_blocks):
    n_exp, d, d_ff2 = w_gate_up.shape
    d_ff = d_ff2 // 2
    rows = MOE_BLOCK * SUBLANES
    blk = lambda b, be, nu: (jnp.minimum(b, nu[0] - 1), 0)
    wsel = lambda b, be, nu: (be[b], 0, 0)
    r = jnp.arange(2 * LANES)[:, None]
    c = jnp.arange(2 * LANES)[None, :]
    perm = (r == jnp.where(c < LANES, 2 * c, 2 * (c - LANES) + 1)).astype(BF16)
    return pl.pallas_call(
        _expert_kernel,
        grid_spec=pltpu.PrefetchScalarGridSpec(
            num_scalar_prefetch=2,
            grid=(n_blocks,),
            in_specs=[
                pl.BlockSpec((rows, LANES), blk),
                pl.BlockSpec((1, d, d_ff2), wsel),
                pl.BlockSpec((1, 1, d_ff), wsel), pl.BlockSpec((1, 1, d_ff), wsel),
                pl.BlockSpec((1, d_ff, d), wsel), pl.BlockSpec((1, 1, d), wsel),
                pl.BlockSpec((2 * LANES, 2 * LANES), lambda b, be, nu: (0, 0)),
            ],
            out_specs=pl.BlockSpec((rows, LANES), lambda b, be, nu: (b, 0)),
            scratch_shapes=[
                pltpu.VMEM((d, d_ff), BF16), pltpu.VMEM((d, d_ff), BF16), pltpu.VMEM((d_ff, d), BF16),
            ],
        ),
        out_shape=jax.ShapeDtypeStruct((n_blocks * rows, LANES), F32),
        compiler_params=pltpu.CompilerParams(
            dimension_semantics=("arbitrary",), vmem_limit_bytes=VMEM_LIMIT),
        name="experts",
    )(block_expert, n_used, xs, w_gate_up, b_gate, b_lin, w_down, b_down, perm)


def _combine_kernel(dest_ref, gate_ref, h_ref, nw_ref, eye_ref, ypad_ref, out_ref, buf, sem, *, n_steps):
    s = pl.program_id(0)
    rows = BLOCK * SUBLANES

    @pl.when(s < n_steps)
    def _():
        slot = s % 2

        def issue(t, carry):
            for k in range(TOP_K):
                src = pl.multiple_of(dest_ref[k, t] * SUBLANES, SUBLANES)
                dst = pl.multiple_of(k * rows + t * SUBLANES, SUBLANES)
                pltpu.make_async_copy(ypad_ref.at[pl.ds(src, SUBLANES)],
                                      buf.at[slot, pl.ds(dst, SUBLANES)], sem.at[slot]).start()
            return carry

        lax.fori_loop(0, BLOCK, issue, 0)

    @pl.when(s > 0)
    def _():
        slot = (s - 1) % 2
        pltpu.make_async_copy(ypad_ref.at[pl.ds(0, TOP_K * rows)], buf.at[slot], sem.at[slot]).wait()
        gates_t = _nt_dot(eye_ref[...], gate_ref[...], precision=HIGHEST)
        acc = h_ref[...]
        for k in range(TOP_K):
            yk = jnp.concatenate(
                [buf[slot, pl.ds(k * rows + c, BLOCK, stride=SUBLANES), :] for c in range(SUBLANES)],
                axis=1)
            acc = acc + gates_t[:, k:k + 1] * yk
        out = acc * lax.rsqrt(jnp.mean(acc * acc, axis=-1, keepdims=True) + EPS) * nw_ref[...]
        out_ref[...] = out


def _combine(ypad, dest, gates, h1, final_norm_w, bsz, nc):
    m, d = h1.shape
    per = nc - 1
    n_steps = bsz * per
    eye = jnp.eye(BLOCK, dtype=F32)

    def tok_blk(s):
        s = jnp.clip(s, 0, n_steps - 1)
        return (s // per) * nc + 1 + s % per

    return pl.pallas_call(
        functools.partial(_combine_kernel, n_steps=n_steps),
        grid=(n_steps + 1,),
        in_specs=[
            pl.BlockSpec((SUBLANES, BLOCK), lambda s: (0, tok_blk(s)), memory_space=pltpu.SMEM),
            pl.BlockSpec((SUBLANES, BLOCK), lambda s: (0, tok_blk(s - 1))),
            pl.BlockSpec((BLOCK, d), lambda s: (tok_blk(s - 1), 0)),
            pl.BlockSpec((1, d), lambda s: (0, 0)),
            pl.BlockSpec((BLOCK, BLOCK), lambda s: (0, 0)),
            pl.BlockSpec(memory_space=pl.ANY),
        ],
        out_specs=pl.BlockSpec((BLOCK, d), lambda s: (jnp.clip(s - 1, 0, n_steps - 1), 0)),
        out_shape=jax.ShapeDtypeStruct((n_steps * BLOCK, d), F32),
        scratch_shapes=[
            pltpu.VMEM((2, TOP_K * BLOCK * SUBLANES, LANES), F32),
            pltpu.SemaphoreType.DMA((2,)),
        ],
        compiler_params=pltpu.CompilerParams(
            dimension_semantics=("arbitrary",), vmem_limit_bytes=VMEM_LIMIT),
        name="combine",
    )(dest, gates, h1, final_norm_w[None, :], eye, ypad)


def _layer(h, bsz, nc, mix_norm_w, w_in, conv_w, conv_b, dt_bias, a_log, d_skip, ssd_norm_w,
           w_ssd_out, w_att_out, w_out, ffn_norm_w, w_router, b_router, w_gate_up, b_gate_up,
           w_down, b_down):
    m, d = h.shape
    seq = nc * BLOCK
    n_heads = a_log.shape[0]
    d_inner = n_heads * SSD_HEADDIM
    conv_dim = d_inner + 2 * SSD_GROUPS * SSD_STATE
    att_dim = w_att_out.shape[0]
    n_exp = w_router.shape[1]

    o_xbc = d_inner
    o_dt = o_xbc + conv_dim
    o_q = o_dt + n_heads
    w_main = jnp.concatenate([w_in[:, :o_dt], w_in[:, o_q:]], axis=1).astype(BF16)
    w_dt = jnp.pad(w_in[:, o_dt:o_q], ((0, 0), (0, LANES - n_heads))).astype(BF16)
    q_col = o_dt
    k_col = q_col + att_dim
    v_col = k_col + att_dim
    gs_col = v_col + att_dim
    ga_col = gs_col + d

    proj, dt_raw = _inproj(h, mix_norm_w[None, :], w_main, w_dt)
    yssd = _ssd(proj, dt_raw, conv_w, conv_b, dt_bias, a_log, d_skip, ssd_norm_w, bsz, nc, d_inner)
    att = _attention(proj, bsz, nc, q_col, k_col, v_col, att_dim)
    h1 = _merge(yssd, att, proj, h, w_ssd_out.astype(BF16), w_att_out.astype(BF16),
                w_out.astype(BF16), bsz, seq, gs_col, ga_col)

    ut, idx, gates, rank, cnt = _router(h1, ffn_norm_w[None, :], w_router, b_router)

    counts = cnt[:, 0].astype(jnp.int32)
    padded = (counts + MOE_BLOCK - 1) // MOE_BLOCK * MOE_BLOCK
    padded_ends = jnp.cumsum(padded)
    padded_starts = padded_ends - padded
    n_blocks = (m * TOP_K + MOE_BLOCK - 1) // MOE_BLOCK + n_exp
    dest = rank + jnp.sum(
        jnp.where(idx[None] == jnp.arange(n_exp, dtype=jnp.int32)[:, None, None],
                  padded_starts[:, None, None], 0), axis=0)
    block_first = jnp.arange(n_blocks, dtype=jnp.int32) * MOE_BLOCK
    block_expert = jnp.minimum(
        jnp.sum((padded_ends[None, :] <= block_first[:, None]).astype(jnp.int32), axis=1), n_exp - 1)
    n_used = (padded_ends[-1:] // MOE_BLOCK).astype(jnp.int32)
    zstart = (padded_starts + counts).astype(jnp.int32)

    zpad = (padded - counts).astype(jnp.int32)
    xs = _dispatch(ut, dest, zstart, zpad, n_used, n_blocks * MOE_BLOCK)
    b_gate = b_gate_up[:, None, 0::2]
    b_lin = b_gate_up[:, None, 1::2]
    ypad = _experts(xs, block_expert, n_used, w_gate_up, b_gate, b_lin, w_down, b_down[:, None, :],
                    n_blocks)
    return h1, ypad, dest, gates


def kernel(x, meta_tokens, mix_norm_w, w_in, conv_w, conv_b, dt_bias, a_log, d_skip, ssd_norm_w,
           w_ssd_out, w_att_out, w_out, ffn_norm_w, w_router, b_router, w_gate_up, b_gate_up,
           w_down, b_down, final_norm_w):
    bsz, seq_x, d = x.shape
    depth = mix_norm_w.shape[0]
    assert depth == 1 and seq_x % BLOCK == 0 and (seq_x + N_LEAD) % ROW_TILE == 0
    nc = (seq_x + N_LEAD) // BLOCK
    lead = jnp.concatenate([jnp.zeros((N_PAD, d), x.dtype), meta_tokens.astype(x.dtype)], axis=0)
    h = jnp.concatenate([jnp.broadcast_to(lead[None], (bsz, N_LEAD, d)), x], axis=1)
    h = h.reshape(bsz * nc * BLOCK, d)
    layer = 0
    h1, ypad, dest, gates = _layer(
        h, bsz, nc, mix_norm_w[layer], w_in[layer], conv_w[layer], conv_b[layer], dt_bias[layer],
        a_log[layer], d_skip[layer], ssd_norm_w[layer], w_ssd_out[layer], w_att_out[layer],
        w_out[layer], ffn_norm_w[layer], w_router[layer], b_router[layer], w_gate_up[layer],
        b_gate_up[layer], w_down[layer], b_down[layer])
    out = _combine(ypad, dest, gates, h1, final_norm_w, bsz, nc)
    return out.reshape(bsz, seq_x, d)
```

```python
import functools

import jax
import jax.numpy as jnp
from jax import lax
from jax.experimental import pallas as pl
from jax.experimental.pallas import tpu as pltpu

F32 = jnp.float32
BF16 = jnp.bfloat16
HIGHEST = lax.Precision.HIGHEST

N_META = 16
BLOCK = 128
N_LEAD = BLOCK
N_PAD = N_LEAD - N_META
EPS = 1e-5
SSD_HEADDIM = 64
SSD_GROUPS = 4
SSD_STATE = 128
CONV_K = 4
ATT_HEADDIM = 64
TOP_K = 4
SWIGLU_LIMIT = 7.0
SWIGLU_ALPHA = 1.702
MOE_BLOCK = 256

LANES = 128
SUBLANES = 8
ROW_TILE = 640
INPROJ_ROWS = 1664
EXP_UNDERFLOW = -88.0
VMEM_LIMIT = 56 * 1024 * 1024


def _nt_dot(a, b, precision=None):
    return lax.dot_general(a, b, (((1,), (1,)), ((), ())),
                           preferred_element_type=F32, precision=precision)


def _tn_dot(a, b):
    return lax.dot_general(a, b, (((0,), (0,)), ((), ())), preferred_element_type=F32)


def _silu(x):
    return x * jax.nn.sigmoid(x)


def _inproj_kernel(h_ref, nw_ref, w_ref, wdt_ref, out_ref, dt_ref, u_scr):
    @pl.when(pl.program_id(1) == 0)
    def _():
        x = h_ref[...]
        u = x * lax.rsqrt(jnp.mean(x * x, axis=-1, keepdims=True) + EPS) * nw_ref[...]
        ub = u.astype(BF16)
        u_scr[...] = ub
        dt_ref[...] = jnp.dot(ub, wdt_ref[...], preferred_element_type=F32)

    out_ref[...] = jnp.dot(u_scr[...], w_ref[...], preferred_element_type=F32).astype(out_ref.dtype)


def _largest_row_tile(m, cap):
    return max(t for t in range(BLOCK, cap + 1, BLOCK) if m % t == 0)


def _inproj(h, norm_w, w_main, w_dt, tn=1024):
    m, d = h.shape
    n = w_main.shape[1]
    tm = _largest_row_tile(m, INPROJ_ROWS)
    return pl.pallas_call(
        _inproj_kernel,
        grid=(m // tm, n // tn),
        in_specs=[
            pl.BlockSpec((tm, d), lambda i, j: (i, 0)),
            pl.BlockSpec((1, d), lambda i, j: (0, 0)),
            pl.BlockSpec((d, tn), lambda i, j: (0, j)),
            pl.BlockSpec((d, LANES), lambda i, j: (0, 0)),
        ],
        out_specs=[
            pl.BlockSpec((tm, tn), lambda i, j: (i, j)),
            pl.BlockSpec((tm, LANES), lambda i, j: (i, 0)),
        ],
        out_shape=[jax.ShapeDtypeStruct((m, n), BF16), jax.ShapeDtypeStruct((m, LANES), F32)],
        scratch_shapes=[pltpu.VMEM((tm, d), BF16)],
        compiler_params=pltpu.CompilerParams(
            dimension_semantics=("parallel", "arbitrary"), vmem_limit_bytes=VMEM_LIMIT),
        name="inproj",
    )(h, norm_w, w_main, w_dt)


def _ssd_kernel(z_ref, xs_ref, bc_ref, dt_ref, cwx_ref, cbx_ref, cwb_ref, cbb_ref,
                dtb_ref, alog_ref, dskip_ref, nw_ref, expand_ref, tril_ref,
                out_ref, tailx_scr, tailb_scr, state_scr, *, n_heads):
    c = pl.program_id(1)
    d_inner = xs_ref.shape[1]
    gw = d_inner // SSD_GROUPS
    hpg = n_heads // SSD_GROUPS

    @pl.when(c == 0)
    def _():
        tailx_scr[...] = jnp.zeros(tailx_scr.shape, F32)
        tailb_scr[...] = jnp.zeros(tailb_scr.shape, F32)
        state_scr[...] = jnp.zeros(state_scr.shape, F32)

    row = lax.broadcasted_iota(jnp.int32, (BLOCK, 1), 0)
    vmask = (c * BLOCK + row >= N_PAD).astype(F32)
    first_row = lax.broadcasted_iota(jnp.int32, (SUBLANES, 1), 0) == 0

    def conv(x_ref, tail_scr, w_ref, b_ref):
        x = x_ref[...].astype(F32)
        tail = tail_scr[...]
        tail_scr[...] = x[BLOCK - SUBLANES:, :]
        acc = w_ref[0:1, :] * x
        tacc = w_ref[0:1, :] * tail
        for i in range(1, CONV_K):
            rolled = pltpu.roll(acc, 1, axis=0)
            trolled = pltpu.roll(tacc, 1, axis=0)
            shifted = jnp.concatenate(
                [jnp.where(first_row, trolled, rolled[:SUBLANES]), rolled[SUBLANES:]], axis=0)
            acc = w_ref[i:i + 1, :] * x + shifted
            tacc = w_ref[i:i + 1, :] * tail + trolled
        return _silu(acc + b_ref[...]) * vmask

    xs = conv(xs_ref, tailx_scr, cwx_ref, cbx_ref)
    bc = conv(bc_ref, tailb_scr, cwb_ref, cbb_ref)
    gn = SSD_GROUPS * SSD_STATE
    b_all = bc[:, :gn].astype(BF16)
    c_all = bc[:, gn:].astype(BF16)

    dt = jax.nn.softplus(dt_ref[:, :n_heads] + dtb_ref[...]) * vmask
    a = -jnp.exp(alog_ref[...]) * dt
    a_cs = jnp.dot(tril_ref[...], a, preferred_element_type=F32, precision=HIGHEST)
    a_cs_t = a_cs.T
    a_last = a_cs[BLOCK - 1:BLOCK, :]
    decay_states = jnp.exp(a_last - a_cs)
    decay_out = jnp.exp(a_cs)
    chunk_decay = jnp.broadcast_to(jnp.exp(a_last), (SUBLANES, n_heads))
    stacked = jnp.concatenate([dt, decay_states, decay_out, chunk_decay], axis=0)
    st_hi = stacked.astype(BF16)
    st_lo = (stacked - st_hi.astype(F32)).astype(BF16)
    expanded = (jnp.dot(st_hi, expand_ref[...], preferred_element_type=F32)
                + jnp.dot(st_lo, expand_ref[...], preferred_element_type=F32))
    dt_x = expanded[0:BLOCK]
    ds_x = expanded[BLOCK:2 * BLOCK]
    do_x = expanded[2 * BLOCK:3 * BLOCK]
    cd_x = expanded[3 * BLOCK:3 * BLOCK + 1]

    x_dt = xs * dt_x
    x_dt_b = x_dt.astype(BF16)
    x_ds_b = (x_dt * ds_x).astype(BF16)

    li = lax.broadcasted_iota(jnp.int32, (BLOCK, BLOCK), 0)
    si = lax.broadcasted_iota(jnp.int32, (BLOCK, BLOCK), 1)
    causal = si <= li
    first_half = lax.broadcasted_iota(jnp.int32, (BLOCK, 2 * SSD_HEADDIM), 1) < SSD_HEADDIM

    y_groups = []
    for g in range(SSD_GROUPS):
        bg = b_all[:, g * SSD_STATE:(g + 1) * SSD_STATE]
        cg = c_all[:, g * SSD_STATE:(g + 1) * SSD_STATE]
        cols = slice(g * gw, (g + 1) * gw)
        cb = _nt_dot(cg, bg)
        prev = state_scr[g]
        y_off = jnp.dot(cg, prev.astype(BF16), preferred_element_type=F32) * do_x[:, cols]
        new_state = _tn_dot(bg, x_ds_b[:, cols])
        state_scr[g] = cd_x[:, cols] * prev + new_state
        pairs = []
        for jp in range(hpg // 2):
            h0 = g * hpg + 2 * jp
            pc = slice(g * gw + jp * 2 * SSD_HEADDIM, g * gw + (jp + 1) * 2 * SSD_HEADDIM)
            xp = x_dt_b[:, pc]
            ys = []
            for h in (h0, h0 + 1):
                seg = a_cs[:, h:h + 1] - a_cs_t[h:h + 1, :]
                m = (cb * jnp.exp(jnp.where(causal, seg, -jnp.inf))).astype(BF16)
                ys.append(jnp.dot(m, xp, preferred_element_type=F32))
            pairs.append(jnp.where(first_half, ys[0], ys[1]))
        y_groups.append(jnp.concatenate(pairs, axis=1) + y_off)
    y = jnp.concatenate(y_groups, axis=1) + dskip_ref[...] * xs

    gated = y * _silu(z_ref[...].astype(F32))
    outs = []
    for g in range(SSD_GROUPS):
        gg = gated[:, g * gw:(g + 1) * gw]
        outs.append(gg * lax.rsqrt(jnp.mean(gg * gg, axis=-1, keepdims=True) + EPS))
    out_ref[...] = (jnp.concatenate(outs, axis=1) * nw_ref[...]).astype(out_ref.dtype)


def _ssd(proj, dt_raw, conv_w, conv_b, dt_bias, a_log, d_skip, norm_w, bsz, nc, d_inner):
    m = proj.shape[0]
    n_heads = a_log.shape[0]
    bc_w = 2 * SSD_GROUPS * SSD_STATE
    gw = d_inner // SSD_GROUPS
    cwx, cwb = conv_w[:, :d_inner], conv_w[:, d_inner:]
    cbx, cbb = conv_b[None, :d_inner], conv_b[None, d_inner:]
    expand = jnp.repeat(jnp.eye(n_heads, dtype=BF16), SSD_HEADDIM, axis=1)
    tril = jnp.tril(jnp.ones((BLOCK, BLOCK), F32))
    dskip_x = jnp.repeat(d_skip.astype(F32), SSD_HEADDIM)[None, :]
    full = lambda shape: pl.BlockSpec(shape, lambda b, c: (0,) * len(shape))
    row_blk = lambda b, c: b * nc + c
    return pl.pallas_call(
        functools.partial(_ssd_kernel, n_heads=n_heads),
        grid=(bsz, nc),
        in_specs=[
            pl.BlockSpec((BLOCK, d_inner), lambda b, c: (row_blk(b, c), 0)),
            pl.BlockSpec((BLOCK, d_inner), lambda b, c: (row_blk(b, c), 1)),
            pl.BlockSpec((BLOCK, bc_w), lambda b, c: (row_blk(b, c), 2 * d_inner // bc_w)),
            pl.BlockSpec((BLOCK, LANES), lambda b, c: (row_blk(b, c), 0)),
            full((CONV_K, d_inner)), full((1, d_inner)), full((CONV_K, bc_w)), full((1, bc_w)),
            full((1, n_heads)), full((1, n_heads)), full((1, d_inner)), full((1, d_inner)),
            full((n_heads, d_inner)), full((BLOCK, BLOCK)),
        ],
        out_specs=pl.BlockSpec((BLOCK, d_inner), lambda b, c: (row_blk(b, c), 0)),
        out_shape=jax.ShapeDtypeStruct((m, d_inner), BF16),
        scratch_shapes=[
            pltpu.VMEM((SUBLANES, d_inner), F32),
            pltpu.VMEM((SUBLANES, bc_w), F32),
            pltpu.VMEM((SSD_GROUPS, SSD_STATE, gw), F32),
        ],
        compiler_params=pltpu.CompilerParams(
            dimension_semantics=("parallel", "arbitrary"), vmem_limit_bytes=VMEM_LIMIT),
        name="ssd",
    )(proj, proj, proj, dt_raw, cwx, cbx, cwb, cbb, dt_bias[None, :].astype(F32),
      a_log[None, :].astype(F32), dskip_x, norm_w[None, :].astype(F32), expand, tril)


def _attn_kernel(q_ref, k_ref, v_ref, upper_ref, out_ref, *, n_pairs):
    i = pl.program_id(2)
    pw = 2 * ATT_HEADDIM
    lane = lax.broadcasted_iota(jnp.int32, (BLOCK, pw), 1)
    first = lane < ATT_HEADDIM
    row2 = lax.broadcasted_iota(jnp.int32, (2 * BLOCK, BLOCK), 0)
    qpos = i * BLOCK + jnp.where(row2 < BLOCK, row2, row2 - BLOCK)
    kin = lax.broadcasted_iota(jnp.int32, (2 * BLOCK, BLOCK), 1)
    upper = upper_ref[...]
    scale = ATT_HEADDIM ** -0.5
    q2 = []
    for p in range(n_pairs):
        q = q_ref[:, p * pw:(p + 1) * pw] * jnp.asarray(scale, BF16)
        zero = jnp.zeros_like(q)
        q2.append(jnp.concatenate([jnp.where(first, q, zero), jnp.where(first, zero, q)], axis=0))

    def process(kbs, runs, accs):
        starts = [pl.multiple_of(jnp.maximum(kb, 0) * BLOCK, BLOCK) for kb in kbs]
        alloweds = []
        for kb in kbs:
            kpos = kb * BLOCK + kin
            alloweds.append(jnp.logical_and(kpos < qpos, kpos >= N_PAD))
        zs = [[_nt_dot(q2[p], k_ref[pl.ds(start, BLOCK), p * pw:(p + 1) * pw])
               for p in range(n_pairs)] for start in starts]
        log_betas, laters, sums = [], [], []
        for j in range(len(kbs)):
            log_betas.append([])
            laters.append([])
            sums.append([])
            for p in range(n_pairs):
                z = zs[j][p]
                l1p = jnp.log(1.0 + jnp.exp(-jnp.abs(z)))
                log_beta = jnp.minimum(z, 0.0) - l1p
                log_stay = jnp.where(alloweds[j], log_beta - z, 0.0)
                hi = log_stay.astype(BF16)
                lo = (log_stay - hi.astype(F32)).astype(BF16)
                log_betas[j].append(log_beta)
                laters[j].append(jnp.dot(hi, upper, preferred_element_type=F32)
                                 + jnp.dot(lo, upper, preferred_element_type=F32))
                sums[j].append(jnp.sum(log_stay, axis=1, keepdims=True))
        runs, accs = list(runs), list(accs)
        for j, start in enumerate(starts):
            for p in range(n_pairs):
                vblk = v_ref[pl.ds(start, BLOCK), p * pw:(p + 1) * pw]
                w = jnp.where(alloweds[j], jnp.exp(log_betas[j][p] + laters[j][p] + runs[p]),
                              0.0).astype(BF16)
                vzero = jnp.zeros_like(vblk)
                accs[p] = (accs[p]
                           + jnp.dot(w[:BLOCK], jnp.where(first, vblk, vzero),
                                     preferred_element_type=F32)
                           + jnp.dot(w[BLOCK:], jnp.where(first, vzero, vblk),
                                     preferred_element_type=F32))
                runs[p] = runs[p] + sums[j][p]
        top = jnp.max(runs[0])
        for r in runs[1:]:
            top = jnp.maximum(top, jnp.max(r))
        return top, runs, accs

    def cond(carry):
        kb, top = carry[0], carry[1]
        return jnp.logical_and(kb >= 0, top > EXP_UNDERFLOW)

    def body(carry):
        kb = carry[0]
        top, runs, accs = process([kb], carry[2:2 + n_pairs], carry[2 + n_pairs:])
        return (kb - 1, top, *runs, *accs)

    zero_runs = [jnp.zeros((2 * BLOCK, 1), F32) for _ in range(n_pairs)]
    zero_accs = [jnp.zeros((BLOCK, pw), F32) for _ in range(n_pairs)]
    top, runs, accs = process([i, i - 1], zero_runs, zero_accs)
    final = lax.while_loop(cond, body, (i - 2, top, *runs, *accs))
    for p in range(n_pairs):
        out_ref[:, p * pw:(p + 1) * pw] = final[2 + n_pairs + p].astype(out_ref.dtype)


def _attention(proj, bsz, nc, q_col, k_col, v_col, att_dim, pairs_per_step=4):
    m = proj.shape[0]
    seq = nc * BLOCK
    sw = pairs_per_step * 2 * ATT_HEADDIM
    upper = jnp.triu(jnp.ones((BLOCK, BLOCK), F32), 1).T.astype(BF16)
    return pl.pallas_call(
        functools.partial(_attn_kernel, n_pairs=pairs_per_step),
        grid=(bsz, att_dim // sw, nc),
        in_specs=[
            pl.BlockSpec((BLOCK, sw), lambda b, p, i: (b * nc + i, q_col // sw + p)),
            pl.BlockSpec((seq, sw), lambda b, p, i: (b, k_col // sw + p)),
            pl.BlockSpec((seq, sw), lambda b, p, i: (b, v_col // sw + p)),
            pl.BlockSpec((BLOCK, BLOCK), lambda b, p, i: (0, 0)),
        ],
        out_specs=pl.BlockSpec((BLOCK, sw), lambda b, p, i: (b * nc + i, p)),
        out_shape=jax.ShapeDtypeStruct((m, att_dim), BF16),
        compiler_params=pltpu.CompilerParams(
            dimension_semantics=("parallel", "parallel", "arbitrary"),
            vmem_limit_bytes=VMEM_LIMIT),
        name="attention",
    )(proj, proj, proj, upper)


def _merge_kernel(yssd_ref, att_ref, gs_ref, ga_ref, h_ref, wso_ref, wao_ref, wo_ref, out_ref):
    t = pl.program_id(1)
    y_ssd = jnp.dot(yssd_ref[...], wso_ref[...], preferred_element_type=F32)
    y_att = jnp.dot(att_ref[...], wao_ref[...], preferred_element_type=F32)
    merged = (jax.nn.sigmoid(gs_ref[...].astype(F32)) * y_ssd
              + jax.nn.sigmoid(ga_ref[...].astype(F32)) * y_att)
    mixed = jnp.dot(merged.astype(BF16), wo_ref[...], preferred_element_type=F32)
    row = lax.broadcasted_iota(jnp.int32, (ROW_TILE, 1), 0)
    valid = (t * ROW_TILE + row >= N_PAD).astype(F32)
    out_ref[...] = h_ref[...] + mixed * valid


def _merge(yssd, att, proj, h, w_ssd_out, w_att_out, w_out, bsz, seq, gs_col, ga_col):
    m, d = h.shape
    d_inner = yssd.shape[1]
    att_dim = att.shape[1]
    per = seq // ROW_TILE
    rb = lambda b, t: b * per + t
    full = lambda shape: pl.BlockSpec(shape, lambda b, t: (0,) * len(shape))
    return pl.pallas_call(
        _merge_kernel,
        grid=(bsz, per),
        in_specs=[
            pl.BlockSpec((ROW_TILE, d_inner), lambda b, t: (rb(b, t), 0)),
            pl.BlockSpec((ROW_TILE, att_dim), lambda b, t: (rb(b, t), 0)),
            pl.BlockSpec((ROW_TILE, d), lambda b, t: (rb(b, t), gs_col // d)),
            pl.BlockSpec((ROW_TILE, d), lambda b, t: (rb(b, t), ga_col // d)),
            pl.BlockSpec((ROW_TILE, d), lambda b, t: (rb(b, t), 0)),
            full((d_inner, d)), full((att_dim, d)), full((d, d)),
        ],
        out_specs=pl.BlockSpec((ROW_TILE, d), lambda b, t: (rb(b, t), 0)),
        out_shape=jax.ShapeDtypeStruct((m, d), F32),
        compiler_params=pltpu.CompilerParams(
            dimension_semantics=("parallel", "parallel"), vmem_limit_bytes=VMEM_LIMIT),
        name="merge",
    )(yssd, att, proj, proj, h, w_ssd_out, w_att_out, w_out)


def _router_kernel(h_ref, nw_ref, wr_ref, br_ref, before_ref,
                   ut_ref, idx_ref, gate_ref, rank_ref, cnt_ref, base_scr):
    step = pl.program_id(0)
    n_exp = wr_ref.shape[0]
    tm = h_ref.shape[0]

    @pl.when(step == 0)
    def _():
        base_scr[...] = jnp.zeros(base_scr.shape, F32)

    x = h_ref[...]
    u = x * lax.rsqrt(jnp.mean(x * x, axis=-1, keepdims=True) + EPS) * nw_ref[...]
    for c in range(SUBLANES):
        ut_ref[pl.ds(c, tm, stride=SUBLANES), :] = u[:, c * LANES:(c + 1) * LANES]

    logits = _nt_dot(wr_ref[...], u, precision=HIGHEST) + br_ref[...]
    eidx = lax.broadcasted_iota(jnp.int32, (n_exp, tm), 0)
    work = logits
    tops, idxs, onehots = [], [], []
    for _ in range(TOP_K):
        top = jnp.max(work, axis=0, keepdims=True)
        idx = jnp.min(jnp.where(work == top, eidx, n_exp), axis=0, keepdims=True)
        hot = eidx == idx
        work = jnp.where(hot, -jnp.inf, work)
        tops.append(top)
        idxs.append(idx)
        onehots.append(hot.astype(F32))
    exps = [jnp.exp(t - tops[0]) for t in tops]
    denom = exps[0] + exps[1] + exps[2] + exps[3]
    cnt = onehots[0] + onehots[1] + onehots[2] + onehots[3]
    before = jnp.dot(cnt.astype(BF16), before_ref[...], preferred_element_type=F32) + base_scr[...]
    ranks = [jnp.sum(hot * before, axis=0, keepdims=True) for hot in onehots]
    pad_rows = SUBLANES - TOP_K
    idx_ref[...] = jnp.concatenate(idxs + [jnp.zeros((pad_rows, tm), jnp.int32)], axis=0)
    gate_ref[...] = jnp.concatenate([e / denom for e in exps] + [jnp.zeros((pad_rows, tm), F32)], axis=0)
    rank_ref[...] = jnp.concatenate(
        [r.astype(jnp.int32) for r in ranks] + [jnp.zeros((pad_rows, tm), jnp.int32)], axis=0)
    base_scr[...] = base_scr[...] + jnp.sum(cnt, axis=1, keepdims=True)
    cnt_ref[...] = jnp.broadcast_to(base_scr[...], cnt_ref.shape)


def _router(h1, norm_w, w_router, b_router):
    m, d = h1.shape
    n_exp = w_router.shape[1]
    before = jnp.triu(jnp.ones((ROW_TILE, ROW_TILE), F32), 1).astype(BF16)
    full = lambda shape: pl.BlockSpec(shape, lambda i: (0,) * len(shape))
    return pl.pallas_call(
        _router_kernel,
        grid=(m // ROW_TILE,),
        in_specs=[
            pl.BlockSpec((ROW_TILE, d), lambda i: (i, 0)),
            full((1, d)), full((n_exp, d)), full((n_exp, 1)), full((ROW_TILE, ROW_TILE)),
        ],
        out_specs=[
            pl.BlockSpec((ROW_TILE * SUBLANES, LANES), lambda i: (i, 0)),
            pl.BlockSpec((SUBLANES, ROW_TILE), lambda i: (0, i)),
            pl.BlockSpec((SUBLANES, ROW_TILE), lambda i: (0, i)),
            pl.BlockSpec((SUBLANES, ROW_TILE), lambda i: (0, i)),
            full((n_exp, LANES)),
        ],
        out_shape=[
            jax.ShapeDtypeStruct((m * SUBLANES, LANES), F32),
            jax.ShapeDtypeStruct((SUBLANES, m), jnp.int32),
            jax.ShapeDtypeStruct((SUBLANES, m), F32),
            jax.ShapeDtypeStruct((SUBLANES, m), jnp.int32),
            jax.ShapeDtypeStruct((n_exp, LANES), F32),
        ],
        scratch_shapes=[pltpu.VMEM((n_exp, 1), F32)],
        compiler_params=pltpu.CompilerParams(
            dimension_semantics=("arbitrary",), vmem_limit_bytes=VMEM_LIMIT),
        name="router",
    )(h1, norm_w, w_router.T, b_router[:, None], before)


def _dispatch_kernel(zstart_ref, zpad_ref, nused_ref, dest_ref, ut_ref, xs_ref, zero_scr, sem, zsem,
                     *, n_exp):
    step = pl.program_id(0)
    tb = dest_ref.shape[1]
    slot_rows = MOE_BLOCK * SUBLANES
    n_blocks = xs_ref.shape[0] // slot_rows

    @pl.when(step == 0)
    def _():
        zero_scr[...] = jnp.zeros(zero_scr.shape, F32)

        def zero_rows(row_start, n_rows):
            return pltpu.make_async_copy(zero_scr.at[pl.ds(0, n_rows)],
                                         xs_ref.at[pl.ds(row_start, n_rows)], zsem)

        def pad_fill(wait):
            def per_expert(e, carry):
                n_pad = zpad_ref[e]
                pos = zstart_ref[e]
                size = MOE_BLOCK // 2
                while size >= 1:
                    take = n_pad & size

                    @pl.when(take != 0)
                    def _(pos=pos, size=size):
                        copy = zero_rows(pl.multiple_of(pos * SUBLANES, SUBLANES), size * SUBLANES)
                        copy.wait() if wait else copy.start()

                    pos = pos + take
                    size //= 2
                return carry

            lax.fori_loop(0, n_exp, per_expert, 0)

        def tail_fill(wait):
            def per_block(b, carry):
                copy = zero_rows(pl.multiple_of(b * slot_rows, slot_rows), slot_rows)
                copy.wait() if wait else copy.start()
                return carry

            lax.fori_loop(nused_ref[0], n_blocks, per_block, 0)

        pad_fill(False)
        tail_fill(False)
        pad_fill(True)
        tail_fill(True)

    def issue(t, carry):
        src = pl.multiple_of(t * SUBLANES, SUBLANES)
        for k in range(TOP_K):
            dst = pl.multiple_of(dest_ref[k, t] * SUBLANES, SUBLANES)
            pltpu.make_async_copy(ut_ref.at[pl.ds(src, SUBLANES)],
                                  xs_ref.at[pl.ds(dst, SUBLANES)], sem).start(priority=k % 2)
        return carry

    lax.fori_loop(0, tb, issue, 0, unroll=4)
    for _ in range(TOP_K):
        pltpu.make_async_copy(ut_ref, xs_ref.at[pl.ds(0, tb * SUBLANES)], sem).wait()


def _dispatch(ut, dest, zstart, zpad, n_used, n_slots):
    m = dest.shape[1]
    n_exp = zstart.shape[0]
    return pl.pallas_call(
        functools.partial(_dispatch_kernel, n_exp=n_exp),
        grid_spec=pltpu.PrefetchScalarGridSpec(
            num_scalar_prefetch=3,
            grid=(m // ROW_TILE,),
            in_specs=[
                pl.BlockSpec((SUBLANES, ROW_TILE), lambda i, *_: (0, i), memory_space=pltpu.SMEM),
                pl.BlockSpec((ROW_TILE * SUBLANES, LANES), lambda i, *_: (i, 0)),
            ],
            out_specs=pl.BlockSpec(memory_space=pl.ANY),
            scratch_shapes=[
                pltpu.VMEM((MOE_BLOCK * SUBLANES, LANES), F32),
                pltpu.SemaphoreType.DMA(()),
                pltpu.SemaphoreType.DMA(()),
            ],
        ),
        out_shape=jax.ShapeDtypeStruct((n_slots * SUBLANES, LANES), F32),
        compiler_params=pltpu.CompilerParams(
            dimension_semantics=("arbitrary",), vmem_limit_bytes=VMEM_LIMIT),
        name="dispatch",
    )(zstart, zpad, n_used, dest, ut)


def _expert_kernel(be_ref, nused_ref, xs_ref, wgu_ref, bg_ref, bl_ref, wd_ref, bd_ref, perm_ref,
                   y_ref, wg_scr, wl_scr, wd_scr):
    b = pl.program_id(0)
    pair = 2 * LANES
    changed = jnp.logical_or(b == 0, be_ref[b] != be_ref[jnp.maximum(b - 1, 0)])

    @pl.when(jnp.logical_and(changed, b < nused_ref[0]))
    def _():
        for cb in range(wgu_ref.shape[2] // pair):
            wb = wgu_ref[0, :, cb * pair:(cb + 1) * pair].astype(BF16)
            sep = jnp.dot(wb, perm_ref[...], preferred_element_type=F32).astype(BF16)
            wg_scr[:, cb * LANES:(cb + 1) * LANES] = sep[:, :LANES]
            wl_scr[:, cb * LANES:(cb + 1) * LANES] = sep[:, LANES:]
        wd_scr[...] = wd_ref[0].astype(BF16)

    @pl.when(b < nused_ref[0])
    def _():
        x = jnp.concatenate(
            [xs_ref[pl.ds(c, MOE_BLOCK, stride=SUBLANES), :] for c in range(SUBLANES)],
            axis=1).astype(BF16)
        gate = jnp.dot(x, wg_scr[...], preferred_element_type=F32) + bg_ref[0]
        lin = jnp.dot(x, wl_scr[...], preferred_element_type=F32) + bl_ref[0]
        gate = jnp.minimum(gate, SWIGLU_LIMIT)
        lin = jnp.clip(lin, -SWIGLU_LIMIT, SWIGLU_LIMIT)
        act = gate * jax.nn.sigmoid(SWIGLU_ALPHA * gate) * (lin + 1.0)
        y = jnp.dot(act.astype(BF16), wd_scr[...], preferred_element_type=F32) + bd_ref[0]
        for c in range(SUBLANES):
            y_ref[pl.ds(c, MOE_BLOCK, stride=SUBLANES), :] = y[:, c * LANES:(c + 1) * LANES]

    @pl.when(b >= nused_ref[0])
    def _():
        y_ref[...] = jnp.zeros(y_ref.shape, F32)


def _experts(xs, block_expert, n_used, w_gate_up, b_gate, b_lin, w_down, b_down, n_blocks):
    n_exp, d, d_ff2 = w_gate_up.shape
    d_ff = d_ff2 // 2
    rows = MOE_BLOCK * SUBLANES
    blk = lambda b, be, nu: (jnp.minimum(b, nu[0] - 1), 0)
    wsel = lambda b, be, nu: (be[b], 0, 0)
    r = jnp.arange(2 * LANES)[:, None]
    c = jnp.arange(2 * LANES)[None, :]
    perm = (r == jnp.where(c < LANES, 2 * c, 2 * (c - LANES) + 1)).astype(BF16)
    return pl.pallas_call(
        _expert_kernel,
        grid_spec=pltpu.PrefetchScalarGridSpec(
            num_scalar_prefetch=2,
            grid=(n_blocks,),
            in_specs=[
                pl.BlockSpec((rows, LANES), blk),
                pl.BlockSpec((1, d, d_ff2), wsel),
                pl.BlockSpec((1, 1, d_ff), wsel), pl.BlockSpec((1, 1, d_ff), wsel),
                pl.BlockSpec((1, d_ff, d), wsel), pl.BlockSpec((1, 1, d), wsel),
                pl.BlockSpec((2 * LANES, 2 * LANES), lambda b, be, nu: (0, 0)),
            ],
            out_specs=pl.BlockSpec((rows, LANES), lambda b, be, nu: (b, 0)),
            scratch_shapes=[
                pltpu.VMEM((d, d_ff), BF16), pltpu.VMEM((d, d_ff), BF16), pltpu.VMEM((d_ff, d), BF16),
            ],
        ),
        out_shape=jax.ShapeDtypeStruct((n_blocks * rows, LANES), F32),
        compiler_params=pltpu.CompilerParams(
            dimension_semantics=("arbitrary",), vmem_limit_bytes=VMEM_LIMIT),
        name="experts",
    )(block_expert, n_used, xs, w_gate_up, b_gate, b_lin, w_down, b_down, perm)


def _combine_kernel(dest_ref, gate_ref, h_ref, nw_ref, eye_ref, ypad_ref, out_ref, buf, sem, *, n_steps):
    s = pl.program_id(0)
    rows = BLOCK * SUBLANES

    @pl.when(s < n_steps)
    def _():
        slot = s % 2

        def issue(t, carry):
            for k in range(TOP_K):
                src = pl.multiple_of(dest_ref[k, t] * SUBLANES, SUBLANES)
                dst = pl.multiple_of(k * rows + t * SUBLANES, SUBLANES)
                pltpu.make_async_copy(ypad_ref.at[pl.ds(src, SUBLANES)],
                                      buf.at[slot, pl.ds(dst, SUBLANES)],
                                      sem.at[slot]).start(priority=k % 2)
            return carry

        lax.fori_loop(0, BLOCK, issue, 0, unroll=4)

    @pl.when(s > 0)
    def _():
        slot = (s - 1) % 2
        pltpu.make_async_copy(ypad_ref.at[pl.ds(0, TOP_K * rows)], buf.at[slot], sem.at[slot]).wait()
        gates_t = _nt_dot(eye_ref[...], gate_ref[...], precision=HIGHEST)
        acc = h_ref[...]
        for k in range(TOP_K):
            yk = jnp.concatenate(
                [buf[slot, pl.ds(k * rows + c, BLOCK, stride=SUBLANES), :] for c in range(SUBLANES)],
                axis=1)
            acc = acc + gates_t[:, k:k + 1] * yk
        out = acc * lax.rsqrt(jnp.mean(acc * acc, axis=-1, keepdims=True) + EPS) * nw_ref[...]
        out_ref[...] = out


def _combine(ypad, dest, gates, h1, final_norm_w, bsz, nc):
    m, d = h1.shape
    per = nc - 1
    n_steps = bsz * per
    eye = jnp.eye(BLOCK, dtype=F32)

    def tok_blk(s):
        s = jnp.clip(s, 0, n_steps - 1)
        return (s // per) * nc + 1 + s % per

    return pl.pallas_call(
        functools.partial(_combine_kernel, n_steps=n_steps),
        grid=(n_steps + 1,),
        in_specs=[
            pl.BlockSpec((SUBLANES, BLOCK), lambda s: (0, tok_blk(s)), memory_space=pltpu.SMEM),
            pl.BlockSpec((SUBLANES, BLOCK), lambda s: (0, tok_blk(s - 1))),
            pl.BlockSpec((BLOCK, d), lambda s: (tok_blk(s - 1), 0)),
            pl.BlockSpec((1, d), lambda s: (0, 0)),
            pl.BlockSpec((BLOCK, BLOCK), lambda s: (0, 0)),
            pl.BlockSpec(memory_space=pl.ANY),
        ],
        out_specs=pl.BlockSpec((BLOCK, d), lambda s: (jnp.clip(s - 1, 0, n_steps - 1), 0)),
        out_shape=jax.ShapeDtypeStruct((n_steps * BLOCK, d), F32),
        scratch_shapes=[
            pltpu.VMEM((2, TOP_K * BLOCK * SUBLANES, LANES), F32),
            pltpu.SemaphoreType.DMA((2,)),
        ],
        compiler_params=pltpu.CompilerParams(
            dimension_semantics=("arbitrary",), vmem_limit_bytes=VMEM_LIMIT),
        name="combine",
    )(dest, gates, h1, final_norm_w[None, :], eye, ypad)


def _layer(h, bsz, nc, mix_norm_w, w_in, conv_w, conv_b, dt_bias, a_log, d_skip, ssd_norm_w,
           w_ssd_out, w_att_out, w_out, ffn_norm_w, w_router, b_router, w_gate_up, b_gate_up,
           w_down, b_down):
    m, d = h.shape
    seq = nc * BLOCK
    n_heads = a_log.shape[0]
    d_inner = n_heads * SSD_HEADDIM
    conv_dim = d_inner + 2 * SSD_GROUPS * SSD_STATE
    att_dim = w_att_out.shape[0]
    n_exp = w_router.shape[1]

    o_xbc = d_inner
    o_dt = o_xbc + conv_dim
    o_q = o_dt + n_heads
    w_main = jnp.concatenate([w_in[:, :o_dt], w_in[:, o_q:]], axis=1).astype(BF16)
    w_dt = jnp.pad(w_in[:, o_dt:o_q], ((0, 0), (0, LANES - n_heads))).astype(BF16)
    q_col = o_dt
    k_col = q_col + att_dim
    v_col = k_col + att_dim
    gs_col = v_col + att_dim
    ga_col = gs_col + d

    proj, dt_raw = _inproj(h, mix_norm_w[None, :], w_main, w_dt)
    yssd = _ssd(proj, dt_raw, conv_w, conv_b, dt_bias, a_log, d_skip, ssd_norm_w, bsz, nc, d_inner)
    att = _attention(proj, bsz, nc, q_col, k_col, v_col, att_dim)
    h1 = _merge(yssd, att, proj, h, w_ssd_out.astype(BF16), w_att_out.astype(BF16),
                w_out.astype(BF16), bsz, seq, gs_col, ga_col)

    ut, idx, gates, rank, cnt = _router(h1, ffn_norm_w[None, :], w_router, b_router)

    counts = cnt[:, 0].astype(jnp.int32)
    padded = (counts + MOE_BLOCK - 1) // MOE_BLOCK * MOE_BLOCK
    padded_ends = jnp.cumsum(padded)
    padded_starts = padded_ends - padded
    n_blocks = (m * TOP_K + MOE_BLOCK - 1) // MOE_BLOCK + n_exp
    dest = rank + jnp.sum(
        jnp.where(idx[None] == jnp.arange(n_exp, dtype=jnp.int32)[:, None, None],
                  padded_starts[:, None, None], 0), axis=0)
    block_first = jnp.arange(n_blocks, dtype=jnp.int32) * MOE_BLOCK
    block_expert = jnp.minimum(
        jnp.sum((padded_ends[None, :] <= block_first[:, None]).astype(jnp.int32), axis=1), n_exp - 1)
    n_used = (padded_ends[-1:] // MOE_BLOCK).astype(jnp.int32)
    zstart = (padded_starts + counts).astype(jnp.int32)

    zpad = (padded - counts).astype(jnp.int32)
    xs = _dispatch(ut, dest, zstart, zpad, n_used, n_blocks * MOE_BLOCK)
    b_gate = b_gate_up[:, None, 0::2]
    b_lin = b_gate_up[:, None, 1::2]
    ypad = _experts(xs, block_expert, n_used, w_gate_up, b_gate, b_lin, w_down, b_down[:, None, :],
                    n_blocks)
    return h1, ypad, dest, gates


def kernel(x, meta_tokens, mix_norm_w, w_in, conv_w, conv_b, dt_bias, a_log, d_skip, ssd_norm_w,
           w_ssd_out, w_att_out, w_out, ffn_norm_w, w_router, b_router, w_gate_up, b_gate_up,
           w_down, b_down, final_norm_w):
    bsz, seq_x, d = x.shape
    depth = mix_norm_w.shape[0]
    assert depth == 1 and seq_x % BLOCK == 0 and (seq_x + N_LEAD) % ROW_TILE == 0
    nc = (seq_x + N_LEAD) // BLOCK
    lead = jnp.concatenate([jnp.zeros((N_PAD, d), x.dtype), meta_tokens.astype(x.dtype)], axis=0)
    h = jnp.concatenate([jnp.broadcast_to(lead[None], (bsz, N_LEAD, d)), x], axis=1)
    h = h.reshape(bsz * nc * BLOCK, d)
    layer = 0
    h1, ypad, dest, gates = _layer(
        h, bsz, nc, mix_norm_w[layer], w_in[layer], conv_w[layer], conv_b[layer], dt_bias[layer],
        a_log[layer], d_skip[layer], ssd_norm_w[layer], w_ssd_out[layer], w_att_out[layer],
        w_out[layer], ffn_norm_w[layer], w_router[layer], b_router[layer], w_gate_up[layer],
        b_gate_up[layer], w_down[layer], b_down[layer])
    out = _combine(ypad, dest, gates, h1, final_norm_w, bsz, nc)
    return out.reshape(bsz, seq_x, d)
```

```python
import functools

import jax
import jax.numpy as jnp
from jax import lax
from jax.experimental import pallas as pl
from jax.experimental.pallas import tpu as pltpu

F32 = jnp.float32
BF16 = jnp.bfloat16
HIGHEST = lax.Precision.HIGHEST

N_META = 16
BLOCK = 128
N_LEAD = BLOCK
N_PAD = N_LEAD - N_META
EPS = 1e-5
SSD_HEADDIM = 64
SSD_GROUPS = 4
SSD_STATE = 128
CONV_K = 4
ATT_HEADDIM = 64
TOP_K = 4
SWIGLU_LIMIT = 7.0
SWIGLU_ALPHA = 1.702
MOE_BLOCK = 512

LANES = 128
SUBLANES = 8
ROW_TILE = 640
INPROJ_ROWS = 1664
COMBINE_LAG = 2
COMBINE_BUFFERS = COMBINE_LAG + 1
EXP_UNDERFLOW = -88.0
MASKED_SCORE = -1e30
VMEM_LIMIT = 56 * 1024 * 1024


def _nt_dot(a, b, precision=None):
    return lax.dot_general(a, b, (((1,), (1,)), ((), ())),
                           preferred_element_type=F32, precision=precision)


def _tn_dot(a, b):
    return lax.dot_general(a, b, (((0,), (0,)), ((), ())), preferred_element_type=F32)


def _silu(x):
    half = 0.5 * x
    return half + half * jnp.tanh(half)


def _inproj_kernel(h_ref, nw_ref, w_ref, wdt_ref, out_ref, dt_ref, u_scr):
    @pl.when(pl.program_id(1) == 0)
    def _():
        x = h_ref[...]
        u = x * lax.rsqrt(jnp.mean(x * x, axis=-1, keepdims=True) + EPS) * nw_ref[...]
        ub = u.astype(BF16)
        u_scr[...] = ub
        dt_ref[...] = jnp.dot(ub, wdt_ref[...], preferred_element_type=F32)

    out_ref[...] = jnp.dot(u_scr[...], w_ref[...], preferred_element_type=F32).astype(out_ref.dtype)


def _largest_row_tile(m, cap):
    return max(t for t in range(BLOCK, cap + 1, BLOCK) if m % t == 0)


def _inproj(h, norm_w, w_main, w_dt, tn=1024):
    m, d = h.shape
    n = w_main.shape[1]
    tm = _largest_row_tile(m, INPROJ_ROWS)
    return pl.pallas_call(
        _inproj_kernel,
        grid=(m // tm, n // tn),
        in_specs=[
            pl.BlockSpec((tm, d), lambda i, j: (i, 0)),
            pl.BlockSpec((1, d), lambda i, j: (0, 0)),
            pl.BlockSpec((d, tn), lambda i, j: (0, j)),
            pl.BlockSpec((d, LANES), lambda i, j: (0, 0)),
        ],
        out_specs=[
            pl.BlockSpec((tm, tn), lambda i, j: (i, j)),
            pl.BlockSpec((tm, LANES), lambda i, j: (i, 0)),
        ],
        out_shape=[jax.ShapeDtypeStruct((m, n), BF16), jax.ShapeDtypeStruct((m, LANES), F32)],
        scratch_shapes=[pltpu.VMEM((tm, d), BF16)],
        compiler_params=pltpu.CompilerParams(
            dimension_semantics=("parallel", "arbitrary"), vmem_limit_bytes=VMEM_LIMIT),
        name="inproj",
    )(h, norm_w, w_main, w_dt)


def _ssd_kernel(z_ref, xs_ref, bc_ref, dt_ref, cwx_ref, cbx_ref, cwb_ref, cbb_ref,
                dtb_ref, alog_ref, dskip_ref, nw_ref, expand_ref, tril_ref,
                out_ref, tailx_scr, tailb_scr, state_scr, *, n_heads):
    c = pl.program_id(1)
    d_inner = xs_ref.shape[1]
    gw = d_inner // SSD_GROUPS
    hpg = n_heads // SSD_GROUPS

    @pl.when(c == 0)
    def _():
        tailx_scr[...] = jnp.zeros(tailx_scr.shape, F32)
        tailb_scr[...] = jnp.zeros(tailb_scr.shape, F32)
        state_scr[...] = jnp.zeros(state_scr.shape, F32)

    row = lax.broadcasted_iota(jnp.int32, (BLOCK, 1), 0)
    vmask = (c * BLOCK + row >= N_PAD).astype(F32)
    first_row = lax.broadcasted_iota(jnp.int32, (SUBLANES, 1), 0) == 0

    def conv(x_ref, tail_scr, w_ref, b_ref):
        x = x_ref[...].astype(F32)
        tail = tail_scr[...]
        tail_scr[...] = x[BLOCK - SUBLANES:, :]
        acc = w_ref[0:1, :] * x
        tacc = w_ref[0:1, :] * tail
        for i in range(1, CONV_K):
            rolled = pltpu.roll(acc, 1, axis=0)
            trolled = pltpu.roll(tacc, 1, axis=0)
            shifted = jnp.concatenate(
                [jnp.where(first_row, trolled, rolled[:SUBLANES]), rolled[SUBLANES:]], axis=0)
            acc = w_ref[i:i + 1, :] * x + shifted
            tacc = w_ref[i:i + 1, :] * tail + trolled
        return _silu(acc + b_ref[...]) * vmask

    xs = conv(xs_ref, tailx_scr, cwx_ref, cbx_ref)
    bc = conv(bc_ref, tailb_scr, cwb_ref, cbb_ref)
    gn = SSD_GROUPS * SSD_STATE
    b_all = bc[:, :gn].astype(BF16)
    c_all = bc[:, gn:].astype(BF16)

    dt = jax.nn.softplus(dt_ref[:, :n_heads] + dtb_ref[...]) * vmask
    a = -jnp.exp(alog_ref[...]) * dt
    a_cs = jnp.dot(tril_ref[...], a, preferred_element_type=F32, precision=HIGHEST)
    a_cs_t = a_cs.T
    a_last = a_cs[BLOCK - 1:BLOCK, :]
    decay_states = jnp.exp(a_last - a_cs)
    decay_out = jnp.exp(a_cs)
    chunk_decay = jnp.broadcast_to(jnp.exp(a_last), (SUBLANES, n_heads))
    stacked = jnp.concatenate([dt, decay_states, decay_out, chunk_decay], axis=0)
    st_hi = stacked.astype(BF16)
    st_lo = (stacked - st_hi.astype(F32)).astype(BF16)
    expanded = (jnp.dot(st_hi, expand_ref[...], preferred_element_type=F32)
                + jnp.dot(st_lo, expand_ref[...], preferred_element_type=F32))
    dt_x = expanded[0:BLOCK]
    ds_x = expanded[BLOCK:2 * BLOCK]
    do_x = expanded[2 * BLOCK:3 * BLOCK]
    cd_x = expanded[3 * BLOCK:3 * BLOCK + 1]

    x_dt = xs * dt_x
    x_dt_b = x_dt.astype(BF16)
    x_ds_b = (x_dt * ds_x).astype(BF16)

    li = lax.broadcasted_iota(jnp.int32, (BLOCK, BLOCK), 0)
    si = lax.broadcasted_iota(jnp.int32, (BLOCK, BLOCK), 1)
    causal = si <= li
    first_half = lax.broadcasted_iota(jnp.int32, (BLOCK, 2 * SSD_HEADDIM), 1) < SSD_HEADDIM

    y_groups = []
    for g in range(SSD_GROUPS):
        bg = b_all[:, g * SSD_STATE:(g + 1) * SSD_STATE]
        cg = c_all[:, g * SSD_STATE:(g + 1) * SSD_STATE]
        cols = slice(g * gw, (g + 1) * gw)
        cb = _nt_dot(cg, bg)
        prev = state_scr[g]
        y_off = jnp.dot(cg, prev.astype(BF16), preferred_element_type=F32) * do_x[:, cols]
        new_state = _tn_dot(bg, x_ds_b[:, cols])
        state_scr[g] = cd_x[:, cols] * prev + new_state
        pairs = []
        for jp in range(hpg // 2):
            h0 = g * hpg + 2 * jp
            pc = slice(g * gw + jp * 2 * SSD_HEADDIM, g * gw + (jp + 1) * 2 * SSD_HEADDIM)
            xp = x_dt_b[:, pc]
            ys = []
            for h in (h0, h0 + 1):
                seg = a_cs[:, h:h + 1] - a_cs_t[h:h + 1, :]
                m = (cb * jnp.exp(jnp.where(causal, seg, -jnp.inf))).astype(BF16)
                ys.append(jnp.dot(m, xp, preferred_element_type=F32))
            pairs.append(jnp.where(first_half, ys[0], ys[1]))
        y_groups.append(jnp.concatenate(pairs, axis=1) + y_off)
    y = jnp.concatenate(y_groups, axis=1) + dskip_ref[...] * xs

    gated = y * _silu(z_ref[...].astype(F32))
    outs = []
    for g in range(SSD_GROUPS):
        gg = gated[:, g * gw:(g + 1) * gw]
        outs.append(gg * lax.rsqrt(jnp.mean(gg * gg, axis=-1, keepdims=True) + EPS))
    out_ref[...] = (jnp.concatenate(outs, axis=1) * nw_ref[...]).astype(out_ref.dtype)


def _ssd(proj, dt_raw, conv_w, conv_b, dt_bias, a_log, d_skip, norm_w, bsz, nc, d_inner):
    m = proj.shape[0]
    n_heads = a_log.shape[0]
    bc_w = 2 * SSD_GROUPS * SSD_STATE
    gw = d_inner // SSD_GROUPS
    cwx, cwb = conv_w[:, :d_inner], conv_w[:, d_inner:]
    cbx, cbb = conv_b[None, :d_inner], conv_b[None, d_inner:]
    expand = jnp.repeat(jnp.eye(n_heads, dtype=BF16), SSD_HEADDIM, axis=1)
    tril = jnp.tril(jnp.ones((BLOCK, BLOCK), F32))
    dskip_x = jnp.repeat(d_skip.astype(F32), SSD_HEADDIM)[None, :]
    full = lambda shape: pl.BlockSpec(shape, lambda b, c: (0,) * len(shape))
    row_blk = lambda b, c: b * nc + c
    return pl.pallas_call(
        functools.partial(_ssd_kernel, n_heads=n_heads),
        grid=(bsz, nc),
        in_specs=[
            pl.BlockSpec((BLOCK, d_inner), lambda b, c: (row_blk(b, c), 0)),
            pl.BlockSpec((BLOCK, d_inner), lambda b, c: (row_blk(b, c), 1)),
            pl.BlockSpec((BLOCK, bc_w), lambda b, c: (row_blk(b, c), 2 * d_inner // bc_w)),
            pl.BlockSpec((BLOCK, LANES), lambda b, c: (row_blk(b, c), 0)),
            full((CONV_K, d_inner)), full((1, d_inner)), full((CONV_K, bc_w)), full((1, bc_w)),
            full((1, n_heads)), full((1, n_heads)), full((1, d_inner)), full((1, d_inner)),
            full((n_heads, d_inner)), full((BLOCK, BLOCK)),
        ],
        out_specs=pl.BlockSpec((BLOCK, d_inner), lambda b, c: (row_blk(b, c), 0)),
        out_shape=jax.ShapeDtypeStruct((m, d_inner), BF16),
        scratch_shapes=[
            pltpu.VMEM((SUBLANES, d_inner), F32),
            pltpu.VMEM((SUBLANES, bc_w), F32),
            pltpu.VMEM((SSD_GROUPS, SSD_STATE, gw), F32),
        ],
        compiler_params=pltpu.CompilerParams(
            dimension_semantics=("parallel", "arbitrary"), vmem_limit_bytes=VMEM_LIMIT),
        name="ssd",
    )(proj, proj, proj, dt_raw, cwx, cbx, cwb, cbb, dt_bias[None, :].astype(F32),
      a_log[None, :].astype(F32), dskip_x, norm_w[None, :].astype(F32), expand, tril)


def _attn_kernel(q_ref, k_ref, v_ref, upper_ref, out_ref, *, n_pairs):
    i = pl.program_id(2)
    pw = 2 * ATT_HEADDIM
    lane = lax.broadcasted_iota(jnp.int32, (BLOCK, pw), 1)
    first = lane < ATT_HEADDIM
    row2 = lax.broadcasted_iota(jnp.int32, (2 * BLOCK, BLOCK), 0)
    qpos = i * BLOCK + jnp.where(row2 < BLOCK, row2, row2 - BLOCK)
    kin = lax.broadcasted_iota(jnp.int32, (2 * BLOCK, BLOCK), 1)
    upper = upper_ref[...]
    scale = ATT_HEADDIM ** -0.5
    q2 = []
    for p in range(n_pairs):
        q = q_ref[:, p * pw:(p + 1) * pw] * jnp.asarray(scale, BF16)
        zero = jnp.zeros_like(q)
        q2.append(jnp.concatenate([jnp.where(first, q, zero), jnp.where(first, zero, q)], axis=0))

    def process(kbs, runs, accs):
        starts = [pl.multiple_of(jnp.maximum(kb, 0) * BLOCK, BLOCK) for kb in kbs]
        alloweds = []
        for kb in kbs:
            kpos = kb * BLOCK + kin
            alloweds.append(jnp.logical_and(kpos < qpos, kpos >= N_PAD))
        zs = [[_nt_dot(q2[p], k_ref[pl.ds(start, BLOCK), p * pw:(p + 1) * pw])
               for p in range(n_pairs)] for start in starts]
        log_betas, laters, sums = [], [], []
        for j in range(len(kbs)):
            log_betas.append([])
            laters.append([])
            sums.append([])
            for p in range(n_pairs):
                z = jnp.where(alloweds[j], zs[j][p], MASKED_SCORE)
                l1p = jnp.log(1.0 + jnp.exp(-jnp.abs(z)))
                log_beta = jnp.minimum(z, 0.0) - l1p
                log_stay = log_beta - z
                hi = log_stay.astype(BF16)
                lo = (log_stay - hi.astype(F32)).astype(BF16)
                log_betas[j].append(log_beta)
                laters[j].append(jnp.dot(hi, upper, preferred_element_type=F32)
                                 + jnp.dot(lo, upper, preferred_element_type=F32))
                sums[j].append(jnp.sum(log_stay, axis=1, keepdims=True))
        runs, accs = list(runs), list(accs)
        for j, start in enumerate(starts):
            for p in range(n_pairs):
                vblk = v_ref[pl.ds(start, BLOCK), p * pw:(p + 1) * pw]
                w = jnp.exp(log_betas[j][p] + laters[j][p] + runs[p]).astype(BF16)
                vzero = jnp.zeros_like(vblk)
                accs[p] = (accs[p]
                           + jnp.dot(w[:BLOCK], jnp.where(first, vblk, vzero),
                                     preferred_element_type=F32)
                           + jnp.dot(w[BLOCK:], jnp.where(first, vzero, vblk),
                                     preferred_element_type=F32))
                runs[p] = runs[p] + sums[j][p]
        top = jnp.max(runs[0])
        for r in runs[1:]:
            top = jnp.maximum(top, jnp.max(r))
        return top, runs, accs

    def cond(carry):
        kb, top = carry[0], carry[1]
        return jnp.logical_and(kb >= 0, top > EXP_UNDERFLOW)

    def body(carry):
        kb = carry[0]
        top, runs, accs = process([kb], carry[2:2 + n_pairs], carry[2 + n_pairs:])
        return (kb - 1, top, *runs, *accs)

    zero_runs = [jnp.zeros((2 * BLOCK, 1), F32) for _ in range(n_pairs)]
    zero_accs = [jnp.zeros((BLOCK, pw), F32) for _ in range(n_pairs)]
    top, runs, accs = process([i, i - 1], zero_runs, zero_accs)
    final = lax.while_loop(cond, body, (i - 2, top, *runs, *accs))
    for p in range(n_pairs):
        out_ref[:, p * pw:(p + 1) * pw] = final[2 + n_pairs + p].astype(out_ref.dtype)


def _attention(proj, bsz, nc, q_col, k_col, v_col, att_dim, pairs_per_step=4):
    m = proj.shape[0]
    seq = nc * BLOCK
    sw = pairs_per_step * 2 * ATT_HEADDIM
    upper = jnp.triu(jnp.ones((BLOCK, BLOCK), F32), 1).T.astype(BF16)
    return pl.pallas_call(
        functools.partial(_attn_kernel, n_pairs=pairs_per_step),
        grid=(bsz, att_dim // sw, nc),
        in_specs=[
            pl.BlockSpec((BLOCK, sw), lambda b, p, i: (b * nc + i, q_col // sw + p)),
            pl.BlockSpec((seq, sw), lambda b, p, i: (b, k_col // sw + p)),
            pl.BlockSpec((seq, sw), lambda b, p, i: (b, v_col // sw + p)),
            pl.BlockSpec((BLOCK, BLOCK), lambda b, p, i: (0, 0)),
        ],
        out_specs=pl.BlockSpec((BLOCK, sw), lambda b, p, i: (b * nc + i, p)),
        out_shape=jax.ShapeDtypeStruct((m, att_dim), BF16),
        compiler_params=pltpu.CompilerParams(
            dimension_semantics=("parallel", "parallel", "arbitrary"),
            vmem_limit_bytes=VMEM_LIMIT),
        name="attention",
    )(proj, proj, proj, upper)


def _merge_kernel(yssd_ref, att_ref, gs_ref, ga_ref, h_ref, wso_ref, wao_ref, wo_ref, out_ref):
    t = pl.program_id(1)
    y_ssd = jnp.dot(yssd_ref[...], wso_ref[...], preferred_element_type=F32)
    y_att = jnp.dot(att_ref[...], wao_ref[...], preferred_element_type=F32)
    merged = (jax.nn.sigmoid(gs_ref[...].astype(F32)) * y_ssd
              + jax.nn.sigmoid(ga_ref[...].astype(F32)) * y_att)
    mixed = jnp.dot(merged.astype(BF16), wo_ref[...], preferred_element_type=F32)
    row = lax.broadcasted_iota(jnp.int32, (ROW_TILE, 1), 0)
    valid = (t * ROW_TILE + row >= N_PAD).astype(F32)
    out_ref[...] = h_ref[...] + mixed * valid


def _merge(yssd, att, proj, h, w_ssd_out, w_att_out, w_out, bsz, seq, gs_col, ga_col):
    m, d = h.shape
    d_inner = yssd.shape[1]
    att_dim = att.shape[1]
    per = seq // ROW_TILE
    rb = lambda b, t: b * per + t
    full = lambda shape: pl.BlockSpec(shape, lambda b, t: (0,) * len(shape))
    return pl.pallas_call(
        _merge_kernel,
        grid=(bsz, per),
        in_specs=[
            pl.BlockSpec((ROW_TILE, d_inner), lambda b, t: (rb(b, t), 0)),
            pl.BlockSpec((ROW_TILE, att_dim), lambda b, t: (rb(b, t), 0)),
            pl.BlockSpec((ROW_TILE, d), lambda b, t: (rb(b, t), gs_col // d)),
            pl.BlockSpec((ROW_TILE, d), lambda b, t: (rb(b, t), ga_col // d)),
            pl.BlockSpec((ROW_TILE, d), lambda b, t: (rb(b, t), 0)),
            full((d_inner, d)), full((att_dim, d)), full((d, d)),
        ],
        out_specs=pl.BlockSpec((ROW_TILE, d), lambda b, t: (rb(b, t), 0)),
        out_shape=jax.ShapeDtypeStruct((m, d), F32),
        compiler_params=pltpu.CompilerParams(
            dimension_semantics=("parallel", "parallel"), vmem_limit_bytes=VMEM_LIMIT),
        name="merge",
    )(yssd, att, proj, proj, h, w_ssd_out, w_att_out, w_out)


def _router_kernel(h_ref, nw_ref, wr_ref, br_ref, before_ref,
                   ut_ref, idx_ref, gate_ref, rank_ref, cnt_ref, base_scr):
    step = pl.program_id(0)
    n_exp = wr_ref.shape[0]
    tm = h_ref.shape[0]

    @pl.when(step == 0)
    def _():
        base_scr[...] = jnp.zeros(base_scr.shape, F32)

    x = h_ref[...]
    u = x * lax.rsqrt(jnp.mean(x * x, axis=-1, keepdims=True) + EPS) * nw_ref[...]
    for c in range(SUBLANES):
        ut_ref[pl.ds(c, tm, stride=SUBLANES), :] = u[:, c * LANES:(c + 1) * LANES]

    logits = _nt_dot(wr_ref[...], u, precision=HIGHEST) + br_ref[...]
    eidx = lax.broadcasted_iota(jnp.int32, (n_exp, tm), 0)
    work = logits
    tops, idxs, onehots = [], [], []
    for _ in range(TOP_K):
        top = jnp.max(work, axis=0, keepdims=True)
        idx = jnp.min(jnp.where(work == top, eidx, n_exp), axis=0, keepdims=True)
        hot = eidx == idx
        work = jnp.where(hot, -jnp.inf, work)
        tops.append(top)
        idxs.append(idx)
        onehots.append(hot.astype(F32))
    exps = [jnp.exp(t - tops[0]) for t in tops]
    denom = exps[0] + exps[1] + exps[2] + exps[3]
    cnt = onehots[0] + onehots[1] + onehots[2] + onehots[3]
    before = jnp.dot(cnt.astype(BF16), before_ref[...], preferred_element_type=F32) + base_scr[...]
    ranks = [jnp.sum(hot * before, axis=0, keepdims=True) for hot in onehots]
    pad_rows = SUBLANES - TOP_K
    idx_ref[...] = jnp.concatenate(idxs + [jnp.zeros((pad_rows, tm), jnp.int32)], axis=0)
    gate_ref[...] = jnp.concatenate([e / denom for e in exps] + [jnp.zeros((pad_rows, tm), F32)], axis=0)
    rank_ref[...] = jnp.concatenate(
        [r.astype(jnp.int32) for r in ranks] + [jnp.zeros((pad_rows, tm), jnp.int32)], axis=0)
    base_scr[...] = base_scr[...] + jnp.sum(cnt, axis=1, keepdims=True)
    cnt_ref[...] = jnp.broadcast_to(base_scr[...], cnt_ref.shape)


def _router(h1, norm_w, w_router, b_router):
    m, d = h1.shape
    n_exp = w_router.shape[1]
    before = jnp.triu(jnp.ones((ROW_TILE, ROW_TILE), F32), 1).astype(BF16)
    full = lambda shape: pl.BlockSpec(shape, lambda i: (0,) * len(shape))
    return pl.pallas_call(
        _router_kernel,
        grid=(m // ROW_TILE,),
        in_specs=[
            pl.BlockSpec((ROW_TILE, d), lambda i: (i, 0)),
            full((1, d)), full((n_exp, d)), full((n_exp, 1)), full((ROW_TILE, ROW_TILE)),
        ],
        out_specs=[
            pl.BlockSpec((ROW_TILE * SUBLANES, LANES), lambda i: (i, 0)),
            pl.BlockSpec((SUBLANES, ROW_TILE), lambda i: (0, i)),
            pl.BlockSpec((SUBLANES, ROW_TILE), lambda i: (0, i)),
            pl.BlockSpec((SUBLANES, ROW_TILE), lambda i: (0, i)),
            full((n_exp, LANES)),
        ],
        out_shape=[
            jax.ShapeDtypeStruct((m * SUBLANES, LANES), F32),
            jax.ShapeDtypeStruct((SUBLANES, m), jnp.int32),
            jax.ShapeDtypeStruct((SUBLANES, m), F32),
            jax.ShapeDtypeStruct((SUBLANES, m), jnp.int32),
            jax.ShapeDtypeStruct((n_exp, LANES), F32),
        ],
        scratch_shapes=[pltpu.VMEM((n_exp, 1), F32)],
        compiler_params=pltpu.CompilerParams(
            dimension_semantics=("arbitrary",), vmem_limit_bytes=VMEM_LIMIT),
        name="router",
    )(h1, norm_w, w_router.T, b_router[:, None], before)


def _dispatch_kernel(zstart_ref, zpad_ref, nused_ref, dest_ref, ut_ref, xs_ref, zero_scr, sem, zsem,
                     *, n_exp):
    step = pl.program_id(0)
    tb = dest_ref.shape[1]
    slot_rows = MOE_BLOCK * SUBLANES
    n_blocks = xs_ref.shape[0] // slot_rows

    @pl.when(step == 0)
    def _():
        zero_scr[...] = jnp.zeros(zero_scr.shape, F32)

        def zero_rows(row_start, n_rows):
            return pltpu.make_async_copy(zero_scr.at[pl.ds(0, n_rows)],
                                         xs_ref.at[pl.ds(row_start, n_rows)], zsem)

        def pad_fill(wait):
            def per_expert(e, carry):
                n_pad = zpad_ref[e]
                pos = zstart_ref[e]
                size = MOE_BLOCK // 2
                while size >= 1:
                    take = n_pad & size

                    @pl.when(take != 0)
                    def _(pos=pos, size=size):
                        copy = zero_rows(pl.multiple_of(pos * SUBLANES, SUBLANES), size * SUBLANES)
                        copy.wait() if wait else copy.start()

                    pos = pos + take
                    size //= 2
                return carry

            lax.fori_loop(0, n_exp, per_expert, 0)

        def tail_fill(wait):
            def per_block(b, carry):
                copy = zero_rows(pl.multiple_of(b * slot_rows, slot_rows), slot_rows)
                copy.wait() if wait else copy.start()
                return carry

            lax.fori_loop(nused_ref[0], n_blocks, per_block, 0)

        pad_fill(False)
        tail_fill(False)
        pad_fill(True)
        tail_fill(True)

    def issue(t, carry):
        src = pl.multiple_of(t * SUBLANES, SUBLANES)
        for k in range(TOP_K):
            dst = pl.multiple_of(dest_ref[k, t] * SUBLANES, SUBLANES)
            pltpu.make_async_copy(ut_ref.at[pl.ds(src, SUBLANES)],
                                  xs_ref.at[pl.ds(dst, SUBLANES)], sem).start(priority=k % 2)
        return carry

    lax.fori_loop(0, tb, issue, 0, unroll=4)
    for _ in range(TOP_K):
        pltpu.make_async_copy(ut_ref, xs_ref.at[pl.ds(0, tb * SUBLANES)], sem).wait()


def _dispatch(ut, dest, zstart, zpad, n_used, n_slots):
    m = dest.shape[1]
    n_exp = zstart.shape[0]
    return pl.pallas_call(
        functools.partial(_dispatch_kernel, n_exp=n_exp),
        grid_spec=pltpu.PrefetchScalarGridSpec(
            num_scalar_prefetch=3,
            grid=(m // ROW_TILE,),
            in_specs=[
                pl.BlockSpec((SUBLANES, ROW_TILE), lambda i, *_: (0, i), memory_space=pltpu.SMEM),
                pl.BlockSpec((ROW_TILE * SUBLANES, LANES), lambda i, *_: (i, 0)),
            ],
            out_specs=pl.BlockSpec(memory_space=pl.ANY),
            scratch_shapes=[
                pltpu.VMEM((MOE_BLOCK * SUBLANES, LANES), F32),
                pltpu.SemaphoreType.DMA(()),
                pltpu.SemaphoreType.DMA(()),
            ],
        ),
        out_shape=jax.ShapeDtypeStruct((n_slots * SUBLANES, LANES), F32),
        compiler_params=pltpu.CompilerParams(
            dimension_semantics=("arbitrary",), vmem_limit_bytes=VMEM_LIMIT),
        name="dispatch",
    )(zstart, zpad, n_used, dest, ut)


def _expert_kernel(be_ref, nused_ref, xs_ref, wgu_ref, bg_ref, bl_ref, wd_ref, bd_ref, perm_ref,
                   y_ref, wg_scr, wl_scr, wd_scr):
    b = pl.program_id(0)
    pair = 2 * LANES
    changed = jnp.logical_or(b == 0, be_ref[b] != be_ref[jnp.maximum(b - 1, 0)])

    @pl.when(jnp.logical_and(changed, b < nused_ref[0]))
    def _():
        for cb in range(wgu_ref.shape[2] // pair):
            wb = wgu_ref[0, :, cb * pair:(cb + 1) * pair].astype(BF16)
            sep = jnp.dot(wb, perm_ref[...], preferred_element_type=F32).astype(BF16)
            wg_scr[:, cb * LANES:(cb + 1) * LANES] = sep[:, :LANES]
            wl_scr[:, cb * LANES:(cb + 1) * LANES] = sep[:, LANES:]
        wd_scr[...] = wd_ref[0].astype(BF16)

    @pl.when(b < nused_ref[0])
    def _():
        x = jnp.concatenate(
            [xs_ref[pl.ds(c, MOE_BLOCK, stride=SUBLANES), :] for c in range(SUBLANES)],
            axis=1).astype(BF16)
        gate = jnp.dot(x, wg_scr[...], preferred_element_type=F32) + bg_ref[0]
        lin = jnp.dot(x, wl_scr[...], preferred_element_type=F32) + bl_ref[0]
        gate = jnp.minimum(gate, SWIGLU_LIMIT)
        lin = jnp.clip(lin, -SWIGLU_LIMIT, SWIGLU_LIMIT)
        act = gate * jax.nn.sigmoid(SWIGLU_ALPHA * gate) * (lin + 1.0)
        y = jnp.dot(act.astype(BF16), wd_scr[...], preferred_element_type=F32) + bd_ref[0]
        for c in range(SUBLANES):
            y_ref[pl.ds(c, MOE_BLOCK, stride=SUBLANES), :] = y[:, c * LANES:(c + 1) * LANES]

    @pl.when(b >= nused_ref[0])
    def _():
        y_ref[...] = jnp.zeros(y_ref.shape, F32)


def _experts(xs, block_expert, n_used, w_gate_up, b_gate, b_lin, w_down, b_down, n_blocks):
    n_exp, d, d_ff2 = w_gate_up.shape
    d_ff = d_ff2 // 2
    rows = MOE_BLOCK * SUBLANES
    blk = lambda b, be, nu: (jnp.minimum(b, nu[0] - 1), 0)
    wsel = lambda b, be, nu: (be[b], 0, 0)
    r = jnp.arange(2 * LANES)[:, None]
    c = jnp.arange(2 * LANES)[None, :]
    perm = (r == jnp.where(c < LANES, 2 * c, 2 * (c - LANES) + 1)).astype(BF16)
    return pl.pallas_call(
        _expert_kernel,
        grid_spec=pltpu.PrefetchScalarGridSpec(
            num_scalar_prefetch=2,
            grid=(n_blocks,),
            in_specs=[
                pl.BlockSpec((rows, LANES), blk),
                pl.BlockSpec((1, d, d_ff2), wsel),
                pl.BlockSpec((1, 1, d_ff), wsel), pl.BlockSpec((1, 1, d_ff), wsel),
                pl.BlockSpec((1, d_ff, d), wsel), pl.BlockSpec((1, 1, d), wsel),
                pl.BlockSpec((2 * LANES, 2 * LANES), lambda b, be, nu: (0, 0)),
            ],
            out_specs=pl.BlockSpec((rows, LANES), lambda b, be, nu: (b, 0)),
            scratch_shapes=[
                pltpu.VMEM((d, d_ff), BF16), pltpu.VMEM((d, d_ff), BF16), pltpu.VMEM((d_ff, d), BF16),
            ],
        ),
        out_shape=jax.ShapeDtypeStruct((n_blocks * rows, LANES), F32),
        compiler_params=pltpu.CompilerParams(
            dimension_semantics=("arbitrary",), vmem_limit_bytes=VMEM_LIMIT),
        name="experts",
    )(block_expert, n_used, xs, w_gate_up, b_gate, b_lin, w_down, b_down, perm)


def _combine_kernel(dest_ref, gate_ref, h_ref, nw_ref, eye_ref, ypad_ref, out_ref, *scratch, n_steps):
    bufs = scratch[:COMBINE_BUFFERS]
    sem = scratch[COMBINE_BUFFERS]
    s = pl.program_id(0)
    rows = BLOCK * SUBLANES
    n_groups = BLOCK // SUBLANES

    def step(new, old):
        def issue_group(g):
            for j in range(SUBLANES):
                t = g * SUBLANES + j
                for k in range(TOP_K):
                    src = pl.multiple_of(dest_ref[k, t] * SUBLANES, SUBLANES)
                    dst = pl.multiple_of(k * rows + t * SUBLANES, SUBLANES)
                    pltpu.make_async_copy(ypad_ref.at[pl.ds(src, SUBLANES)],
                                          bufs[new].at[pl.ds(dst, SUBLANES)],
                                          sem.at[new]).start(priority=k % 2)

        def compute():
            gates_t = _nt_dot(eye_ref[...], gate_ref[...], precision=HIGHEST)
            acc = h_ref[...]
            for k in range(TOP_K):
                yk = jnp.concatenate(
                    [bufs[old][pl.ds(k * rows + c, BLOCK, stride=SUBLANES), :]
                     for c in range(SUBLANES)], axis=1)
                acc = acc + gates_t[:, k:k + 1] * yk
            scale = lax.rsqrt(jnp.mean(acc * acc, axis=-1, keepdims=True) + EPS)
            out_ref[...] = acc * scale * nw_ref[...]

        def wait_old():
            pltpu.make_async_copy(ypad_ref.at[pl.ds(0, TOP_K * rows)], bufs[old], sem.at[old]).wait()

        @pl.when(s < COMBINE_LAG)
        def _():
            lax.fori_loop(0, n_groups, lambda g, carry: (issue_group(g), carry)[1], 0)

        @pl.when(jnp.logical_and(s >= COMBINE_LAG, s < n_steps))
        def _():
            wait_old()
            for g in range(n_groups):
                issue_group(g)
            compute()

        @pl.when(s >= n_steps)
        def _():
            wait_old()
            compute()

    for new in range(COMBINE_BUFFERS):
        pl.when(s % COMBINE_BUFFERS == new)(
            functools.partial(step, new, (new + 1) % COMBINE_BUFFERS))


def _combine(ypad, dest, gates, h1, final_norm_w, bsz, nc):
    m, d = h1.shape
    per = nc - 1
    n_steps = bsz * per
    eye = jnp.eye(BLOCK, dtype=F32)

    def tok_blk(s):
        s = jnp.clip(s, 0, n_steps - 1)
        return (s // per) * nc + 1 + s % per

    return pl.pallas_call(
        functools.partial(_combine_kernel, n_steps=n_steps),
        grid=(n_steps + COMBINE_LAG,),
        in_specs=[
            pl.BlockSpec((SUBLANES, BLOCK), lambda s: (0, tok_blk(s)), memory_space=pltpu.SMEM),
            pl.BlockSpec((SUBLANES, BLOCK), lambda s: (0, tok_blk(s - COMBINE_LAG))),
            pl.BlockSpec((BLOCK, d), lambda s: (tok_blk(s - COMBINE_LAG), 0)),
            pl.BlockSpec((1, d), lambda s: (0, 0)),
            pl.BlockSpec((BLOCK, BLOCK), lambda s: (0, 0)),
            pl.BlockSpec(memory_space=pl.ANY),
        ],
        out_specs=pl.BlockSpec((BLOCK, d), lambda s: (jnp.clip(s - COMBINE_LAG, 0, n_steps - 1), 0)),
        out_shape=jax.ShapeDtypeStruct((n_steps * BLOCK, d), F32),
        scratch_shapes=(
            [pltpu.VMEM((TOP_K * BLOCK * SUBLANES, LANES), F32) for _ in range(COMBINE_BUFFERS)]
            + [pltpu.SemaphoreType.DMA((COMBINE_BUFFERS,))]),
        compiler_params=pltpu.CompilerParams(
            dimension_semantics=("arbitrary",), vmem_limit_bytes=VMEM_LIMIT),
        name="combine",
    )(dest, gates, h1, final_norm_w[None, :], eye, ypad)


def _layer(h, bsz, nc, mix_norm_w, w_in, conv_w, conv_b, dt_bias, a_log, d_skip, ssd_norm_w,
           w_ssd_out, w_att_out, w_out, ffn_norm_w, w_router, b_router, w_gate_up, b_gate_up,
           w_down, b_down):
    m, d = h.shape
    seq = nc * BLOCK
    n_heads = a_log.shape[0]
    d_inner = n_heads * SSD_HEADDIM
    conv_dim = d_inner + 2 * SSD_GROUPS * SSD_STATE
    att_dim = w_att_out.shape[0]
    n_exp = w_router.shape[1]

    o_xbc = d_inner
    o_dt = o_xbc + conv_dim
    o_q = o_dt + n_heads
    w_main = jnp.concatenate([w_in[:, :o_dt], w_in[:, o_q:]], axis=1).astype(BF16)
    w_dt = jnp.pad(w_in[:, o_dt:o_q], ((0, 0), (0, LANES - n_heads))).astype(BF16)
    q_col = o_dt
    k_col = q_col + att_dim
    v_col = k_col + att_dim
    gs_col = v_col + att_dim
    ga_col = gs_col + d

    proj, dt_raw = _inproj(h, mix_norm_w[None, :], w_main, w_dt)
    yssd = _ssd(proj, dt_raw, conv_w, conv_b, dt_bias, a_log, d_skip, ssd_norm_w, bsz, nc, d_inner)
    att = _attention(proj, bsz, nc, q_col, k_col, v_col, att_dim)
    h1 = _merge(yssd, att, proj, h, w_ssd_out.astype(BF16), w_att_out.astype(BF16),
                w_out.astype(BF16), bsz, seq, gs_col, ga_col)

    ut, idx, gates, rank, cnt = _router(h1, ffn_norm_w[None, :], w_router, b_router)

    counts = cnt[:, 0].astype(jnp.int32)
    padded = (counts + MOE_BLOCK - 1) // MOE_BLOCK * MOE_BLOCK
    padded_ends = jnp.cumsum(padded)
    padded_starts = padded_ends - padded
    n_blocks = (m * TOP_K + MOE_BLOCK - 1) // MOE_BLOCK + n_exp
    dest = rank + jnp.sum(
        jnp.where(idx[None] == jnp.arange(n_exp, dtype=jnp.int32)[:, None, None],
                  padded_starts[:, None, None], 0), axis=0)
    block_first = jnp.arange(n_blocks, dtype=jnp.int32) * MOE_BLOCK
    block_expert = jnp.minimum(
        jnp.sum((padded_ends[None, :] <= block_first[:, None]).astype(jnp.int32), axis=1), n_exp - 1)
    n_used = (padded_ends[-1:] // MOE_BLOCK).astype(jnp.int32)
    zstart = (padded_starts + counts).astype(jnp.int32)

    zpad = (padded - counts).astype(jnp.int32)
    xs = _dispatch(ut, dest, zstart, zpad, n_used, n_blocks * MOE_BLOCK)
    b_gate = b_gate_up[:, None, 0::2]
    b_lin = b_gate_up[:, None, 1::2]
    ypad = _experts(xs, block_expert, n_used, w_gate_up, b_gate, b_lin, w_down, b_down[:, None, :],
                    n_blocks)
    return h1, ypad, dest, gates


def kernel(x, meta_tokens, mix_norm_w, w_in, conv_w, conv_b, dt_bias, a_log, d_skip, ssd_norm_w,
           w_ssd_out, w_att_out, w_out, ffn_norm_w, w_router, b_router, w_gate_up, b_gate_up,
           w_down, b_down, final_norm_w):
    bsz, seq_x, d = x.shape
    depth = mix_norm_w.shape[0]
    assert depth == 1 and seq_x % BLOCK == 0 and (seq_x + N_LEAD) % ROW_TILE == 0
    nc = (seq_x + N_LEAD) // BLOCK
    lead = jnp.concatenate([jnp.zeros((N_PAD, d), x.dtype), meta_tokens.astype(x.dtype)], axis=0)
    h = jnp.concatenate([jnp.broadcast_to(lead[None], (bsz, N_LEAD, d)), x], axis=1)
    h = h.reshape(bsz * nc * BLOCK, d)
    layer = 0
    h1, ypad, dest, gates = _layer(
        h, bsz, nc, mix_norm_w[layer], w_in[layer], conv_w[layer], conv_b[layer], dt_bias[layer],
        a_log[layer], d_skip[layer], ssd_norm_w[layer], w_ssd_out[layer], w_att_out[layer],
        w_out[layer], ffn_norm_w[layer], w_router[layer], b_router[layer], w_gate_up[layer],
        b_gate_up[layer], w_down[layer], b_down[layer])
    out = _combine(ypad, dest, gates, h1, final_norm_w, bsz, nc)
    return out.reshape(bsz, seq_x, d)
```

```python
import functools

import jax
import jax.numpy as jnp
from jax import lax
from jax.experimental import pallas as pl
from jax.experimental.pallas import tpu as pltpu

F32 = jnp.float32
BF16 = jnp.bfloat16
HIGHEST = lax.Precision.HIGHEST

N_META = 16
BLOCK = 128
N_LEAD = BLOCK
N_PAD = N_LEAD - N_META
EPS = 1e-5
SSD_HEADDIM = 64
SSD_GROUPS = 4
SSD_STATE = 128
CONV_K = 4
ATT_HEADDIM = 64
TOP_K = 4
SWIGLU_LIMIT = 7.0
SWIGLU_ALPHA = 1.702
MOE_BLOCK = 512

LANES = 128
SUBLANES = 8
ROW_TILE = 640
INPROJ_ROWS = 1664
COMBINE_LAG = 2
COMBINE_BUFFERS = COMBINE_LAG + 1
EXP_UNDERFLOW = -88.0
MASKED_SCORE = -1e30
VMEM_LIMIT = 56 * 1024 * 1024


def _nt_dot(a, b, precision=None):
    return lax.dot_general(a, b, (((1,), (1,)), ((), ())),
                           preferred_element_type=F32, precision=precision)


def _tn_dot(a, b):
    return lax.dot_general(a, b, (((0,), (0,)), ((), ())), preferred_element_type=F32)


def _silu(x):
    half = 0.5 * x
    return half + half * jnp.tanh(half)


def _inproj_kernel(h_ref, nw_ref, w_ref, wdt_ref, out_ref, dt_ref, u_scr):
    @pl.when(pl.program_id(1) == 0)
    def _():
        x = h_ref[...]
        u = x * lax.rsqrt(jnp.mean(x * x, axis=-1, keepdims=True) + EPS) * nw_ref[...]
        ub = u.astype(BF16)
        u_scr[...] = ub
        dt_ref[...] = jnp.dot(ub, wdt_ref[...], preferred_element_type=F32)

    out_ref[...] = jnp.dot(u_scr[...], w_ref[...], preferred_element_type=F32).astype(out_ref.dtype)


def _largest_row_tile(m, cap):
    return max(t for t in range(BLOCK, cap + 1, BLOCK) if m % t == 0)


def _inproj(h, norm_w, w_main, w_dt, tn=1024):
    m, d = h.shape
    n = w_main.shape[1]
    tm = _largest_row_tile(m, INPROJ_ROWS)
    return pl.pallas_call(
        _inproj_kernel,
        grid=(m // tm, n // tn),
        in_specs=[
            pl.BlockSpec((tm, d), lambda i, j: (i, 0)),
            pl.BlockSpec((1, d), lambda i, j: (0, 0)),
            pl.BlockSpec((d, tn), lambda i, j: (0, j)),
            pl.BlockSpec((d, LANES), lambda i, j: (0, 0)),
        ],
        out_specs=[
            pl.BlockSpec((tm, tn), lambda i, j: (i, j)),
            pl.BlockSpec((tm, LANES), lambda i, j: (i, 0)),
        ],
        out_shape=[jax.ShapeDtypeStruct((m, n), BF16), jax.ShapeDtypeStruct((m, LANES), F32)],
        scratch_shapes=[pltpu.VMEM((tm, d), BF16)],
        compiler_params=pltpu.CompilerParams(
            dimension_semantics=("parallel", "arbitrary"), vmem_limit_bytes=VMEM_LIMIT),
        name="inproj",
    )(h, norm_w, w_main, w_dt)


def _ssd_kernel(z_ref, xs_ref, bc_ref, dt_ref, cwx_ref, cbx_ref, cwb_ref, cbb_ref,
                dtb_ref, alog_ref, dskip_ref, nw_ref, expand_ref, tril_ref,
                out_ref, tailx_scr, tailb_scr, state_scr, *, n_heads):
    c = pl.program_id(1)
    d_inner = xs_ref.shape[1]
    gw = d_inner // SSD_GROUPS
    hpg = n_heads // SSD_GROUPS

    @pl.when(c == 0)
    def _():
        tailx_scr[...] = jnp.zeros(tailx_scr.shape, F32)
        tailb_scr[...] = jnp.zeros(tailb_scr.shape, F32)
        state_scr[...] = jnp.zeros(state_scr.shape, F32)

    row = lax.broadcasted_iota(jnp.int32, (BLOCK, 1), 0)
    vmask = (c * BLOCK + row >= N_PAD).astype(F32)
    first_row = lax.broadcasted_iota(jnp.int32, (SUBLANES, 1), 0) == 0

    def conv(x_ref, tail_scr, w_ref, b_ref):
        x = x_ref[...].astype(F32)
        tail = tail_scr[...]
        tail_scr[...] = x[BLOCK - SUBLANES:, :]
        groups = [x[r * SUBLANES:(r + 1) * SUBLANES, :] for r in range(BLOCK // SUBLANES)]
        accs = [w_ref[0:1, :] * g for g in groups]
        tacc = w_ref[0:1, :] * tail
        for i in range(1, CONV_K):
            trot = pltpu.roll(tacc, 1, axis=0)
            rots = [pltpu.roll(a, 1, axis=0) for a in accs]
            befores = [trot] + rots[:-1]
            accs = [w_ref[i:i + 1, :] * g + jnp.where(first_row, before, rot)
                    for g, before, rot in zip(groups, befores, rots)]
            tacc = w_ref[i:i + 1, :] * tail + trot
        return _silu(jnp.concatenate(accs, axis=0) + b_ref[...]) * vmask

    xs = conv(xs_ref, tailx_scr, cwx_ref, cbx_ref)
    bc = conv(bc_ref, tailb_scr, cwb_ref, cbb_ref)
    gn = SSD_GROUPS * SSD_STATE
    b_all = bc[:, :gn].astype(BF16)
    c_all = bc[:, gn:].astype(BF16)

    dt = jax.nn.softplus(dt_ref[:, :n_heads] + dtb_ref[...]) * vmask
    a = -jnp.exp(alog_ref[...]) * dt
    a_cs = jnp.dot(tril_ref[...], a, preferred_element_type=F32, precision=HIGHEST)
    a_cs_t = a_cs.T
    a_last = a_cs[BLOCK - 1:BLOCK, :]
    decay_states = jnp.exp(a_last - a_cs)
    decay_out = jnp.exp(a_cs)
    chunk_decay = jnp.broadcast_to(jnp.exp(a_last), (SUBLANES, n_heads))
    stacked = jnp.concatenate([dt, decay_states, decay_out, chunk_decay], axis=0)
    st_hi = stacked.astype(BF16)
    st_lo = (stacked - st_hi.astype(F32)).astype(BF16)
    expanded = (jnp.dot(st_hi, expand_ref[...], preferred_element_type=F32)
                + jnp.dot(st_lo, expand_ref[...], preferred_element_type=F32))
    dt_x = expanded[0:BLOCK]
    ds_x = expanded[BLOCK:2 * BLOCK]
    do_x = expanded[2 * BLOCK:3 * BLOCK]
    cd_x = expanded[3 * BLOCK:3 * BLOCK + 1]

    x_dt = xs * dt_x
    x_dt_b = x_dt.astype(BF16)
    x_ds_b = (x_dt * ds_x).astype(BF16)

    li = lax.broadcasted_iota(jnp.int32, (BLOCK, BLOCK), 0)
    si = lax.broadcasted_iota(jnp.int32, (BLOCK, BLOCK), 1)
    causal = si <= li
    first_half = lax.broadcasted_iota(jnp.int32, (BLOCK, 2 * SSD_HEADDIM), 1) < SSD_HEADDIM

    y_groups = []
    for g in range(SSD_GROUPS):
        bg = b_all[:, g * SSD_STATE:(g + 1) * SSD_STATE]
        cg = c_all[:, g * SSD_STATE:(g + 1) * SSD_STATE]
        cols = slice(g * gw, (g + 1) * gw)
        cb = _nt_dot(cg, bg)
        prev = state_scr[g]
        y_off = jnp.dot(cg, prev.astype(BF16), preferred_element_type=F32) * do_x[:, cols]
        new_state = _tn_dot(bg, x_ds_b[:, cols])
        state_scr[g] = cd_x[:, cols] * prev + new_state
        pairs = []
        for jp in range(hpg // 2):
            h0 = g * hpg + 2 * jp
            pc = slice(g * gw + jp * 2 * SSD_HEADDIM, g * gw + (jp + 1) * 2 * SSD_HEADDIM)
            xp = x_dt_b[:, pc]
            ys = []
            for h in (h0, h0 + 1):
                seg = a_cs[:, h:h + 1] - a_cs_t[h:h + 1, :]
                m = (cb * jnp.exp(jnp.where(causal, seg, -jnp.inf))).astype(BF16)
                ys.append(jnp.dot(m, xp, preferred_element_type=F32))
            pairs.append(jnp.where(first_half, ys[0], ys[1]))
        y_groups.append(jnp.concatenate(pairs, axis=1) + y_off)
    y = jnp.concatenate(y_groups, axis=1) + dskip_ref[...] * xs

    gated = y * _silu(z_ref[...].astype(F32))
    outs = []
    for g in range(SSD_GROUPS):
        gg = gated[:, g * gw:(g + 1) * gw]
        outs.append(gg * lax.rsqrt(jnp.mean(gg * gg, axis=-1, keepdims=True) + EPS))
    out_ref[...] = (jnp.concatenate(outs, axis=1) * nw_ref[...]).astype(out_ref.dtype)


def _ssd(proj, dt_raw, conv_w, conv_b, dt_bias, a_log, d_skip, norm_w, bsz, nc, d_inner):
    m = proj.shape[0]
    n_heads = a_log.shape[0]
    bc_w = 2 * SSD_GROUPS * SSD_STATE
    gw = d_inner // SSD_GROUPS
    cwx, cwb = conv_w[:, :d_inner], conv_w[:, d_inner:]
    cbx, cbb = conv_b[None, :d_inner], conv_b[None, d_inner:]
    expand = jnp.repeat(jnp.eye(n_heads, dtype=BF16), SSD_HEADDIM, axis=1)
    tril = jnp.tril(jnp.ones((BLOCK, BLOCK), F32))
    dskip_x = jnp.repeat(d_skip.astype(F32), SSD_HEADDIM)[None, :]
    full = lambda shape: pl.BlockSpec(shape, lambda b, c: (0,) * len(shape))
    row_blk = lambda b, c: b * nc + c
    return pl.pallas_call(
        functools.partial(_ssd_kernel, n_heads=n_heads),
        grid=(bsz, nc),
        in_specs=[
            pl.BlockSpec((BLOCK, d_inner), lambda b, c: (row_blk(b, c), 0)),
            pl.BlockSpec((BLOCK, d_inner), lambda b, c: (row_blk(b, c), 1)),
            pl.BlockSpec((BLOCK, bc_w), lambda b, c: (row_blk(b, c), 2 * d_inner // bc_w)),
            pl.BlockSpec((BLOCK, LANES), lambda b, c: (row_blk(b, c), 0)),
            full((CONV_K, d_inner)), full((1, d_inner)), full((CONV_K, bc_w)), full((1, bc_w)),
            full((1, n_heads)), full((1, n_heads)), full((1, d_inner)), full((1, d_inner)),
            full((n_heads, d_inner)), full((BLOCK, BLOCK)),
        ],
        out_specs=pl.BlockSpec((BLOCK, d_inner), lambda b, c: (row_blk(b, c), 0)),
        out_shape=jax.ShapeDtypeStruct((m, d_inner), BF16),
        scratch_shapes=[
            pltpu.VMEM((SUBLANES, d_inner), F32),
            pltpu.VMEM((SUBLANES, bc_w), F32),
            pltpu.VMEM((SSD_GROUPS, SSD_STATE, gw), F32),
        ],
        compiler_params=pltpu.CompilerParams(
            dimension_semantics=("parallel", "arbitrary"), vmem_limit_bytes=VMEM_LIMIT),
        name="ssd",
    )(proj, proj, proj, dt_raw, cwx, cbx, cwb, cbb, dt_bias[None, :].astype(F32),
      a_log[None, :].astype(F32), dskip_x, norm_w[None, :].astype(F32), expand, tril)


def _attn_kernel(q_ref, k_ref, v_ref, upper_ref, out_ref, *, n_pairs):
    i = pl.program_id(2)
    pw = 2 * ATT_HEADDIM
    lane = lax.broadcasted_iota(jnp.int32, (BLOCK, pw), 1)
    first = lane < ATT_HEADDIM
    row2 = lax.broadcasted_iota(jnp.int32, (2 * BLOCK, BLOCK), 0)
    qpos = i * BLOCK + jnp.where(row2 < BLOCK, row2, row2 - BLOCK)
    kin = lax.broadcasted_iota(jnp.int32, (2 * BLOCK, BLOCK), 1)
    upper = upper_ref[...]
    scale = ATT_HEADDIM ** -0.5
    q2 = []
    for p in range(n_pairs):
        q = q_ref[:, p * pw:(p + 1) * pw] * jnp.asarray(scale, BF16)
        zero = jnp.zeros_like(q)
        q2.append(jnp.concatenate([jnp.where(first, q, zero), jnp.where(first, zero, q)], axis=0))

    def process(kbs, runs, accs):
        starts = [pl.multiple_of(jnp.maximum(kb, 0) * BLOCK, BLOCK) for kb in kbs]
        alloweds = []
        for kb in kbs:
            kpos = kb * BLOCK + kin
            alloweds.append(jnp.logical_and(kpos < qpos, kpos >= N_PAD))
        zs = [[_nt_dot(q2[p], k_ref[pl.ds(start, BLOCK), p * pw:(p + 1) * pw])
               for p in range(n_pairs)] for start in starts]
        log_betas, laters, sums = [], [], []
        for j in range(len(kbs)):
            log_betas.append([])
            laters.append([])
            sums.append([])
            for p in range(n_pairs):
                z = jnp.where(alloweds[j], zs[j][p], MASKED_SCORE)
                l1p = jnp.log(1.0 + jnp.exp(-jnp.abs(z)))
                log_beta = jnp.minimum(z, 0.0) - l1p
                log_stay = log_beta - z
                hi = log_stay.astype(BF16)
                lo = (log_stay - hi.astype(F32)).astype(BF16)
                log_betas[j].append(log_beta)
                laters[j].append(jnp.dot(hi, upper, preferred_element_type=F32)
                                 + jnp.dot(lo, upper, preferred_element_type=F32))
                sums[j].append(jnp.sum(log_stay, axis=1, keepdims=True))
        runs, accs = list(runs), list(accs)
        for j, start in enumerate(starts):
            for p in range(n_pairs):
                vblk = v_ref[pl.ds(start, BLOCK), p * pw:(p + 1) * pw]
                w = jnp.exp(log_betas[j][p] + laters[j][p] + runs[p]).astype(BF16)
                vzero = jnp.zeros_like(vblk)
                accs[p] = (accs[p]
                           + jnp.dot(w[:BLOCK], jnp.where(first, vblk, vzero),
                                     preferred_element_type=F32)
                           + jnp.dot(w[BLOCK:], jnp.where(first, vzero, vblk),
                                     preferred_element_type=F32))
                runs[p] = runs[p] + sums[j][p]
        top = jnp.max(runs[0])
        for r in runs[1:]:
            top = jnp.maximum(top, jnp.max(r))
        return top, runs, accs

    def cond(carry):
        kb, top = carry[0], carry[1]
        return jnp.logical_and(kb >= 0, top > EXP_UNDERFLOW)

    def body(carry):
        kb = carry[0]
        top, runs, accs = process([kb], carry[2:2 + n_pairs], carry[2 + n_pairs:])
        return (kb - 1, top, *runs, *accs)

    zero_runs = [jnp.zeros((2 * BLOCK, 1), F32) for _ in range(n_pairs)]
    zero_accs = [jnp.zeros((BLOCK, pw), F32) for _ in range(n_pairs)]
    top, runs, accs = process([i, i - 1], zero_runs, zero_accs)
    final = lax.while_loop(cond, body, (i - 2, top, *runs, *accs))
    for p in range(n_pairs):
        out_ref[:, p * pw:(p + 1) * pw] = final[2 + n_pairs + p].astype(out_ref.dtype)


def _attention(proj, bsz, nc, q_col, k_col, v_col, att_dim, pairs_per_step=4):
    m = proj.shape[0]
    seq = nc * BLOCK
    sw = pairs_per_step * 2 * ATT_HEADDIM
    upper = jnp.triu(jnp.ones((BLOCK, BLOCK), F32), 1).T.astype(BF16)
    return pl.pallas_call(
        functools.partial(_attn_kernel, n_pairs=pairs_per_step),
        grid=(bsz, att_dim // sw, nc),
        in_specs=[
            pl.BlockSpec((BLOCK, sw), lambda b, p, i: (b * nc + i, q_col // sw + p)),
            pl.BlockSpec((seq, sw), lambda b, p, i: (b, k_col // sw + p)),
            pl.BlockSpec((seq, sw), lambda b, p, i: (b, v_col // sw + p)),
            pl.BlockSpec((BLOCK, BLOCK), lambda b, p, i: (0, 0)),
        ],
        out_specs=pl.BlockSpec((BLOCK, sw), lambda b, p, i: (b * nc + i, p)),
        out_shape=jax.ShapeDtypeStruct((m, att_dim), BF16),
        compiler_params=pltpu.CompilerParams(
            dimension_semantics=("parallel", "parallel", "arbitrary"),
            vmem_limit_bytes=VMEM_LIMIT),
        name="attention",
    )(proj, proj, proj, upper)


def _merge_kernel(yssd_ref, att_ref, gs_ref, ga_ref, h_ref, wso_ref, wao_ref, wo_ref, out_ref):
    t = pl.program_id(1)
    y_ssd = jnp.dot(yssd_ref[...], wso_ref[...], preferred_element_type=F32)
    y_att = jnp.dot(att_ref[...], wao_ref[...], preferred_element_type=F32)
    merged = (jax.nn.sigmoid(gs_ref[...].astype(F32)) * y_ssd
              + jax.nn.sigmoid(ga_ref[...].astype(F32)) * y_att)
    mixed = jnp.dot(merged.astype(BF16), wo_ref[...], preferred_element_type=F32)
    row = lax.broadcasted_iota(jnp.int32, (ROW_TILE, 1), 0)
    valid = (t * ROW_TILE + row >= N_PAD).astype(F32)
    out_ref[...] = h_ref[...] + mixed * valid


def _merge(yssd, att, proj, h, w_ssd_out, w_att_out, w_out, bsz, seq, gs_col, ga_col):
    m, d = h.shape
    d_inner = yssd.shape[1]
    att_dim = att.shape[1]
    per = seq // ROW_TILE
    rb = lambda b, t: b * per + t
    full = lambda shape: pl.BlockSpec(shape, lambda b, t: (0,) * len(shape))
    return pl.pallas_call(
        _merge_kernel,
        grid=(bsz, per),
        in_specs=[
            pl.BlockSpec((ROW_TILE, d_inner), lambda b, t: (rb(b, t), 0)),
            pl.BlockSpec((ROW_TILE, att_dim), lambda b, t: (rb(b, t), 0)),
            pl.BlockSpec((ROW_TILE, d), lambda b, t: (rb(b, t), gs_col // d)),
            pl.BlockSpec((ROW_TILE, d), lambda b, t: (rb(b, t), ga_col // d)),
            pl.BlockSpec((ROW_TILE, d), lambda b, t: (rb(b, t), 0)),
            full((d_inner, d)), full((att_dim, d)), full((d, d)),
        ],
        out_specs=pl.BlockSpec((ROW_TILE, d), lambda b, t: (rb(b, t), 0)),
        out_shape=jax.ShapeDtypeStruct((m, d), F32),
        compiler_params=pltpu.CompilerParams(
            dimension_semantics=("parallel", "parallel"), vmem_limit_bytes=VMEM_LIMIT),
        name="merge",
    )(yssd, att, proj, proj, h, w_ssd_out, w_att_out, w_out)


def _router_kernel(h_ref, nw_ref, wrh_ref, wrl_ref, br_ref, before_ref,
                   ut_ref, idx_ref, gate_ref, rank_ref, cnt_ref, base_scr):
    step = pl.program_id(0)
    n_exp = br_ref.shape[0]
    tm = h_ref.shape[0]

    @pl.when(step == 0)
    def _():
        base_scr[...] = jnp.zeros(base_scr.shape, F32)

    x = h_ref[...]
    u = x * lax.rsqrt(jnp.mean(x * x, axis=-1, keepdims=True) + EPS) * nw_ref[...]
    for c in range(SUBLANES):
        ut_ref[pl.ds(c, tm, stride=SUBLANES), :] = u[:, c * LANES:(c + 1) * LANES]

    u_hi = u.astype(BF16)
    u_lo = (u - u_hi.astype(F32)).astype(BF16)
    logits_tm = (jnp.dot(u_hi, wrh_ref[...], preferred_element_type=F32)
                 + jnp.dot(u_lo, wrh_ref[...], preferred_element_type=F32)
                 + jnp.dot(u_hi, wrl_ref[...], preferred_element_type=F32))
    logits = logits_tm.T[:n_exp] + br_ref[...]
    eidx = lax.broadcasted_iota(jnp.int32, (n_exp, tm), 0)
    work = logits
    tops, idxs, onehots = [], [], []
    for _ in range(TOP_K):
        top = jnp.max(work, axis=0, keepdims=True)
        idx = jnp.min(jnp.where(work == top, eidx, n_exp), axis=0, keepdims=True)
        hot = eidx == idx
        work = jnp.where(hot, -jnp.inf, work)
        tops.append(top)
        idxs.append(idx)
        onehots.append(hot.astype(F32))
    exps = [jnp.exp(v - tops[0]) for v in tops]
    denom = exps[0] + exps[1] + exps[2] + exps[3]
    cnt = onehots[0] + onehots[1] + onehots[2] + onehots[3]
    before = jnp.dot(cnt.astype(BF16), before_ref[...], preferred_element_type=F32) + base_scr[...]
    ranks = [jnp.sum(hot * before, axis=0, keepdims=True) for hot in onehots]
    pad_rows = SUBLANES - TOP_K
    idx_ref[...] = jnp.concatenate(idxs + [jnp.zeros((pad_rows, tm), jnp.int32)], axis=0)
    gate_ref[...] = jnp.concatenate([e / denom for e in exps] + [jnp.zeros((pad_rows, tm), F32)], axis=0)
    rank_ref[...] = jnp.concatenate(
        [r.astype(jnp.int32) for r in ranks] + [jnp.zeros((pad_rows, tm), jnp.int32)], axis=0)
    base_scr[...] = base_scr[...] + jnp.sum(cnt, axis=1, keepdims=True)
    cnt_ref[...] = jnp.broadcast_to(base_scr[...], cnt_ref.shape)


def _router(h1, norm_w, w_router, b_router):
    m, d = h1.shape
    n_exp = w_router.shape[1]
    before = jnp.triu(jnp.ones((ROW_TILE, ROW_TILE), F32), 1).astype(BF16)
    wr = jnp.pad(w_router.astype(F32), ((0, 0), (0, LANES - n_exp)))
    wr_hi = wr.astype(BF16)
    wr_lo = (wr - wr_hi.astype(F32)).astype(BF16)
    full = lambda shape: pl.BlockSpec(shape, lambda i: (0,) * len(shape))
    return pl.pallas_call(
        _router_kernel,
        grid=(m // ROW_TILE,),
        in_specs=[
            pl.BlockSpec((ROW_TILE, d), lambda i: (i, 0)),
            full((1, d)), full((d, LANES)), full((d, LANES)), full((n_exp, 1)),
            full((ROW_TILE, ROW_TILE)),
        ],
        out_specs=[
            pl.BlockSpec((ROW_TILE * SUBLANES, LANES), lambda i: (i, 0)),
            pl.BlockSpec((SUBLANES, ROW_TILE), lambda i: (0, i)),
            pl.BlockSpec((SUBLANES, ROW_TILE), lambda i: (0, i)),
            pl.BlockSpec((SUBLANES, ROW_TILE), lambda i: (0, i)),
            full((n_exp, LANES)),
        ],
        out_shape=[
            jax.ShapeDtypeStruct((m * SUBLANES, LANES), F32),
            jax.ShapeDtypeStruct((SUBLANES, m), jnp.int32),
            jax.ShapeDtypeStruct((SUBLANES, m), F32),
            jax.ShapeDtypeStruct((SUBLANES, m), jnp.int32),
            jax.ShapeDtypeStruct((n_exp, LANES), F32),
        ],
        scratch_shapes=[pltpu.VMEM((n_exp, 1), F32)],
        compiler_params=pltpu.CompilerParams(
            dimension_semantics=("arbitrary",), vmem_limit_bytes=VMEM_LIMIT),
        name="router",
    )(h1, norm_w, wr_hi, wr_lo, b_router[:, None], before)


def _dispatch_kernel(zstart_ref, zpad_ref, nused_ref, dest_ref, ut_ref, xs_ref, zero_scr, sem, zsem,
                     *, n_exp):
    step = pl.program_id(0)
    tb = dest_ref.shape[1]
    slot_rows = MOE_BLOCK * SUBLANES
    n_blocks = xs_ref.shape[0] // slot_rows

    @pl.when(step == 0)
    def _():
        zero_scr[...] = jnp.zeros(zero_scr.shape, F32)

        def zero_rows(row_start, n_rows):
            return pltpu.make_async_copy(zero_scr.at[pl.ds(0, n_rows)],
                                         xs_ref.at[pl.ds(row_start, n_rows)], zsem)

        def pad_fill(wait):
            def per_expert(e, carry):
                n_pad = zpad_ref[e]
                pos = zstart_ref[e]
                size = MOE_BLOCK // 2
                while size >= 1:
                    take = n_pad & size

                    @pl.when(take != 0)
                    def _(pos=pos, size=size):
                        copy = zero_rows(pl.multiple_of(pos * SUBLANES, SUBLANES), size * SUBLANES)
                        copy.wait() if wait else copy.start()

                    pos = pos + take
                    size //= 2
                return carry

            lax.fori_loop(0, n_exp, per_expert, 0)

        def tail_fill(wait):
            def per_block(b, carry):
                copy = zero_rows(pl.multiple_of(b * slot_rows, slot_rows), slot_rows)
                copy.wait() if wait else copy.start()
                return carry

            lax.fori_loop(nused_ref[0], n_blocks, per_block, 0)

        pad_fill(False)
        tail_fill(False)
        pad_fill(True)
        tail_fill(True)

    def issue(t, carry):
        src = pl.multiple_of(t * SUBLANES, SUBLANES)
        for k in range(TOP_K):
            dst = pl.multiple_of(dest_ref[k, t] * SUBLANES, SUBLANES)
            pltpu.make_async_copy(ut_ref.at[pl.ds(src, SUBLANES)],
                                  xs_ref.at[pl.ds(dst, SUBLANES)], sem).start(priority=k % 2)
        return carry

    lax.fori_loop(0, tb, issue, 0, unroll=4)
    for _ in range(TOP_K):
        pltpu.make_async_copy(ut_ref, xs_ref.at[pl.ds(0, tb * SUBLANES)], sem).wait()


def _dispatch(ut, dest, zstart, zpad, n_used, n_slots):
    m = dest.shape[1]
    n_exp = zstart.shape[0]
    return pl.pallas_call(
        functools.partial(_dispatch_kernel, n_exp=n_exp),
        grid_spec=pltpu.PrefetchScalarGridSpec(
            num_scalar_prefetch=3,
            grid=(m // ROW_TILE,),
            in_specs=[
                pl.BlockSpec((SUBLANES, ROW_TILE), lambda i, *_: (0, i), memory_space=pltpu.SMEM),
                pl.BlockSpec((ROW_TILE * SUBLANES, LANES), lambda i, *_: (i, 0)),
            ],
            out_specs=pl.BlockSpec(memory_space=pl.ANY),
            scratch_shapes=[
                pltpu.VMEM((MOE_BLOCK * SUBLANES, LANES), F32),
                pltpu.SemaphoreType.DMA(()),
                pltpu.SemaphoreType.DMA(()),
            ],
        ),
        out_shape=jax.ShapeDtypeStruct((n_slots * SUBLANES, LANES), F32),
        compiler_params=pltpu.CompilerParams(
            dimension_semantics=("arbitrary",), vmem_limit_bytes=VMEM_LIMIT),
        name="dispatch",
    )(zstart, zpad, n_used, dest, ut)


def _expert_kernel(be_ref, nused_ref, slot_ref, next_ref, xs_ref, wgu_hbm, bg_ref, bl_ref, wd_hbm,
                   bd_ref, perm_ref, y_ref, wgu_buf, wd_buf, wg_scr, wl_scr, wd_scr, sem):
    b = pl.program_id(0)
    pair = 2 * LANES
    changed = jnp.logical_or(b == 0, be_ref[b] != be_ref[jnp.maximum(b - 1, 0)])

    def fetch(expert, slot):
        return (pltpu.make_async_copy(wgu_hbm.at[expert], wgu_buf.at[slot], sem.at[0, slot]),
                pltpu.make_async_copy(wd_hbm.at[expert], wd_buf.at[slot], sem.at[1, slot]))

    @pl.when(b == 0)
    def _():
        for copy in fetch(be_ref[0], slot_ref[0]):
            copy.start()

    @pl.when(jnp.logical_and(changed, b < nused_ref[0]))
    def _():
        slot = slot_ref[b]
        for copy in fetch(be_ref[b], slot):
            copy.wait()

        @pl.when(next_ref[b] >= 0)
        def _():
            for copy in fetch(next_ref[b], 1 - slot):
                copy.start()

        for cb in range(wgu_buf.shape[2] // pair):
            wb = wgu_buf[slot, :, cb * pair:(cb + 1) * pair].astype(BF16)
            sep = jnp.dot(wb, perm_ref[...], preferred_element_type=F32).astype(BF16)
            wg_scr[:, cb * LANES:(cb + 1) * LANES] = sep[:, :LANES]
            wl_scr[:, cb * LANES:(cb + 1) * LANES] = sep[:, LANES:]
        wd_scr[...] = wd_buf[slot].astype(BF16)

    @pl.when(b < nused_ref[0])
    def _():
        x = jnp.concatenate(
            [xs_ref[pl.ds(c, MOE_BLOCK, stride=SUBLANES), :] for c in range(SUBLANES)],
            axis=1).astype(BF16)
        gate = jnp.dot(x, wg_scr[...], preferred_element_type=F32) + bg_ref[0]
        lin = jnp.dot(x, wl_scr[...], preferred_element_type=F32) + bl_ref[0]
        gate = jnp.minimum(gate, SWIGLU_LIMIT)
        lin = jnp.clip(lin, -SWIGLU_LIMIT, SWIGLU_LIMIT)
        act = gate * jax.nn.sigmoid(SWIGLU_ALPHA * gate) * (lin + 1.0)
        y = jnp.dot(act.astype(BF16), wd_scr[...], preferred_element_type=F32) + bd_ref[0]
        for c in range(SUBLANES):
            y_ref[pl.ds(c, MOE_BLOCK, stride=SUBLANES), :] = y[:, c * LANES:(c + 1) * LANES]

    @pl.when(b >= nused_ref[0])
    def _():
        y_ref[...] = jnp.zeros(y_ref.shape, F32)


def _experts(xs, block_expert, n_used, block_slot, block_next, w_gate_up, b_gate, b_lin, w_down,
             b_down, n_blocks):
    n_exp, d, d_ff2 = w_gate_up.shape
    d_ff = d_ff2 // 2
    rows = MOE_BLOCK * SUBLANES
    blk = lambda b, be, nu, *_: (jnp.minimum(b, nu[0] - 1), 0)
    wsel = lambda b, be, *_: (be[b], 0, 0)
    r = jnp.arange(2 * LANES)[:, None]
    c = jnp.arange(2 * LANES)[None, :]
    perm = (r == jnp.where(c < LANES, 2 * c, 2 * (c - LANES) + 1)).astype(BF16)
    return pl.pallas_call(
        _expert_kernel,
        grid_spec=pltpu.PrefetchScalarGridSpec(
            num_scalar_prefetch=4,
            grid=(n_blocks,),
            in_specs=[
                pl.BlockSpec((rows, LANES), blk),
                pl.BlockSpec(memory_space=pl.ANY),
                pl.BlockSpec((1, 1, d_ff), wsel), pl.BlockSpec((1, 1, d_ff), wsel),
                pl.BlockSpec(memory_space=pl.ANY), pl.BlockSpec((1, 1, d), wsel),
                pl.BlockSpec((2 * LANES, 2 * LANES), lambda b, *_: (0, 0)),
            ],
            out_specs=pl.BlockSpec((rows, LANES), lambda b, *_: (b, 0)),
            scratch_shapes=[
                pltpu.VMEM((2, d, d_ff2), F32), pltpu.VMEM((2, d_ff, d), F32),
                pltpu.VMEM((d, d_ff), BF16), pltpu.VMEM((d, d_ff), BF16), pltpu.VMEM((d_ff, d), BF16),
                pltpu.SemaphoreType.DMA((2, 2)),
            ],
        ),
        out_shape=jax.ShapeDtypeStruct((n_blocks * rows, LANES), F32),
        compiler_params=pltpu.CompilerParams(
            dimension_semantics=("arbitrary",), vmem_limit_bytes=VMEM_LIMIT),
        name="experts",
    )(block_expert, n_used, block_slot, block_next, xs, w_gate_up, b_gate, b_lin, w_down, b_down,
      perm)


def _combine_kernel(dest_ref, gate_ref, h_ref, nw_ref, eye_ref, ypad_ref, out_ref, *scratch, n_steps):
    bufs = scratch[:COMBINE_BUFFERS]
    sem = scratch[COMBINE_BUFFERS]
    s = pl.program_id(0)
    rows = BLOCK * SUBLANES
    n_groups = BLOCK // SUBLANES

    def step(new, old):
        def issue_group(g):
            for j in range(SUBLANES):
                t = g * SUBLANES + j
                for k in range(TOP_K):
                    src = pl.multiple_of(dest_ref[k, t] * SUBLANES, SUBLANES)
                    dst = pl.multiple_of(k * rows + t * SUBLANES, SUBLANES)
                    pltpu.make_async_copy(ypad_ref.at[pl.ds(src, SUBLANES)],
                                          bufs[new].at[pl.ds(dst, SUBLANES)],
                                          sem.at[new]).start(priority=k % 2)

        def compute():
            gates_t = _nt_dot(eye_ref[...], gate_ref[...], precision=HIGHEST)
            acc = h_ref[...]
            for k in range(TOP_K):
                yk = jnp.concatenate(
                    [bufs[old][pl.ds(k * rows + c, BLOCK, stride=SUBLANES), :]
                     for c in range(SUBLANES)], axis=1)
                acc = acc + gates_t[:, k:k + 1] * yk
            scale = lax.rsqrt(jnp.mean(acc * acc, axis=-1, keepdims=True) + EPS)
            out_ref[...] = acc * scale * nw_ref[...]

        def wait_old():
            pltpu.make_async_copy(ypad_ref.at[pl.ds(0, TOP_K * rows)], bufs[old], sem.at[old]).wait()

        @pl.when(s < COMBINE_LAG)
        def _():
            lax.fori_loop(0, n_groups, lambda g, carry: (issue_group(g), carry)[1], 0)

        @pl.when(jnp.logical_and(s >= COMBINE_LAG, s < n_steps))
        def _():
            wait_old()
            for g in range(n_groups):
                issue_group(g)
            compute()

        @pl.when(s >= n_steps)
        def _():
            wait_old()
            compute()

    for new in range(COMBINE_BUFFERS):
        pl.when(s % COMBINE_BUFFERS == new)(
            functools.partial(step, new, (new + 1) % COMBINE_BUFFERS))


def _combine(ypad, dest, gates, h1, final_norm_w, bsz, nc):
    m, d = h1.shape
    per = nc - 1
    n_steps = bsz * per
    eye = jnp.eye(BLOCK, dtype=F32)

    def tok_blk(s):
        s = jnp.clip(s, 0, n_steps - 1)
        return (s // per) * nc + 1 + s % per

    return pl.pallas_call(
        functools.partial(_combine_kernel, n_steps=n_steps),
        grid=(n_steps + COMBINE_LAG,),
        in_specs=[
            pl.BlockSpec((SUBLANES, BLOCK), lambda s: (0, tok_blk(s)), memory_space=pltpu.SMEM),
            pl.BlockSpec((SUBLANES, BLOCK), lambda s: (0, tok_blk(s - COMBINE_LAG))),
            pl.BlockSpec((BLOCK, d), lambda s: (tok_blk(s - COMBINE_LAG), 0)),
            pl.BlockSpec((1, d), lambda s: (0, 0)),
            pl.BlockSpec((BLOCK, BLOCK), lambda s: (0, 0)),
            pl.BlockSpec(memory_space=pl.ANY),
        ],
        out_specs=pl.BlockSpec((BLOCK, d), lambda s: (jnp.clip(s - COMBINE_LAG, 0, n_steps - 1), 0)),
        out_shape=jax.ShapeDtypeStruct((n_steps * BLOCK, d), F32),
        scratch_shapes=(
            [pltpu.VMEM((TOP_K * BLOCK * SUBLANES, LANES), F32) for _ in range(COMBINE_BUFFERS)]
            + [pltpu.SemaphoreType.DMA((COMBINE_BUFFERS,))]),
        compiler_params=pltpu.CompilerParams(
            dimension_semantics=("arbitrary",), vmem_limit_bytes=VMEM_LIMIT),
        name="combine",
    )(dest, gates, h1, final_norm_w[None, :], eye, ypad)


def _layer(h, bsz, nc, mix_norm_w, w_in, conv_w, conv_b, dt_bias, a_log, d_skip, ssd_norm_w,
           w_ssd_out, w_att_out, w_out, ffn_norm_w, w_router, b_router, w_gate_up, b_gate_up,
           w_down, b_down):
    m, d = h.shape
    seq = nc * BLOCK
    n_heads = a_log.shape[0]
    d_inner = n_heads * SSD_HEADDIM
    conv_dim = d_inner + 2 * SSD_GROUPS * SSD_STATE
    att_dim = w_att_out.shape[0]
    n_exp = w_router.shape[1]

    o_xbc = d_inner
    o_dt = o_xbc + conv_dim
    o_q = o_dt + n_heads
    w_main = jnp.concatenate([w_in[:, :o_dt], w_in[:, o_q:]], axis=1).astype(BF16)
    w_dt = jnp.pad(w_in[:, o_dt:o_q], ((0, 0), (0, LANES - n_heads))).astype(BF16)
    q_col = o_dt
    k_col = q_col + att_dim
    v_col = k_col + att_dim
    gs_col = v_col + att_dim
    ga_col = gs_col + d

    proj, dt_raw = _inproj(h, mix_norm_w[None, :], w_main, w_dt)
    yssd = _ssd(proj, dt_raw, conv_w, conv_b, dt_bias, a_log, d_skip, ssd_norm_w, bsz, nc, d_inner)
    att = _attention(proj, bsz, nc, q_col, k_col, v_col, att_dim)
    h1 = _merge(yssd, att, proj, h, w_ssd_out.astype(BF16), w_att_out.astype(BF16),
                w_out.astype(BF16), bsz, seq, gs_col, ga_col)

    ut, idx, gates, rank, cnt = _router(h1, ffn_norm_w[None, :], w_router, b_router)

    counts = cnt[:, 0].astype(jnp.int32)
    padded = (counts + MOE_BLOCK - 1) // MOE_BLOCK * MOE_BLOCK
    padded_ends = jnp.cumsum(padded)
    padded_starts = padded_ends - padded
    n_blocks = (m * TOP_K + MOE_BLOCK - 1) // MOE_BLOCK + n_exp
    dest = rank + jnp.sum(
        jnp.where(idx[None] == jnp.arange(n_exp, dtype=jnp.int32)[:, None, None],
                  padded_starts[:, None, None], 0), axis=0)
    block_first = jnp.arange(n_blocks, dtype=jnp.int32) * MOE_BLOCK
    block_expert = jnp.minimum(
        jnp.sum((padded_ends[None, :] <= block_first[:, None]).astype(jnp.int32), axis=1), n_exp - 1)
    n_used = (padded_ends[-1:] // MOE_BLOCK).astype(jnp.int32)
    zstart = (padded_starts + counts).astype(jnp.int32)

    zpad = (padded - counts).astype(jnp.int32)
    xs = _dispatch(ut, dest, zstart, zpad, n_used, n_blocks * MOE_BLOCK)
    b_gate = b_gate_up[:, None, 0::2]
    b_lin = b_gate_up[:, None, 1::2]
    experts = jnp.arange(n_exp, dtype=jnp.int32)
    nonempty = padded > 0
    order = jnp.cumsum(nonempty.astype(jnp.int32)) - 1
    later = jnp.where(nonempty[None, :] & (experts[None, :] > experts[:, None]), experts[None, :], n_exp)
    next_nonempty = jnp.min(later, axis=1)
    next_nonempty = jnp.where(next_nonempty >= n_exp, -1, next_nonempty)
    of_block = block_expert[:, None] == experts[None, :]
    block_slot = jnp.sum(jnp.where(of_block, order[None, :] % 2, 0), axis=1).astype(jnp.int32)
    block_next = jnp.sum(jnp.where(of_block, next_nonempty[None, :], 0), axis=1).astype(jnp.int32)
    ypad = _experts(xs, block_expert, n_used, block_slot, block_next, w_gate_up, b_gate, b_lin, w_down,
                    b_down[:, None, :], n_blocks)
    return h1, ypad, dest, gates


def kernel(x, meta_tokens, mix_norm_w, w_in, conv_w, conv_b, dt_bias, a_log, d_skip, ssd_norm_w,
           w_ssd_out, w_att_out, w_out, ffn_norm_w, w_router, b_router, w_gate_up, b_gate_up,
           w_down, b_down, final_norm_w):
    bsz, seq_x, d = x.shape
    depth = mix_norm_w.shape[0]
    assert depth == 1 and seq_x % BLOCK == 0 and (seq_x + N_LEAD) % ROW_TILE == 0
    nc = (seq_x + N_LEAD) // BLOCK
    lead = jnp.concatenate([jnp.zeros((N_PAD, d), x.dtype), meta_tokens.astype(x.dtype)], axis=0)
    h = jnp.concatenate([jnp.broadcast_to(lead[None], (bsz, N_LEAD, d)), x], axis=1)
    h = h.reshape(bsz * nc * BLOCK, d)
    layer = 0
    h1, ypad, dest, gates = _layer(
        h, bsz, nc, mix_norm_w[layer], w_in[layer], conv_w[layer], conv_b[layer], dt_bias[layer],
        a_log[layer], d_skip[layer], ssd_norm_w[layer], w_ssd_out[layer], w_att_out[layer],
        w_out[layer], ffn_norm_w[layer], w_router[layer], b_router[layer], w_gate_up[layer],
        b_gate_up[layer], w_down[layer], b_down[layer])
    out = _combine(ypad, dest, gates, h1, final_norm_w, bsz, nc)
    return out.reshape(bsz, seq_x, d)
```

```python
import functools

import jax
import jax.numpy as jnp
from jax import lax
from jax.experimental import pallas as pl
from jax.experimental.pallas import tpu as pltpu

F32 = jnp.float32
BF16 = jnp.bfloat16
HIGHEST = lax.Precision.HIGHEST

N_META = 16
BLOCK = 128
N_LEAD = BLOCK
N_PAD = N_LEAD - N_META
EPS = 1e-5
SSD_HEADDIM = 64
SSD_GROUPS = 4
SSD_STATE = 128
CONV_K = 4
ATT_HEADDIM = 64
TOP_K = 4
SWIGLU_LIMIT = 7.0
SWIGLU_ALPHA = 1.702
MOE_BLOCK = 512

LANES = 128
SUBLANES = 8
ROW_TILE = 640
INPROJ_ROWS = 1664
COMBINE_LAG = 2
COMBINE_BUFFERS = COMBINE_LAG + 1
EXP_UNDERFLOW = -88.0
MASKED_SCORE = -1e30
VMEM_LIMIT = 56 * 1024 * 1024


def _nt_dot(a, b, precision=None):
    return lax.dot_general(a, b, (((1,), (1,)), ((), ())),
                           preferred_element_type=F32, precision=precision)


def _tn_dot(a, b):
    return lax.dot_general(a, b, (((0,), (0,)), ((), ())), preferred_element_type=F32)


def _silu(x):
    half = 0.5 * x
    return half + half * jnp.tanh(half)


def _inproj_kernel(x_ref, lead_ref, nw_ref, w_ref, wdt_ref, out_ref, dt_ref, h_ref, u_scr,
                   *, tiles_per_seq):
    tm = h_ref.shape[0]

    @pl.when(pl.program_id(1) == 0)
    def _():
        @pl.when(pl.program_id(0) % tiles_per_seq == 0)
        def _():
            h_ref[0:N_LEAD, :] = lead_ref[...]
            if tm > N_LEAD:
                h_ref[N_LEAD:, :] = x_ref[0:tm - N_LEAD, :]

        @pl.when(pl.program_id(0) % tiles_per_seq != 0)
        def _():
            h_ref[...] = x_ref[...]

        x = h_ref[...]
        u = x * lax.rsqrt(jnp.mean(x * x, axis=-1, keepdims=True) + EPS) * nw_ref[...]
        ub = u.astype(BF16)
        u_scr[...] = ub
        dt_ref[...] = jnp.dot(ub, wdt_ref[...], preferred_element_type=F32)

    out_ref[...] = jnp.dot(u_scr[...], w_ref[...], preferred_element_type=F32).astype(out_ref.dtype)


def _largest_row_tile(m, cap):
    return max(t for t in range(BLOCK, cap + 1, BLOCK) if m % t == 0)


def _inproj(x, lead, norm_w, w_main, w_dt, tn=1024):
    bsz, seq_x, d = x.shape
    seq = seq_x + N_LEAD
    m = bsz * seq
    n = w_main.shape[1]
    tm = _largest_row_tile(seq, min(INPROJ_ROWS, seq_x))
    per = seq // tm

    def x_start(i, j):
        row = (i // per) * seq_x + jnp.maximum((i % per) * tm - N_LEAD, 0)
        return (pl.multiple_of(row, BLOCK), 0)

    return pl.pallas_call(
        functools.partial(_inproj_kernel, tiles_per_seq=per),
        grid=(m // tm, n // tn),
        in_specs=[
            pl.BlockSpec((pl.Element(tm), pl.Element(d)), x_start),
            pl.BlockSpec((N_LEAD, d), lambda i, j: (0, 0)),
            pl.BlockSpec((1, d), lambda i, j: (0, 0)),
            pl.BlockSpec((d, tn), lambda i, j: (0, j)),
            pl.BlockSpec((d, LANES), lambda i, j: (0, 0)),
        ],
        out_specs=[
            pl.BlockSpec((tm, tn), lambda i, j: (i, j)),
            pl.BlockSpec((tm, LANES), lambda i, j: (i, 0)),
            pl.BlockSpec((tm, d), lambda i, j: (i, 0)),
        ],
        out_shape=[jax.ShapeDtypeStruct((m, n), BF16), jax.ShapeDtypeStruct((m, LANES), F32),
                   jax.ShapeDtypeStruct((m, d), F32)],
        scratch_shapes=[pltpu.VMEM((tm, d), BF16)],
        compiler_params=pltpu.CompilerParams(
            dimension_semantics=("parallel", "arbitrary"), vmem_limit_bytes=VMEM_LIMIT),
        name="inproj",
    )(x.reshape(bsz * seq_x, d), lead, norm_w, w_main, w_dt)


def _ssd_kernel(z_ref, xs_ref, bc_ref, dt_ref, cwx_ref, cbx_ref, cwb_ref, cbb_ref,
                dtb_ref, alog_ref, dskip_ref, nw_ref, expand_ref, tril_ref,
                out_ref, tailx_scr, tailb_scr, state_scr, *, n_heads):
    c = pl.program_id(1)
    d_inner = xs_ref.shape[1]
    gw = d_inner // SSD_GROUPS
    hpg = n_heads // SSD_GROUPS

    @pl.when(c == 0)
    def _():
        tailx_scr[...] = jnp.zeros(tailx_scr.shape, F32)
        tailb_scr[...] = jnp.zeros(tailb_scr.shape, F32)
        state_scr[...] = jnp.zeros(state_scr.shape, F32)

    row = lax.broadcasted_iota(jnp.int32, (BLOCK, 1), 0)
    vmask = (c * BLOCK + row >= N_PAD).astype(F32)
    first_row = lax.broadcasted_iota(jnp.int32, (SUBLANES, 1), 0) == 0

    def conv(x_ref, tail_scr, w_ref, b_ref):
        x = x_ref[...].astype(F32)
        tail = tail_scr[...]
        tail_scr[...] = x[BLOCK - SUBLANES:, :]
        groups = [x[r * SUBLANES:(r + 1) * SUBLANES, :] for r in range(BLOCK // SUBLANES)]
        accs = [w_ref[0:1, :] * g for g in groups]
        tacc = w_ref[0:1, :] * tail
        for i in range(1, CONV_K):
            trot = pltpu.roll(tacc, 1, axis=0)
            rots = [pltpu.roll(a, 1, axis=0) for a in accs]
            befores = [trot] + rots[:-1]
            accs = [w_ref[i:i + 1, :] * g + jnp.where(first_row, before, rot)
                    for g, before, rot in zip(groups, befores, rots)]
            tacc = w_ref[i:i + 1, :] * tail + trot
        return _silu(jnp.concatenate(accs, axis=0) + b_ref[...]) * vmask

    xs = conv(xs_ref, tailx_scr, cwx_ref, cbx_ref)
    bc = conv(bc_ref, tailb_scr, cwb_ref, cbb_ref)
    gn = SSD_GROUPS * SSD_STATE
    b_all = bc[:, :gn].astype(BF16)
    c_all = bc[:, gn:].astype(BF16)

    dt = jax.nn.softplus(dt_ref[:, :n_heads] + dtb_ref[...]) * vmask
    a = -jnp.exp(alog_ref[...]) * dt
    a_cs = jnp.dot(tril_ref[...], a, preferred_element_type=F32, precision=HIGHEST)
    a_cs_t = a_cs.T
    a_last = a_cs[BLOCK - 1:BLOCK, :]
    decay_states = jnp.exp(a_last - a_cs)
    decay_out = jnp.exp(a_cs)
    chunk_decay = jnp.broadcast_to(jnp.exp(a_last), (SUBLANES, n_heads))
    stacked = jnp.concatenate([dt, decay_states, decay_out, chunk_decay], axis=0)
    st_hi = stacked.astype(BF16)
    st_lo = (stacked - st_hi.astype(F32)).astype(BF16)
    expanded = (jnp.dot(st_hi, expand_ref[...], preferred_element_type=F32)
                + jnp.dot(st_lo, expand_ref[...], preferred_element_type=F32))
    dt_x = expanded[0:BLOCK]
    ds_x = expanded[BLOCK:2 * BLOCK]
    do_x = expanded[2 * BLOCK:3 * BLOCK]
    cd_x = expanded[3 * BLOCK:3 * BLOCK + 1]

    x_dt = xs * dt_x
    x_dt_b = x_dt.astype(BF16)
    x_ds_b = (x_dt * ds_x).astype(BF16)

    li = lax.broadcasted_iota(jnp.int32, (BLOCK, BLOCK), 0)
    si = lax.broadcasted_iota(jnp.int32, (BLOCK, BLOCK), 1)
    causal = si <= li
    first_half = lax.broadcasted_iota(jnp.int32, (BLOCK, 2 * SSD_HEADDIM), 1) < SSD_HEADDIM

    y_groups = []
    for g in range(SSD_GROUPS):
        bg = b_all[:, g * SSD_STATE:(g + 1) * SSD_STATE]
        cg = c_all[:, g * SSD_STATE:(g + 1) * SSD_STATE]
        cols = slice(g * gw, (g + 1) * gw)
        cb = _nt_dot(cg, bg)
        prev = state_scr[g]
        y_off = jnp.dot(cg, prev.astype(BF16), preferred_element_type=F32) * do_x[:, cols]
        new_state = _tn_dot(bg, x_ds_b[:, cols])
        state_scr[g] = cd_x[:, cols] * prev + new_state
        pairs = []
        for jp in range(hpg // 2):
            h0 = g * hpg + 2 * jp
            pc = slice(g * gw + jp * 2 * SSD_HEADDIM, g * gw + (jp + 1) * 2 * SSD_HEADDIM)
            xp = x_dt_b[:, pc]
            ys = []
            for h in (h0, h0 + 1):
                seg = a_cs[:, h:h + 1] - a_cs_t[h:h + 1, :]
                m = (cb * jnp.exp(jnp.where(causal, seg, -jnp.inf))).astype(BF16)
                ys.append(jnp.dot(m, xp, preferred_element_type=F32))
            pairs.append(jnp.where(first_half, ys[0], ys[1]))
        y_groups.append(jnp.concatenate(pairs, axis=1) + y_off)
    y = jnp.concatenate(y_groups, axis=1) + dskip_ref[...] * xs

    gated = y * _silu(z_ref[...].astype(F32))
    outs = []
    for g in range(SSD_GROUPS):
        gg = gated[:, g * gw:(g + 1) * gw]
        outs.append(gg * lax.rsqrt(jnp.mean(gg * gg, axis=-1, keepdims=True) + EPS))
    out_ref[...] = (jnp.concatenate(outs, axis=1) * nw_ref[...]).astype(out_ref.dtype)


def _ssd(proj, dt_raw, conv_w, conv_b, dt_bias, a_log, d_skip, norm_w, bsz, nc, d_inner):
    m = proj.shape[0]
    n_heads = a_log.shape[0]
    bc_w = 2 * SSD_GROUPS * SSD_STATE
    gw = d_inner // SSD_GROUPS
    cwx, cwb = conv_w[:, :d_inner], conv_w[:, d_inner:]
    cbx, cbb = conv_b[None, :d_inner], conv_b[None, d_inner:]
    expand = jnp.repeat(jnp.eye(n_heads, dtype=BF16), SSD_HEADDIM, axis=1)
    tril = jnp.tril(jnp.ones((BLOCK, BLOCK), F32))
    dskip_x = jnp.repeat(d_skip.astype(F32), SSD_HEADDIM)[None, :]
    full = lambda shape: pl.BlockSpec(shape, lambda b, c: (0,) * len(shape))
    row_blk = lambda b, c: b * nc + c
    return pl.pallas_call(
        functools.partial(_ssd_kernel, n_heads=n_heads),
        grid=(bsz, nc),
        in_specs=[
            pl.BlockSpec((BLOCK, d_inner), lambda b, c: (row_blk(b, c), 0)),
            pl.BlockSpec((BLOCK, d_inner), lambda b, c: (row_blk(b, c), 1)),
            pl.BlockSpec((BLOCK, bc_w), lambda b, c: (row_blk(b, c), 2 * d_inner // bc_w)),
            pl.BlockSpec((BLOCK, LANES), lambda b, c: (row_blk(b, c), 0)),
            full((CONV_K, d_inner)), full((1, d_inner)), full((CONV_K, bc_w)), full((1, bc_w)),
            full((1, n_heads)), full((1, n_heads)), full((1, d_inner)), full((1, d_inner)),
            full((n_heads, d_inner)), full((BLOCK, BLOCK)),
        ],
        out_specs=pl.BlockSpec((BLOCK, d_inner), lambda b, c: (row_blk(b, c), 0)),
        out_shape=jax.ShapeDtypeStruct((m, d_inner), BF16),
        scratch_shapes=[
            pltpu.VMEM((SUBLANES, d_inner), F32),
            pltpu.VMEM((SUBLANES, bc_w), F32),
            pltpu.VMEM((SSD_GROUPS, SSD_STATE, gw), F32),
        ],
        compiler_params=pltpu.CompilerParams(
            dimension_semantics=("parallel", "arbitrary"), vmem_limit_bytes=VMEM_LIMIT),
        name="ssd",
    )(proj, proj, proj, dt_raw, cwx, cbx, cwb, cbb, dt_bias[None, :].astype(F32),
      a_log[None, :].astype(F32), dskip_x, norm_w[None, :].astype(F32), expand, tril)


def _attn_kernel(q_ref, k_ref, v_ref, upper_ref, out_ref, *, n_pairs):
    i = pl.program_id(2)
    pw = 2 * ATT_HEADDIM
    lane = lax.broadcasted_iota(jnp.int32, (BLOCK, pw), 1)
    first = lane < ATT_HEADDIM
    row2 = lax.broadcasted_iota(jnp.int32, (2 * BLOCK, BLOCK), 0)
    qpos = i * BLOCK + jnp.where(row2 < BLOCK, row2, row2 - BLOCK)
    kin = lax.broadcasted_iota(jnp.int32, (2 * BLOCK, BLOCK), 1)
    upper = upper_ref[...]
    scale = ATT_HEADDIM ** -0.5
    q2 = []
    for p in range(n_pairs):
        q = q_ref[:, p * pw:(p + 1) * pw] * jnp.asarray(scale, BF16)
        zero = jnp.zeros_like(q)
        q2.append(jnp.concatenate([jnp.where(first, q, zero), jnp.where(first, zero, q)], axis=0))

    def process(kbs, runs, accs):
        starts = [pl.multiple_of(jnp.maximum(kb, 0) * BLOCK, BLOCK) for kb in kbs]
        alloweds = []
        for kb in kbs:
            kpos = kb * BLOCK + kin
            alloweds.append(jnp.logical_and(kpos < qpos, kpos >= N_PAD))
        zs = [[_nt_dot(q2[p], k_ref[pl.ds(start, BLOCK), p * pw:(p + 1) * pw])
               for p in range(n_pairs)] for start in starts]
        log_betas, laters, sums = [], [], []
        for j in range(len(kbs)):
            log_betas.append([])
            laters.append([])
            sums.append([])
            for p in range(n_pairs):
                z = jnp.where(alloweds[j], zs[j][p], MASKED_SCORE)
                l1p = jnp.log(1.0 + jnp.exp(-jnp.abs(z)))
                log_beta = jnp.minimum(z, 0.0) - l1p
                log_stay = log_beta - z
                hi = log_stay.astype(BF16)
                lo = (log_stay - hi.astype(F32)).astype(BF16)
                log_betas[j].append(log_beta)
                laters[j].append(jnp.dot(hi, upper, preferred_element_type=F32)
                                 + jnp.dot(lo, upper, preferred_element_type=F32))
                sums[j].append(jnp.sum(log_stay, axis=1, keepdims=True))
        runs, accs = list(runs), list(accs)
        for j, start in enumerate(starts):
            for p in range(n_pairs):
                vblk = v_ref[pl.ds(start, BLOCK), p * pw:(p + 1) * pw]
                w = jnp.exp(log_betas[j][p] + laters[j][p] + runs[p]).astype(BF16)
                vzero = jnp.zeros_like(vblk)
                accs[p] = (accs[p]
                           + jnp.dot(w[:BLOCK], jnp.where(first, vblk, vzero),
                                     preferred_element_type=F32)
                           + jnp.dot(w[BLOCK:], jnp.where(first, vzero, vblk),
                                     preferred_element_type=F32))
                runs[p] = runs[p] + sums[j][p]
        top = jnp.max(runs[0])
        for r in runs[1:]:
            top = jnp.maximum(top, jnp.max(r))
        return top, runs, accs

    def cond(carry):
        kb, top = carry[0], carry[1]
        return jnp.logical_and(kb >= 0, top > EXP_UNDERFLOW)

    def body(carry):
        kb = carry[0]
        top, runs, accs = process([kb], carry[2:2 + n_pairs], carry[2 + n_pairs:])
        return (kb - 1, top, *runs, *accs)

    zero_runs = [jnp.zeros((2 * BLOCK, 1), F32) for _ in range(n_pairs)]
    zero_accs = [jnp.zeros((BLOCK, pw), F32) for _ in range(n_pairs)]
    top, runs, accs = process([i, i - 1], zero_runs, zero_accs)
    final = lax.while_loop(cond, body, (i - 2, top, *runs, *accs))
    for p in range(n_pairs):
        out_ref[:, p * pw:(p + 1) * pw] = final[2 + n_pairs + p].astype(out_ref.dtype)


def _attention(proj, bsz, nc, q_col, k_col, v_col, att_dim, pairs_per_step=4):
    m = proj.shape[0]
    seq = nc * BLOCK
    sw = pairs_per_step * 2 * ATT_HEADDIM
    upper = jnp.triu(jnp.ones((BLOCK, BLOCK), F32), 1).T.astype(BF16)
    return pl.pallas_call(
        functools.partial(_attn_kernel, n_pairs=pairs_per_step),
        grid=(bsz, att_dim // sw, nc),
        in_specs=[
            pl.BlockSpec((BLOCK, sw), lambda b, p, i: (b * nc + i, q_col // sw + p)),
            pl.BlockSpec((seq, sw), lambda b, p, i: (b, k_col // sw + p)),
            pl.BlockSpec((seq, sw), lambda b, p, i: (b, v_col // sw + p)),
            pl.BlockSpec((BLOCK, BLOCK), lambda b, p, i: (0, 0)),
        ],
        out_specs=pl.BlockSpec((BLOCK, sw), lambda b, p, i: (b * nc + i, p)),
        out_shape=jax.ShapeDtypeStruct((m, att_dim), BF16),
        compiler_params=pltpu.CompilerParams(
            dimension_semantics=("parallel", "parallel", "arbitrary"),
            vmem_limit_bytes=VMEM_LIMIT),
        name="attention",
    )(proj, proj, proj, upper)


def _merge_kernel(yssd_ref, att_ref, gs_ref, ga_ref, h_ref, wso_ref, wao_ref, wo_ref, out_ref):
    t = pl.program_id(1)
    y_ssd = jnp.dot(yssd_ref[...], wso_ref[...], preferred_element_type=F32)
    y_att = jnp.dot(att_ref[...], wao_ref[...], preferred_element_type=F32)
    merged = (jax.nn.sigmoid(gs_ref[...].astype(F32)) * y_ssd
              + jax.nn.sigmoid(ga_ref[...].astype(F32)) * y_att)
    mixed = jnp.dot(merged.astype(BF16), wo_ref[...], preferred_element_type=F32)
    row = lax.broadcasted_iota(jnp.int32, (ROW_TILE, 1), 0)
    valid = (t * ROW_TILE + row >= N_PAD).astype(F32)
    out_ref[...] = h_ref[...] + mixed * valid


def _merge(yssd, att, proj, h, w_ssd_out, w_att_out, w_out, bsz, seq, gs_col, ga_col):
    m, d = h.shape
    d_inner = yssd.shape[1]
    att_dim = att.shape[1]
    per = seq // ROW_TILE
    rb = lambda b, t: b * per + t
    full = lambda shape: pl.BlockSpec(shape, lambda b, t: (0,) * len(shape))
    return pl.pallas_call(
        _merge_kernel,
        grid=(bsz, per),
        in_specs=[
            pl.BlockSpec((ROW_TILE, d_inner), lambda b, t: (rb(b, t), 0)),
            pl.BlockSpec((ROW_TILE, att_dim), lambda b, t: (rb(b, t), 0)),
            pl.BlockSpec((ROW_TILE, d), lambda b, t: (rb(b, t), gs_col // d)),
            pl.BlockSpec((ROW_TILE, d), lambda b, t: (rb(b, t), ga_col // d)),
            pl.BlockSpec((ROW_TILE, d), lambda b, t: (rb(b, t), 0)),
            full((d_inner, d)), full((att_dim, d)), full((d, d)),
        ],
        out_specs=pl.BlockSpec((ROW_TILE, d), lambda b, t: (rb(b, t), 0)),
        out_shape=jax.ShapeDtypeStruct((m, d), F32),
        compiler_params=pltpu.CompilerParams(
            dimension_semantics=("parallel", "parallel"), vmem_limit_bytes=VMEM_LIMIT,
            allow_input_fusion=[False, False, False, False, True, True, True, True]),
        name="merge",
    )(yssd, att, proj, proj, h, w_ssd_out, w_att_out, w_out)


def _router_kernel(h_ref, nw_ref, wrh_ref, wrl_ref, br_ref, before_ref,
                   ut_ref, idx_ref, gate_ref, rank_ref, cnt_ref, base_scr):
    step = pl.program_id(0)
    n_exp = br_ref.shape[0]
    tm = h_ref.shape[0]

    @pl.when(step == 0)
    def _():
        base_scr[...] = jnp.zeros(base_scr.shape, F32)

    x = h_ref[...]
    u = x * lax.rsqrt(jnp.mean(x * x, axis=-1, keepdims=True) + EPS) * nw_ref[...]
    for c in range(SUBLANES):
        ut_ref[pl.ds(c, tm, stride=SUBLANES), :] = u[:, c * LANES:(c + 1) * LANES]

    u_hi = u.astype(BF16)
    u_lo = (u - u_hi.astype(F32)).astype(BF16)
    logits_tm = (jnp.dot(u_hi, wrh_ref[...], preferred_element_type=F32)
                 + jnp.dot(u_lo, wrh_ref[...], preferred_element_type=F32)
                 + jnp.dot(u_hi, wrl_ref[...], preferred_element_type=F32))
    logits = logits_tm.T[:n_exp] + br_ref[...]
    eidx = lax.broadcasted_iota(jnp.int32, (n_exp, tm), 0)
    work = logits
    tops, idxs, onehots = [], [], []
    for _ in range(TOP_K):
        top = jnp.max(work, axis=0, keepdims=True)
        idx = jnp.min(jnp.where(work == top, eidx, n_exp), axis=0, keepdims=True)
        hot = eidx == idx
        work = jnp.where(hot, -jnp.inf, work)
        tops.append(top)
        idxs.append(idx)
        onehots.append(hot.astype(F32))
    exps = [jnp.exp(v - tops[0]) for v in tops]
    denom = exps[0] + exps[1] + exps[2] + exps[3]
    cnt = onehots[0] + onehots[1] + onehots[2] + onehots[3]
    before = jnp.dot(cnt.astype(BF16), before_ref[...], preferred_element_type=F32) + base_scr[...]
    ranks = [jnp.sum(hot * before, axis=0, keepdims=True) for hot in onehots]
    pad_rows = SUBLANES - TOP_K
    idx_ref[...] = jnp.concatenate(idxs + [jnp.zeros((pad_rows, tm), jnp.int32)], axis=0)
    gate_ref[...] = jnp.concatenate([e / denom for e in exps] + [jnp.zeros((pad_rows, tm), F32)], axis=0)
    rank_ref[...] = jnp.concatenate(
        [r.astype(jnp.int32) for r in ranks] + [jnp.zeros((pad_rows, tm), jnp.int32)], axis=0)
    base_scr[...] = base_scr[...] + jnp.sum(cnt, axis=1, keepdims=True)
    cnt_ref[...] = jnp.broadcast_to(base_scr[...], cnt_ref.shape)


def _router(h1, norm_w, w_router, b_router):
    m, d = h1.shape
    n_exp = w_router.shape[1]
    before = jnp.triu(jnp.ones((ROW_TILE, ROW_TILE), F32), 1).astype(BF16)
    wr = jnp.pad(w_router.astype(F32), ((0, 0), (0, LANES - n_exp)))
    wr_hi = wr.astype(BF16)
    wr_lo = (wr - wr_hi.astype(F32)).astype(BF16)
    full = lambda shape: pl.BlockSpec(shape, lambda i: (0,) * len(shape))
    return pl.pallas_call(
        _router_kernel,
        grid=(m // ROW_TILE,),
        in_specs=[
            pl.BlockSpec((ROW_TILE, d), lambda i: (i, 0)),
            full((1, d)), full((d, LANES)), full((d, LANES)), full((n_exp, 1)),
            full((ROW_TILE, ROW_TILE)),
        ],
        out_specs=[
            pl.BlockSpec((ROW_TILE * SUBLANES, LANES), lambda i: (i, 0)),
            pl.BlockSpec((SUBLANES, ROW_TILE), lambda i: (0, i)),
            pl.BlockSpec((SUBLANES, ROW_TILE), lambda i: (0, i)),
            pl.BlockSpec((SUBLANES, ROW_TILE), lambda i: (0, i)),
            full((n_exp, LANES)),
        ],
        out_shape=[
            jax.ShapeDtypeStruct((m * SUBLANES, LANES), F32),
            jax.ShapeDtypeStruct((SUBLANES, m), jnp.int32),
            jax.ShapeDtypeStruct((SUBLANES, m), F32),
            jax.ShapeDtypeStruct((SUBLANES, m), jnp.int32),
            jax.ShapeDtypeStruct((n_exp, LANES), F32),
        ],
        scratch_shapes=[pltpu.VMEM((n_exp, 1), F32)],
        compiler_params=pltpu.CompilerParams(
            dimension_semantics=("arbitrary",), vmem_limit_bytes=VMEM_LIMIT),
        name="router",
    )(h1, norm_w, wr_hi, wr_lo, b_router[:, None], before)


def _dispatch_kernel(zstart_ref, zpad_ref, nused_ref, dest_ref, ut_ref, xs_ref, zero_scr, sem, zsem,
                     *, n_exp):
    step = pl.program_id(0)
    tb = dest_ref.shape[1]
    slot_rows = MOE_BLOCK * SUBLANES
    n_blocks = xs_ref.shape[0] // slot_rows

    @pl.when(step == 0)
    def _():
        zero_scr[...] = jnp.zeros(zero_scr.shape, F32)

        def zero_rows(row_start, n_rows):
            return pltpu.make_async_copy(zero_scr.at[pl.ds(0, n_rows)],
                                         xs_ref.at[pl.ds(row_start, n_rows)], zsem)

        def pad_fill(wait):
            def per_expert(e, carry):
                n_pad = zpad_ref[e]
                pos = zstart_ref[e]
                size = MOE_BLOCK // 2
                while size >= 1:
                    take = n_pad & size

                    @pl.when(take != 0)
                    def _(pos=pos, size=size):
                        copy = zero_rows(pl.multiple_of(pos * SUBLANES, SUBLANES), size * SUBLANES)
                        copy.wait() if wait else copy.start()

                    pos = pos + take
                    size //= 2
                return carry

            lax.fori_loop(0, n_exp, per_expert, 0)

        def tail_fill(wait):
            def per_block(b, carry):
                copy = zero_rows(pl.multiple_of(b * slot_rows, slot_rows), slot_rows)
                copy.wait() if wait else copy.start()
                return carry

            lax.fori_loop(nused_ref[0], n_blocks, per_block, 0)

        pad_fill(False)
        tail_fill(False)
        pad_fill(True)
        tail_fill(True)

    def issue(t, carry):
        src = pl.multiple_of(t * SUBLANES, SUBLANES)
        for k in range(TOP_K):
            dst = pl.multiple_of(dest_ref[k, t] * SUBLANES, SUBLANES)
            pltpu.make_async_copy(ut_ref.at[pl.ds(src, SUBLANES)],
                                  xs_ref.at[pl.ds(dst, SUBLANES)], sem).start(priority=k % 2)
        return carry

    lax.fori_loop(0, tb, issue, 0, unroll=4)
    for _ in range(TOP_K):
        pltpu.make_async_copy(ut_ref, xs_ref.at[pl.ds(0, tb * SUBLANES)], sem).wait()


def _dispatch(ut, dest, zstart, zpad, n_used, n_slots):
    m = dest.shape[1]
    n_exp = zstart.shape[0]
    return pl.pallas_call(
        functools.partial(_dispatch_kernel, n_exp=n_exp),
        grid_spec=pltpu.PrefetchScalarGridSpec(
            num_scalar_prefetch=3,
            grid=(m // ROW_TILE,),
            in_specs=[
                pl.BlockSpec((SUBLANES, ROW_TILE), lambda i, *_: (0, i), memory_space=pltpu.SMEM),
                pl.BlockSpec((ROW_TILE * SUBLANES, LANES), lambda i, *_: (i, 0)),
            ],
            out_specs=pl.BlockSpec(memory_space=pl.ANY),
            scratch_shapes=[
                pltpu.VMEM((MOE_BLOCK * SUBLANES, LANES), F32),
                pltpu.SemaphoreType.DMA(()),
                pltpu.SemaphoreType.DMA(()),
            ],
        ),
        out_shape=jax.ShapeDtypeStruct((n_slots * SUBLANES, LANES), F32),
        compiler_params=pltpu.CompilerParams(
            dimension_semantics=("arbitrary",), vmem_limit_bytes=VMEM_LIMIT),
        name="dispatch",
    )(zstart, zpad, n_used, dest, ut)


def _expert_kernel(be_ref, nused_ref, slot_ref, next_ref, xs_ref, wgu_hbm, bg_ref, bl_ref, wd_hbm,
                   bd_ref, perm_ref, y_ref, wgu_buf, wd_buf, wg_scr, wl_scr, wd_scr, sem):
    b = pl.program_id(0)
    pair = 2 * LANES
    changed = jnp.logical_or(b == 0, be_ref[b] != be_ref[jnp.maximum(b - 1, 0)])

    def fetch(expert, slot):
        return (pltpu.make_async_copy(wgu_hbm.at[expert], wgu_buf.at[slot], sem.at[0, slot]),
                pltpu.make_async_copy(wd_hbm.at[expert], wd_buf.at[slot], sem.at[1, slot]))

    @pl.when(b == 0)
    def _():
        for copy in fetch(be_ref[0], slot_ref[0]):
            copy.start()

    @pl.when(jnp.logical_and(changed, b < nused_ref[0]))
    def _():
        slot = slot_ref[b]
        for copy in fetch(be_ref[b], slot):
            copy.wait()

        @pl.when(next_ref[b] >= 0)
        def _():
            for copy in fetch(next_ref[b], 1 - slot):
                copy.start()

        for cb in range(wgu_buf.shape[2] // pair):
            wb = wgu_buf[slot, :, cb * pair:(cb + 1) * pair].astype(BF16)
            sep = jnp.dot(wb, perm_ref[...], preferred_element_type=F32).astype(BF16)
            wg_scr[:, cb * LANES:(cb + 1) * LANES] = sep[:, :LANES]
            wl_scr[:, cb * LANES:(cb + 1) * LANES] = sep[:, LANES:]
        wd_scr[...] = wd_buf[slot].astype(BF16)

    @pl.when(b < nused_ref[0])
    def _():
        x = jnp.concatenate(
            [xs_ref[pl.ds(c, MOE_BLOCK, stride=SUBLANES), :] for c in range(SUBLANES)],
            axis=1).astype(BF16)
        gate = jnp.dot(x, wg_scr[...], preferred_element_type=F32) + bg_ref[0]
        lin = jnp.dot(x, wl_scr[...], preferred_element_type=F32) + bl_ref[0]
        gate = jnp.minimum(gate, SWIGLU_LIMIT)
        lin = jnp.clip(lin, -SWIGLU_LIMIT, SWIGLU_LIMIT)
        act = gate * jax.nn.sigmoid(SWIGLU_ALPHA * gate) * (lin + 1.0)
        y = jnp.dot(act.astype(BF16), wd_scr[...], preferred_element_type=F32) + bd_ref[0]
        for c in range(SUBLANES):
            y_ref[pl.ds(c, MOE_BLOCK, stride=SUBLANES), :] = y[:, c * LANES:(c + 1) * LANES]

    @pl.when(b >= nused_ref[0])
    def _():
        y_ref[...] = jnp.zeros(y_ref.shape, F32)


def _experts(xs, block_expert, n_used, block_slot, block_next, w_gate_up, b_gate, b_lin, w_down,
             b_down, n_blocks):
    n_exp, d, d_ff2 = w_gate_up.shape
    d_ff = d_ff2 // 2
    rows = MOE_BLOCK * SUBLANES
    blk = lambda b, be, nu, *_: (jnp.minimum(b, nu[0] - 1), 0)
    wsel = lambda b, be, *_: (be[b], 0, 0)
    r = jnp.arange(2 * LANES)[:, None]
    c = jnp.arange(2 * LANES)[None, :]
    perm = (r == jnp.where(c < LANES, 2 * c, 2 * (c - LANES) + 1)).astype(BF16)
    return pl.pallas_call(
        _expert_kernel,
        grid_spec=pltpu.PrefetchScalarGridSpec(
            num_scalar_prefetch=4,
            grid=(n_blocks,),
            in_specs=[
                pl.BlockSpec((rows, LANES), blk),
                pl.BlockSpec(memory_space=pl.ANY),
                pl.BlockSpec((1, 1, d_ff), wsel), pl.BlockSpec((1, 1, d_ff), wsel),
                pl.BlockSpec(memory_space=pl.ANY), pl.BlockSpec((1, 1, d), wsel),
                pl.BlockSpec((2 * LANES, 2 * LANES), lambda b, *_: (0, 0)),
            ],
            out_specs=pl.BlockSpec((rows, LANES), lambda b, *_: (b, 0)),
            scratch_shapes=[
                pltpu.VMEM((2, d, d_ff2), F32), pltpu.VMEM((2, d_ff, d), F32),
                pltpu.VMEM((d, d_ff), BF16), pltpu.VMEM((d, d_ff), BF16), pltpu.VMEM((d_ff, d), BF16),
                pltpu.SemaphoreType.DMA((2, 2)),
            ],
        ),
        out_shape=jax.ShapeDtypeStruct((n_blocks * rows, LANES), F32),
        compiler_params=pltpu.CompilerParams(
            dimension_semantics=("arbitrary",), vmem_limit_bytes=VMEM_LIMIT),
        name="experts",
    )(block_expert, n_used, block_slot, block_next, xs, w_gate_up, b_gate, b_lin, w_down, b_down,
      perm)


def _combine_kernel(dest_ref, gate_ref, h_ref, nw_ref, eye_ref, ypad_ref, out_ref, *scratch, n_steps):
    bufs = scratch[:COMBINE_BUFFERS]
    sem = scratch[COMBINE_BUFFERS]
    s = pl.program_id(0)
    rows = BLOCK * SUBLANES
    n_groups = BLOCK // SUBLANES

    def step(new, old):
        def issue_group(g):
            for j in range(SUBLANES):
                t = g * SUBLANES + j
                for k in range(TOP_K):
                    src = pl.multiple_of(dest_ref[k, t] * SUBLANES, SUBLANES)
                    dst = pl.multiple_of(k * rows + t * SUBLANES, SUBLANES)
                    pltpu.make_async_copy(ypad_ref.at[pl.ds(src, SUBLANES)],
                                          bufs[new].at[pl.ds(dst, SUBLANES)],
                                          sem.at[new]).start(priority=k % 2)

        def compute():
            gates_t = _nt_dot(eye_ref[...], gate_ref[...], precision=HIGHEST)
            acc = h_ref[...]
            for k in range(TOP_K):
                yk = jnp.concatenate(
                    [bufs[old][pl.ds(k * rows + c, BLOCK, stride=SUBLANES), :]
                     for c in range(SUBLANES)], axis=1)
                acc = acc + gates_t[:, k:k + 1] * yk
            scale = lax.rsqrt(jnp.mean(acc * acc, axis=-1, keepdims=True) + EPS)
            out_ref[...] = acc * scale * nw_ref[...]

        def wait_old():
            pltpu.make_async_copy(ypad_ref.at[pl.ds(0, TOP_K * rows)], bufs[old], sem.at[old]).wait()

        @pl.when(s < COMBINE_LAG)
        def _():
            lax.fori_loop(0, n_groups, lambda g, carry: (issue_group(g), carry)[1], 0)

        @pl.when(jnp.logical_and(s >= COMBINE_LAG, s < n_steps))
        def _():
            wait_old()
            for g in range(n_groups):
                issue_group(g)
            compute()

        @pl.when(s >= n_steps)
        def _():
            wait_old()
            compute()

    for new in range(COMBINE_BUFFERS):
        pl.when(s % COMBINE_BUFFERS == new)(
            functools.partial(step, new, (new + 1) % COMBINE_BUFFERS))


def _combine(ypad, dest, gates, h1, final_norm_w, bsz, nc):
    m, d = h1.shape
    per = nc - 1
    n_steps = bsz * per
    eye = jnp.eye(BLOCK, dtype=F32)

    def tok_blk(s):
        s = jnp.clip(s, 0, n_steps - 1)
        return (s // per) * nc + 1 + s % per

    return pl.pallas_call(
        functools.partial(_combine_kernel, n_steps=n_steps),
        grid=(n_steps + COMBINE_LAG,),
        in_specs=[
            pl.BlockSpec((SUBLANES, BLOCK), lambda s: (0, tok_blk(s)), memory_space=pltpu.SMEM),
            pl.BlockSpec((SUBLANES, BLOCK), lambda s: (0, tok_blk(s - COMBINE_LAG))),
            pl.BlockSpec((BLOCK, d), lambda s: (tok_blk(s - COMBINE_LAG), 0)),
            pl.BlockSpec((1, d), lambda s: (0, 0)),
            pl.BlockSpec((BLOCK, BLOCK), lambda s: (0, 0)),
            pl.BlockSpec(memory_space=pl.ANY),
        ],
        out_specs=pl.BlockSpec((BLOCK, d), lambda s: (jnp.clip(s - COMBINE_LAG, 0, n_steps - 1), 0)),
        out_shape=jax.ShapeDtypeStruct((n_steps * BLOCK, d), F32),
        scratch_shapes=(
            [pltpu.VMEM((TOP_K * BLOCK * SUBLANES, LANES), F32) for _ in range(COMBINE_BUFFERS)]
            + [pltpu.SemaphoreType.DMA((COMBINE_BUFFERS,))]),
        compiler_params=pltpu.CompilerParams(
            dimension_semantics=("arbitrary",), vmem_limit_bytes=VMEM_LIMIT),
        name="combine",
    )(dest, gates, h1, final_norm_w[None, :], eye, ypad)


def _layer(x, lead, nc, mix_norm_w, w_in, conv_w, conv_b, dt_bias, a_log, d_skip, ssd_norm_w,
           w_ssd_out, w_att_out, w_out, ffn_norm_w, w_router, b_router, w_gate_up, b_gate_up,
           w_down, b_down):
    bsz, _, d = x.shape
    seq = nc * BLOCK
    m = bsz * seq
    n_heads = a_log.shape[0]
    d_inner = n_heads * SSD_HEADDIM
    conv_dim = d_inner + 2 * SSD_GROUPS * SSD_STATE
    att_dim = w_att_out.shape[0]
    n_exp = w_router.shape[1]

    o_xbc = d_inner
    o_dt = o_xbc + conv_dim
    o_q = o_dt + n_heads
    w_main = jnp.concatenate([w_in[:, :o_dt], w_in[:, o_q:]], axis=1).astype(BF16)
    w_dt = jnp.pad(w_in[:, o_dt:o_q], ((0, 0), (0, LANES - n_heads))).astype(BF16)
    q_col = o_dt
    k_col = q_col + att_dim
    v_col = k_col + att_dim
    gs_col = v_col + att_dim
    ga_col = gs_col + d

    proj, dt_raw, h = _inproj(x, lead, mix_norm_w[None, :], w_main, w_dt)
    yssd = _ssd(proj, dt_raw, conv_w, conv_b, dt_bias, a_log, d_skip, ssd_norm_w, bsz, nc, d_inner)
    att = _attention(proj, bsz, nc, q_col, k_col, v_col, att_dim)
    h1 = _merge(yssd, att, proj, h, w_ssd_out.astype(BF16), w_att_out.astype(BF16),
                w_out.astype(BF16), bsz, seq, gs_col, ga_col)

    ut, idx, gates, rank, cnt = _router(h1, ffn_norm_w[None, :], w_router, b_router)

    counts = cnt[:, 0].astype(jnp.int32)
    padded = (counts + MOE_BLOCK - 1) // MOE_BLOCK * MOE_BLOCK
    padded_ends = jnp.cumsum(padded)
    padded_starts = padded_ends - padded
    n_blocks = (m * TOP_K + MOE_BLOCK - 1) // MOE_BLOCK + n_exp
    dest = rank + jnp.sum(
        jnp.where(idx[None] == jnp.arange(n_exp, dtype=jnp.int32)[:, None, None],
                  padded_starts[:, None, None], 0), axis=0)
    block_first = jnp.arange(n_blocks, dtype=jnp.int32) * MOE_BLOCK
    block_expert = jnp.minimum(
        jnp.sum((padded_ends[None, :] <= block_first[:, None]).astype(jnp.int32), axis=1), n_exp - 1)
    n_used = (padded_ends[-1:] // MOE_BLOCK).astype(jnp.int32)
    zstart = (padded_starts + counts).astype(jnp.int32)

    zpad = (padded - counts).astype(jnp.int32)
    xs = _dispatch(ut, dest, zstart, zpad, n_used, n_blocks * MOE_BLOCK)
    b_gate = b_gate_up[:, None, 0::2]
    b_lin = b_gate_up[:, None, 1::2]
    experts = jnp.arange(n_exp, dtype=jnp.int32)
    nonempty = padded > 0
    order = jnp.cumsum(nonempty.astype(jnp.int32)) - 1
    later = jnp.where(nonempty[None, :] & (experts[None, :] > experts[:, None]), experts[None, :], n_exp)
    next_nonempty = jnp.min(later, axis=1)
    next_nonempty = jnp.where(next_nonempty >= n_exp, -1, next_nonempty)
    of_block = block_expert[:, None] == experts[None, :]
    block_slot = jnp.sum(jnp.where(of_block, order[None, :] % 2, 0), axis=1).astype(jnp.int32)
    block_next = jnp.sum(jnp.where(of_block, next_nonempty[None, :], 0), axis=1).astype(jnp.int32)
    ypad = _experts(xs, block_expert, n_used, block_slot, block_next, w_gate_up, b_gate, b_lin, w_down,
                    b_down[:, None, :], n_blocks)
    return h1, ypad, dest, gates


def kernel(x, meta_tokens, mix_norm_w, w_in, conv_w, conv_b, dt_bias, a_log, d_skip, ssd_norm_w,
           w_ssd_out, w_att_out, w_out, ffn_norm_w, w_router, b_router, w_gate_up, b_gate_up,
           w_down, b_down, final_norm_w):
    bsz, seq_x, d = x.shape
    depth = mix_norm_w.shape[0]
    assert depth == 1 and seq_x % BLOCK == 0 and (seq_x + N_LEAD) % ROW_TILE == 0
    nc = (seq_x + N_LEAD) // BLOCK
    lead = jnp.concatenate([jnp.zeros((N_PAD, d), x.dtype), meta_tokens.astype(x.dtype)], axis=0)
    layer = 0
    h1, ypad, dest, gates = _layer(
        x, lead, nc, mix_norm_w[layer], w_in[layer], conv_w[layer], conv_b[layer], dt_bias[layer],
        a_log[layer], d_skip[layer], ssd_norm_w[layer], w_ssd_out[layer], w_att_out[layer],
        w_out[layer], ffn_norm_w[layer], w_router[layer], b_router[layer], w_gate_up[layer],
        b_gate_up[layer], w_down[layer], b_down[layer])
    out = _combine(ypad, dest, gates, h1, final_norm_w, bsz, nc)
    return out.reshape(bsz, seq_x, d)
```

```python
import functools

import jax
import jax.numpy as jnp
from jax import lax
from jax.experimental import pallas as pl
from jax.experimental.pallas import tpu as pltpu

F32 = jnp.float32
BF16 = jnp.bfloat16
HIGHEST = lax.Precision.HIGHEST

N_META = 16
BLOCK = 128
N_LEAD = BLOCK
N_PAD = N_LEAD - N_META
EPS = 1e-5
SSD_HEADDIM = 64
SSD_GROUPS = 4
SSD_STATE = 128
CONV_K = 4
ATT_HEADDIM = 64
TOP_K = 4
SWIGLU_LIMIT = 7.0
SWIGLU_ALPHA = 1.702
MOE_BLOCK = 512

LANES = 128
SUBLANES = 8
ROW_TILE = 640
INPROJ_ROWS = 1664
COMBINE_LAG = 2
COMBINE_BUFFERS = COMBINE_LAG + 1
EXP_UNDERFLOW = -88.0
MASKED_SCORE = -1e30
VMEM_LIMIT = 56 * 1024 * 1024


def _nt_dot(a, b, precision=None):
    return lax.dot_general(a, b, (((1,), (1,)), ((), ())),
                           preferred_element_type=F32, precision=precision)


def _tn_dot(a, b):
    return lax.dot_general(a, b, (((0,), (0,)), ((), ())), preferred_element_type=F32)


def _silu(x):
    half = 0.5 * x
    return half + half * jnp.tanh(half)


def _inproj_kernel(x_ref, lead_ref, nw_ref, w_ref, wdt_ref, out_ref, dt_ref, h_ref, u_scr,
                   *, tiles_per_seq):
    tm = h_ref.shape[0]

    @pl.when(pl.program_id(1) == 0)
    def _():
        @pl.when(pl.program_id(0) % tiles_per_seq == 0)
        def _():
            h_ref[0:N_LEAD, :] = lead_ref[...]
            if tm > N_LEAD:
                h_ref[N_LEAD:, :] = x_ref[0:tm - N_LEAD, :]

        @pl.when(pl.program_id(0) % tiles_per_seq != 0)
        def _():
            h_ref[...] = x_ref[...]

        x = h_ref[...]
        u = x * lax.rsqrt(jnp.mean(x * x, axis=-1, keepdims=True) + EPS) * nw_ref[...]
        ub = u.astype(BF16)
        u_scr[...] = ub
        dt_ref[...] = jnp.dot(ub, wdt_ref[...], preferred_element_type=F32)

    out_ref[...] = jnp.dot(u_scr[...], w_ref[...], preferred_element_type=F32).astype(out_ref.dtype)


def _largest_row_tile(m, cap):
    return max(t for t in range(BLOCK, cap + 1, BLOCK) if m % t == 0)


def _inproj(x, lead, norm_w, w_main, w_dt, tn=1024):
    bsz, seq_x, d = x.shape
    seq = seq_x + N_LEAD
    m = bsz * seq
    n = w_main.shape[1]
    tm = _largest_row_tile(seq, min(INPROJ_ROWS, seq_x))
    per = seq // tm

    def x_start(i, j):
        row = (i // per) * seq_x + jnp.maximum((i % per) * tm - N_LEAD, 0)
        return (pl.multiple_of(row, BLOCK), 0)

    return pl.pallas_call(
        functools.partial(_inproj_kernel, tiles_per_seq=per),
        grid=(m // tm, n // tn),
        in_specs=[
            pl.BlockSpec((pl.Element(tm), pl.Element(d)), x_start),
            pl.BlockSpec((N_LEAD, d), lambda i, j: (0, 0)),
            pl.BlockSpec((1, d), lambda i, j: (0, 0)),
            pl.BlockSpec((d, tn), lambda i, j: (0, j)),
            pl.BlockSpec((d, LANES), lambda i, j: (0, 0)),
        ],
        out_specs=[
            pl.BlockSpec((tm, tn), lambda i, j: (i, j)),
            pl.BlockSpec((tm, LANES), lambda i, j: (i, 0)),
            pl.BlockSpec((tm, d), lambda i, j: (i, 0)),
        ],
        out_shape=[jax.ShapeDtypeStruct((m, n), BF16), jax.ShapeDtypeStruct((m, LANES), F32),
                   jax.ShapeDtypeStruct((m, d), F32)],
        scratch_shapes=[pltpu.VMEM((tm, d), BF16)],
        compiler_params=pltpu.CompilerParams(
            dimension_semantics=("parallel", "arbitrary"), vmem_limit_bytes=VMEM_LIMIT,
            allow_input_fusion=[False, False, False, True, True]),
        name="inproj",
    )(x.reshape(bsz * seq_x, d), lead, norm_w, w_main, w_dt)


def _ssd_kernel(z_ref, xs_ref, bc_ref, dt_ref, cwx_ref, cbx_ref, cwb_ref, cbb_ref,
                dtb_ref, alog_ref, dskip_ref, nw_ref, expand_ref, tril_ref,
                out_ref, tailx_scr, tailb_scr, state_scr, *, n_heads):
    c = pl.program_id(1)
    d_inner = xs_ref.shape[1]
    gw = d_inner // SSD_GROUPS
    hpg = n_heads // SSD_GROUPS

    @pl.when(c == 0)
    def _():
        tailx_scr[...] = jnp.zeros(tailx_scr.shape, F32)
        tailb_scr[...] = jnp.zeros(tailb_scr.shape, F32)
        state_scr[...] = jnp.zeros(state_scr.shape, F32)

    row = lax.broadcasted_iota(jnp.int32, (BLOCK, 1), 0)
    vmask = (c * BLOCK + row >= N_PAD).astype(F32)
    first_row = lax.broadcasted_iota(jnp.int32, (SUBLANES, 1), 0) == 0

    def conv(x_ref, tail_scr, w_ref, b_ref):
        x = x_ref[...].astype(F32)
        tail = tail_scr[...]
        tail_scr[...] = x[BLOCK - SUBLANES:, :]
        groups = [x[r * SUBLANES:(r + 1) * SUBLANES, :] for r in range(BLOCK // SUBLANES)]
        accs = [w_ref[0:1, :] * g for g in groups]
        tacc = w_ref[0:1, :] * tail
        for i in range(1, CONV_K):
            trot = pltpu.roll(tacc, 1, axis=0)
            rots = [pltpu.roll(a, 1, axis=0) for a in accs]
            befores = [trot] + rots[:-1]
            accs = [w_ref[i:i + 1, :] * g + jnp.where(first_row, before, rot)
                    for g, before, rot in zip(groups, befores, rots)]
            tacc = w_ref[i:i + 1, :] * tail + trot
        return _silu(jnp.concatenate(accs, axis=0) + b_ref[...]) * vmask

    xs = conv(xs_ref, tailx_scr, cwx_ref, cbx_ref)
    bc = conv(bc_ref, tailb_scr, cwb_ref, cbb_ref)
    gn = SSD_GROUPS * SSD_STATE
    b_all = bc[:, :gn].astype(BF16)
    c_all = bc[:, gn:].astype(BF16)

    dt = jax.nn.softplus(dt_ref[:, :n_heads] + dtb_ref[...]) * vmask
    a = -jnp.exp(alog_ref[...]) * dt
    a_cs = jnp.dot(tril_ref[...], a, preferred_element_type=F32, precision=HIGHEST)
    a_cs_t = a_cs.T
    a_last = a_cs[BLOCK - 1:BLOCK, :]
    decay_states = jnp.exp(a_last - a_cs)
    decay_out = jnp.exp(a_cs)
    chunk_decay = jnp.broadcast_to(jnp.exp(a_last), (SUBLANES, n_heads))
    stacked = jnp.concatenate([dt, decay_states, decay_out, chunk_decay], axis=0)
    st_hi = stacked.astype(BF16)
    st_lo = (stacked - st_hi.astype(F32)).astype(BF16)
    expanded = (jnp.dot(st_hi, expand_ref[...], preferred_element_type=F32)
                + jnp.dot(st_lo, expand_ref[...], preferred_element_type=F32))
    dt_x = expanded[0:BLOCK]
    ds_x = expanded[BLOCK:2 * BLOCK]
    do_x = expanded[2 * BLOCK:3 * BLOCK]
    cd_x = expanded[3 * BLOCK:3 * BLOCK + 1]

    x_dt = xs * dt_x
    x_dt_b = x_dt.astype(BF16)
    x_ds_b = (x_dt * ds_x).astype(BF16)

    li = lax.broadcasted_iota(jnp.int32, (BLOCK, BLOCK), 0)
    si = lax.broadcasted_iota(jnp.int32, (BLOCK, BLOCK), 1)
    causal = si <= li
    first_half = lax.broadcasted_iota(jnp.int32, (BLOCK, 2 * SSD_HEADDIM), 1) < SSD_HEADDIM

    y_groups = []
    for g in range(SSD_GROUPS):
        bg = b_all[:, g * SSD_STATE:(g + 1) * SSD_STATE]
        cg = c_all[:, g * SSD_STATE:(g + 1) * SSD_STATE]
        cols = slice(g * gw, (g + 1) * gw)
        cb = _nt_dot(cg, bg)
        prev = state_scr[g]
        y_off = jnp.dot(cg, prev.astype(BF16), preferred_element_type=F32) * do_x[:, cols]
        new_state = _tn_dot(bg, x_ds_b[:, cols])
        state_scr[g] = cd_x[:, cols] * prev + new_state
        pairs = []
        for jp in range(hpg // 2):
            h0 = g * hpg + 2 * jp
            pc = slice(g * gw + jp * 2 * SSD_HEADDIM, g * gw + (jp + 1) * 2 * SSD_HEADDIM)
            xp = x_dt_b[:, pc]
            ys = []
            for h in (h0, h0 + 1):
                seg = a_cs[:, h:h + 1] - a_cs_t[h:h + 1, :]
                m = (cb * jnp.exp(jnp.where(causal, seg, -jnp.inf))).astype(BF16)
                ys.append(jnp.dot(m, xp, preferred_element_type=F32))
            pairs.append(jnp.where(first_half, ys[0], ys[1]))
        y_groups.append(jnp.concatenate(pairs, axis=1) + y_off)
    y = jnp.concatenate(y_groups, axis=1) + dskip_ref[...] * xs

    gated = y * _silu(z_ref[...].astype(F32))
    outs = []
    for g in range(SSD_GROUPS):
        gg = gated[:, g * gw:(g + 1) * gw]
        outs.append(gg * lax.rsqrt(jnp.mean(gg * gg, axis=-1, keepdims=True) + EPS))
    out_ref[...] = (jnp.concatenate(outs, axis=1) * nw_ref[...]).astype(out_ref.dtype)


def _ssd(proj, dt_raw, conv_w, conv_b, dt_bias, a_log, d_skip, norm_w, bsz, nc, d_inner):
    m = proj.shape[0]
    n_heads = a_log.shape[0]
    bc_w = 2 * SSD_GROUPS * SSD_STATE
    gw = d_inner // SSD_GROUPS
    cwx, cwb = conv_w[:, :d_inner], conv_w[:, d_inner:]
    cbx, cbb = conv_b[None, :d_inner], conv_b[None, d_inner:]
    expand = jnp.repeat(jnp.eye(n_heads, dtype=BF16), SSD_HEADDIM, axis=1)
    tril = jnp.tril(jnp.ones((BLOCK, BLOCK), F32))
    dskip_x = jnp.repeat(d_skip.astype(F32), SSD_HEADDIM)[None, :]
    full = lambda shape: pl.BlockSpec(shape, lambda b, c: (0,) * len(shape))
    row_blk = lambda b, c: b * nc + c
    return pl.pallas_call(
        functools.partial(_ssd_kernel, n_heads=n_heads),
        grid=(bsz, nc),
        in_specs=[
            pl.BlockSpec((BLOCK, d_inner), lambda b, c: (row_blk(b, c), 0)),
            pl.BlockSpec((BLOCK, d_inner), lambda b, c: (row_blk(b, c), 1)),
            pl.BlockSpec((BLOCK, bc_w), lambda b, c: (row_blk(b, c), 2 * d_inner // bc_w)),
            pl.BlockSpec((BLOCK, LANES), lambda b, c: (row_blk(b, c), 0)),
            full((CONV_K, d_inner)), full((1, d_inner)), full((CONV_K, bc_w)), full((1, bc_w)),
            full((1, n_heads)), full((1, n_heads)), full((1, d_inner)), full((1, d_inner)),
            full((n_heads, d_inner)), full((BLOCK, BLOCK)),
        ],
        out_specs=pl.BlockSpec((BLOCK, d_inner), lambda b, c: (row_blk(b, c), 0)),
        out_shape=jax.ShapeDtypeStruct((m, d_inner), BF16),
        scratch_shapes=[
            pltpu.VMEM((SUBLANES, d_inner), F32),
            pltpu.VMEM((SUBLANES, bc_w), F32),
            pltpu.VMEM((SSD_GROUPS, SSD_STATE, gw), F32),
        ],
        compiler_params=pltpu.CompilerParams(
            dimension_semantics=("parallel", "arbitrary"), vmem_limit_bytes=VMEM_LIMIT),
        name="ssd",
    )(proj, proj, proj, dt_raw, cwx, cbx, cwb, cbb, dt_bias[None, :].astype(F32),
      a_log[None, :].astype(F32), dskip_x, norm_w[None, :].astype(F32), expand, tril)


def _attn_kernel(q_ref, k_ref, v_ref, upper_ref, out_ref, *, n_pairs):
    i = pl.program_id(2)
    pw = 2 * ATT_HEADDIM
    lane = lax.broadcasted_iota(jnp.int32, (BLOCK, pw), 1)
    first = lane < ATT_HEADDIM
    row2 = lax.broadcasted_iota(jnp.int32, (2 * BLOCK, BLOCK), 0)
    qpos = i * BLOCK + jnp.where(row2 < BLOCK, row2, row2 - BLOCK)
    kin = lax.broadcasted_iota(jnp.int32, (2 * BLOCK, BLOCK), 1)
    upper = upper_ref[...]
    scale = ATT_HEADDIM ** -0.5
    q2 = []
    for p in range(n_pairs):
        q = q_ref[:, p * pw:(p + 1) * pw] * jnp.asarray(scale, BF16)
        zero = jnp.zeros_like(q)
        q2.append(jnp.concatenate([jnp.where(first, q, zero), jnp.where(first, zero, q)], axis=0))

    def process(kbs, runs, accs):
        starts = [pl.multiple_of(jnp.maximum(kb, 0) * BLOCK, BLOCK) for kb in kbs]
        alloweds = []
        for kb in kbs:
            kpos = kb * BLOCK + kin
            alloweds.append(jnp.logical_and(kpos < qpos, kpos >= N_PAD))
        zs = [[_nt_dot(q2[p], k_ref[pl.ds(start, BLOCK), p * pw:(p + 1) * pw])
               for p in range(n_pairs)] for start in starts]
        log_betas, laters, sums = [], [], []
        for j in range(len(kbs)):
            log_betas.append([])
            laters.append([])
            sums.append([])
            for p in range(n_pairs):
                z = jnp.where(alloweds[j], zs[j][p], MASKED_SCORE)
                l1p = jnp.log(1.0 + jnp.exp(-jnp.abs(z)))
                log_beta = jnp.minimum(z, 0.0) - l1p
                log_stay = log_beta - z
                hi = log_stay.astype(BF16)
                lo = (log_stay - hi.astype(F32)).astype(BF16)
                log_betas[j].append(log_beta)
                laters[j].append(jnp.dot(hi, upper, preferred_element_type=F32)
                                 + jnp.dot(lo, upper, preferred_element_type=F32))
                sums[j].append(jnp.sum(log_stay, axis=1, keepdims=True))
        runs, accs = list(runs), list(accs)
        for j, start in enumerate(starts):
            for p in range(n_pairs):
                vblk = v_ref[pl.ds(start, BLOCK), p * pw:(p + 1) * pw]
                w = jnp.exp(log_betas[j][p] + laters[j][p] + runs[p]).astype(BF16)
                vzero = jnp.zeros_like(vblk)
                accs[p] = (accs[p]
                           + jnp.dot(w[:BLOCK], jnp.where(first, vblk, vzero),
                                     preferred_element_type=F32)
                           + jnp.dot(w[BLOCK:], jnp.where(first, vzero, vblk),
                                     preferred_element_type=F32))
                runs[p] = runs[p] + sums[j][p]
        top = jnp.max(runs[0])
        for r in runs[1:]:
            top = jnp.maximum(top, jnp.max(r))
        return top, runs, accs

    def cond(carry):
        kb, top = carry[0], carry[1]
        return jnp.logical_and(kb >= 0, top > EXP_UNDERFLOW)

    def body(carry):
        kb = carry[0]
        top, runs, accs = process([kb], carry[2:2 + n_pairs], carry[2 + n_pairs:])
        return (kb - 1, top, *runs, *accs)

    zero_runs = [jnp.zeros((2 * BLOCK, 1), F32) for _ in range(n_pairs)]
    zero_accs = [jnp.zeros((BLOCK, pw), F32) for _ in range(n_pairs)]
    top, runs, accs = process([i, i - 1], zero_runs, zero_accs)
    final = lax.while_loop(cond, body, (i - 2, top, *runs, *accs))
    for p in range(n_pairs):
        out_ref[:, p * pw:(p + 1) * pw] = final[2 + n_pairs + p].astype(out_ref.dtype)


def _attention(proj, bsz, nc, q_col, k_col, v_col, att_dim, pairs_per_step=4):
    m = proj.shape[0]
    seq = nc * BLOCK
    sw = pairs_per_step * 2 * ATT_HEADDIM
    upper = jnp.triu(jnp.ones((BLOCK, BLOCK), F32), 1).T.astype(BF16)
    return pl.pallas_call(
        functools.partial(_attn_kernel, n_pairs=pairs_per_step),
        grid=(bsz, att_dim // sw, nc),
        in_specs=[
            pl.BlockSpec((BLOCK, sw), lambda b, p, i: (b * nc + i, q_col // sw + p)),
            pl.BlockSpec((seq, sw), lambda b, p, i: (b, k_col // sw + p)),
            pl.BlockSpec((seq, sw), lambda b, p, i: (b, v_col // sw + p)),
            pl.BlockSpec((BLOCK, BLOCK), lambda b, p, i: (0, 0)),
        ],
        out_specs=pl.BlockSpec((BLOCK, sw), lambda b, p, i: (b * nc + i, p)),
        out_shape=jax.ShapeDtypeStruct((m, att_dim), BF16),
        compiler_params=pltpu.CompilerParams(
            dimension_semantics=("parallel", "parallel", "arbitrary"),
            vmem_limit_bytes=VMEM_LIMIT),
        name="attention",
    )(proj, proj, proj, upper)


def _merge_kernel(yssd_ref, att_ref, gs_ref, ga_ref, h_ref, wso_ref, wao_ref, wo_ref, out_ref):
    t = pl.program_id(1)
    y_ssd = jnp.dot(yssd_ref[...], wso_ref[...], preferred_element_type=F32)
    y_att = jnp.dot(att_ref[...], wao_ref[...], preferred_element_type=F32)
    merged = (jax.nn.sigmoid(gs_ref[...].astype(F32)) * y_ssd
              + jax.nn.sigmoid(ga_ref[...].astype(F32)) * y_att)
    mixed = jnp.dot(merged.astype(BF16), wo_ref[...], preferred_element_type=F32)
    row = lax.broadcasted_iota(jnp.int32, (ROW_TILE, 1), 0)
    valid = (t * ROW_TILE + row >= N_PAD).astype(F32)
    out_ref[...] = h_ref[...] + mixed * valid


def _merge(yssd, att, proj, h, w_ssd_out, w_att_out, w_out, bsz, seq, gs_col, ga_col):
    m, d = h.shape
    d_inner = yssd.shape[1]
    att_dim = att.shape[1]
    per = seq // ROW_TILE
    rb = lambda b, t: b * per + t
    full = lambda shape: pl.BlockSpec(shape, lambda b, t: (0,) * len(shape))
    return pl.pallas_call(
        _merge_kernel,
        grid=(bsz, per),
        in_specs=[
            pl.BlockSpec((ROW_TILE, d_inner), lambda b, t: (rb(b, t), 0)),
            pl.BlockSpec((ROW_TILE, att_dim), lambda b, t: (rb(b, t), 0)),
            pl.BlockSpec((ROW_TILE, d), lambda b, t: (rb(b, t), gs_col // d)),
            pl.BlockSpec((ROW_TILE, d), lambda b, t: (rb(b, t), ga_col // d)),
            pl.BlockSpec((ROW_TILE, d), lambda b, t: (rb(b, t), 0)),
            full((d_inner, d)), full((att_dim, d)), full((d, d)),
        ],
        out_specs=pl.BlockSpec((ROW_TILE, d), lambda b, t: (rb(b, t), 0)),
        out_shape=jax.ShapeDtypeStruct((m, d), F32),
        compiler_params=pltpu.CompilerParams(
            dimension_semantics=("parallel", "parallel"), vmem_limit_bytes=VMEM_LIMIT,
            allow_input_fusion=[False, False, False, False, True, True, True, True]),
        name="merge",
    )(yssd, att, proj, proj, h, w_ssd_out, w_att_out, w_out)


def _router_kernel(h_ref, nw_ref, wrh_ref, wrl_ref, br_ref, before_ref,
                   ut_ref, idx_ref, gate_ref, rank_ref, cnt_ref, base_scr):
    step = pl.program_id(0)
    n_exp = br_ref.shape[0]
    tm = h_ref.shape[0]

    @pl.when(step == 0)
    def _():
        base_scr[...] = jnp.zeros(base_scr.shape, F32)

    x = h_ref[...]
    u = x * lax.rsqrt(jnp.mean(x * x, axis=-1, keepdims=True) + EPS) * nw_ref[...]
    for c in range(SUBLANES):
        ut_ref[pl.ds(c, tm, stride=SUBLANES), :] = u[:, c * LANES:(c + 1) * LANES]

    u_hi = u.astype(BF16)
    u_lo = (u - u_hi.astype(F32)).astype(BF16)
    logits_tm = (jnp.dot(u_hi, wrh_ref[...], preferred_element_type=F32)
                 + jnp.dot(u_lo, wrh_ref[...], preferred_element_type=F32)
                 + jnp.dot(u_hi, wrl_ref[...], preferred_element_type=F32))
    logits = logits_tm.T[:n_exp] + br_ref[...]
    eidx = lax.broadcasted_iota(jnp.int32, (n_exp, tm), 0)
    work = logits
    tops, idxs, onehots = [], [], []
    for _ in range(TOP_K):
        top = jnp.max(work, axis=0, keepdims=True)
        idx = jnp.min(jnp.where(work == top, eidx, n_exp), axis=0, keepdims=True)
        hot = eidx == idx
        work = jnp.where(hot, -jnp.inf, work)
        tops.append(top)
        idxs.append(idx)
        onehots.append(hot.astype(F32))
    exps = [jnp.exp(v - tops[0]) for v in tops]
    denom = exps[0] + exps[1] + exps[2] + exps[3]
    cnt = onehots[0] + onehots[1] + onehots[2] + onehots[3]
    before = jnp.dot(cnt.astype(BF16), before_ref[...], preferred_element_type=F32) + base_scr[...]
    ranks = [jnp.sum(hot * before, axis=0, keepdims=True) for hot in onehots]
    pad_rows = SUBLANES - TOP_K
    idx_ref[...] = jnp.concatenate(idxs + [jnp.zeros((pad_rows, tm), jnp.int32)], axis=0)
    gate_ref[...] = jnp.concatenate([e / denom for e in exps] + [jnp.zeros((pad_rows, tm), F32)], axis=0)
    rank_ref[...] = jnp.concatenate(
        [r.astype(jnp.int32) for r in ranks] + [jnp.zeros((pad_rows, tm), jnp.int32)], axis=0)
    base_scr[...] = base_scr[...] + jnp.sum(cnt, axis=1, keepdims=True)
    cnt_ref[...] = jnp.broadcast_to(base_scr[...], cnt_ref.shape)


def _router(h1, norm_w, w_router, b_router):
    m, d = h1.shape
    n_exp = w_router.shape[1]
    before = jnp.triu(jnp.ones((ROW_TILE, ROW_TILE), F32), 1).astype(BF16)
    wr = jnp.pad(w_router.astype(F32), ((0, 0), (0, LANES - n_exp)))
    wr_hi = wr.astype(BF16)
    wr_lo = (wr - wr_hi.astype(F32)).astype(BF16)
    full = lambda shape: pl.BlockSpec(shape, lambda i: (0,) * len(shape))
    return pl.pallas_call(
        _router_kernel,
        grid=(m // ROW_TILE,),
        in_specs=[
            pl.BlockSpec((ROW_TILE, d), lambda i: (i, 0)),
            full((1, d)), full((d, LANES)), full((d, LANES)), full((n_exp, 1)),
            full((ROW_TILE, ROW_TILE)),
        ],
        out_specs=[
            pl.BlockSpec((ROW_TILE * SUBLANES, LANES), lambda i: (i, 0)),
            pl.BlockSpec((SUBLANES, ROW_TILE), lambda i: (0, i)),
            pl.BlockSpec((SUBLANES, ROW_TILE), lambda i: (0, i)),
            pl.BlockSpec((SUBLANES, ROW_TILE), lambda i: (0, i)),
            full((n_exp, LANES)),
        ],
        out_shape=[
            jax.ShapeDtypeStruct((m * SUBLANES, LANES), F32),
            jax.ShapeDtypeStruct((SUBLANES, m), jnp.int32),
            jax.ShapeDtypeStruct((SUBLANES, m), F32),
            jax.ShapeDtypeStruct((SUBLANES, m), jnp.int32),
            jax.ShapeDtypeStruct((n_exp, LANES), F32),
        ],
        scratch_shapes=[pltpu.VMEM((n_exp, 1), F32)],
        compiler_params=pltpu.CompilerParams(
            dimension_semantics=("arbitrary",), vmem_limit_bytes=VMEM_LIMIT),
        name="router",
    )(h1, norm_w, wr_hi, wr_lo, b_router[:, None], before)


def _dispatch_kernel(zstart_ref, zpad_ref, nused_ref, dest_ref, ut_ref, xs_ref, zero_scr, sem, zsem,
                     *, n_exp):
    step = pl.program_id(0)
    tb = dest_ref.shape[1]
    slot_rows = MOE_BLOCK * SUBLANES
    n_blocks = xs_ref.shape[0] // slot_rows

    @pl.when(step == 0)
    def _():
        zero_scr[...] = jnp.zeros(zero_scr.shape, F32)

        def zero_rows(row_start, n_rows):
            return pltpu.make_async_copy(zero_scr.at[pl.ds(0, n_rows)],
                                         xs_ref.at[pl.ds(row_start, n_rows)], zsem)

        def pad_fill(wait):
            def per_expert(e, carry):
                n_pad = zpad_ref[e]
                pos = zstart_ref[e]
                size = MOE_BLOCK // 2
                while size >= 1:
                    take = n_pad & size

                    @pl.when(take != 0)
                    def _(pos=pos, size=size):
                        copy = zero_rows(pl.multiple_of(pos * SUBLANES, SUBLANES), size * SUBLANES)
                        copy.wait() if wait else copy.start()

                    pos = pos + take
                    size //= 2
                return carry

            lax.fori_loop(0, n_exp, per_expert, 0)

        def tail_fill(wait):
            def per_block(b, carry):
                copy = zero_rows(pl.multiple_of(b * slot_rows, slot_rows), slot_rows)
                copy.wait() if wait else copy.start()
                return carry

            lax.fori_loop(nused_ref[0], n_blocks, per_block, 0)

        pad_fill(False)
        tail_fill(False)
        pad_fill(True)
        tail_fill(True)

    def issue(t, carry):
        src = pl.multiple_of(t * SUBLANES, SUBLANES)
        for k in range(TOP_K):
            dst = pl.multiple_of(dest_ref[k, t] * SUBLANES, SUBLANES)
            pltpu.make_async_copy(ut_ref.at[pl.ds(src, SUBLANES)],
                                  xs_ref.at[pl.ds(dst, SUBLANES)], sem).start(priority=k % 2)
        return carry

    lax.fori_loop(0, tb, issue, 0, unroll=4)
    for _ in range(TOP_K):
        pltpu.make_async_copy(ut_ref, xs_ref.at[pl.ds(0, tb * SUBLANES)], sem).wait()


def _dispatch(ut, dest, zstart, zpad, n_used, n_slots):
    m = dest.shape[1]
    n_exp = zstart.shape[0]
    return pl.pallas_call(
        functools.partial(_dispatch_kernel, n_exp=n_exp),
        grid_spec=pltpu.PrefetchScalarGridSpec(
            num_scalar_prefetch=3,
            grid=(m // ROW_TILE,),
            in_specs=[
                pl.BlockSpec((SUBLANES, ROW_TILE), lambda i, *_: (0, i), memory_space=pltpu.SMEM),
                pl.BlockSpec((ROW_TILE * SUBLANES, LANES), lambda i, *_: (i, 0)),
            ],
            out_specs=pl.BlockSpec(memory_space=pl.ANY),
            scratch_shapes=[
                pltpu.VMEM((MOE_BLOCK * SUBLANES, LANES), F32),
                pltpu.SemaphoreType.DMA(()),
                pltpu.SemaphoreType.DMA(()),
            ],
        ),
        out_shape=jax.ShapeDtypeStruct((n_slots * SUBLANES, LANES), F32),
        compiler_params=pltpu.CompilerParams(
            dimension_semantics=("arbitrary",), vmem_limit_bytes=VMEM_LIMIT),
        name="dispatch",
    )(zstart, zpad, n_used, dest, ut)


def _expert_kernel(be_ref, nused_ref, slot_ref, next_ref, xs_ref, wgu_hbm, bg_ref, bl_ref, wd_hbm,
                   bd_ref, perm_ref, y_ref, wgu_buf, wd_buf, wg_scr, wl_scr, wd_scr, sem):
    b = pl.program_id(0)
    pair = 2 * LANES
    changed = jnp.logical_or(b == 0, be_ref[b] != be_ref[jnp.maximum(b - 1, 0)])

    def fetch(expert, slot):
        return (pltpu.make_async_copy(wgu_hbm.at[expert], wgu_buf.at[slot], sem.at[0, slot]),
                pltpu.make_async_copy(wd_hbm.at[expert], wd_buf.at[slot], sem.at[1, slot]))

    @pl.when(b == 0)
    def _():
        for copy in fetch(be_ref[0], slot_ref[0]):
            copy.start()

    @pl.when(jnp.logical_and(changed, b < nused_ref[0]))
    def _():
        slot = slot_ref[b]
        for copy in fetch(be_ref[b], slot):
            copy.wait()

        @pl.when(next_ref[b] >= 0)
        def _():
            for copy in fetch(next_ref[b], 1 - slot):
                copy.start()

        for cb in range(wgu_buf.shape[2] // pair):
            wb = wgu_buf[slot, :, cb * pair:(cb + 1) * pair].astype(BF16)
            sep = jnp.dot(wb, perm_ref[...], preferred_element_type=F32).astype(BF16)
            wg_scr[:, cb * LANES:(cb + 1) * LANES] = sep[:, :LANES]
            wl_scr[:, cb * LANES:(cb + 1) * LANES] = sep[:, LANES:]
        wd_scr[...] = wd_buf[slot].astype(BF16)

    @pl.when(b < nused_ref[0])
    def _():
        x = jnp.concatenate(
            [xs_ref[pl.ds(c, MOE_BLOCK, stride=SUBLANES), :] for c in range(SUBLANES)],
            axis=1).astype(BF16)
        gate = jnp.dot(x, wg_scr[...], preferred_element_type=F32) + bg_ref[0]
        lin = jnp.dot(x, wl_scr[...], preferred_element_type=F32) + bl_ref[0]
        gate = jnp.minimum(gate, SWIGLU_LIMIT)
        lin = jnp.clip(lin, -SWIGLU_LIMIT, SWIGLU_LIMIT)
        act = gate * jax.nn.sigmoid(SWIGLU_ALPHA * gate) * (lin + 1.0)
        y = jnp.dot(act.astype(BF16), wd_scr[...], preferred_element_type=F32) + bd_ref[0]
        for c in range(SUBLANES):
            y_ref[pl.ds(c, MOE_BLOCK, stride=SUBLANES), :] = y[:, c * LANES:(c + 1) * LANES]

    @pl.when(b >= nused_ref[0])
    def _():
        y_ref[...] = jnp.zeros(y_ref.shape, F32)


def _experts(xs, block_expert, n_used, block_slot, block_next, w_gate_up, b_gate, b_lin, w_down,
             b_down, n_blocks):
    n_exp, d, d_ff2 = w_gate_up.shape
    d_ff = d_ff2 // 2
    rows = MOE_BLOCK * SUBLANES
    blk = lambda b, be, nu, *_: (jnp.minimum(b, nu[0] - 1), 0)
    wsel = lambda b, be, *_: (be[b], 0, 0)
    r = jnp.arange(2 * LANES)[:, None]
    c = jnp.arange(2 * LANES)[None, :]
    perm = (r == jnp.where(c < LANES, 2 * c, 2 * (c - LANES) + 1)).astype(BF16)
    return pl.pallas_call(
        _expert_kernel,
        grid_spec=pltpu.PrefetchScalarGridSpec(
            num_scalar_prefetch=4,
            grid=(n_blocks,),
            in_specs=[
                pl.BlockSpec((rows, LANES), blk),
                pl.BlockSpec(memory_space=pl.ANY),
                pl.BlockSpec((1, 1, d_ff), wsel), pl.BlockSpec((1, 1, d_ff), wsel),
                pl.BlockSpec(memory_space=pl.ANY), pl.BlockSpec((1, 1, d), wsel),
                pl.BlockSpec((2 * LANES, 2 * LANES), lambda b, *_: (0, 0)),
            ],
            out_specs=pl.BlockSpec((rows, LANES), lambda b, *_: (b, 0)),
            scratch_shapes=[
                pltpu.VMEM((2, d, d_ff2), F32), pltpu.VMEM((2, d_ff, d), F32),
                pltpu.VMEM((d, d_ff), BF16), pltpu.VMEM((d, d_ff), BF16), pltpu.VMEM((d_ff, d), BF16),
                pltpu.SemaphoreType.DMA((2, 2)),
            ],
        ),
        out_shape=jax.ShapeDtypeStruct((n_blocks * rows, LANES), F32),
        compiler_params=pltpu.CompilerParams(
            dimension_semantics=("arbitrary",), vmem_limit_bytes=VMEM_LIMIT),
        name="experts",
    )(block_expert, n_used, block_slot, block_next, xs, w_gate_up, b_gate, b_lin, w_down, b_down,
      perm)


def _combine_kernel(dest_ref, gate_ref, h_ref, nw_ref, eye_ref, ypad_ref, out_ref, *scratch, n_steps):
    bufs = scratch[:COMBINE_BUFFERS]
    sem = scratch[COMBINE_BUFFERS]
    s = pl.program_id(0)
    rows = BLOCK * SUBLANES
    n_groups = BLOCK // SUBLANES

    def step(new, old):
        def issue_group(g):
            for j in range(SUBLANES):
                t = g * SUBLANES + j
                for k in range(TOP_K):
                    src = pl.multiple_of(dest_ref[k, t] * SUBLANES, SUBLANES)
                    dst = pl.multiple_of(k * rows + t * SUBLANES, SUBLANES)
                    pltpu.make_async_copy(ypad_ref.at[pl.ds(src, SUBLANES)],
                                          bufs[new].at[pl.ds(dst, SUBLANES)],
                                          sem.at[new]).start(priority=k % 2)

        def compute():
            gates_t = _nt_dot(eye_ref[...], gate_ref[...], precision=HIGHEST)
            acc = h_ref[...]
            for k in range(TOP_K):
                yk = jnp.concatenate(
                    [bufs[old][pl.ds(k * rows + c, BLOCK, stride=SUBLANES), :]
                     for c in range(SUBLANES)], axis=1)
                acc = acc + gates_t[:, k:k + 1] * yk
            scale = lax.rsqrt(jnp.mean(acc * acc, axis=-1, keepdims=True) + EPS)
            out_ref[...] = acc * scale * nw_ref[...]

        def wait_old():
            pltpu.make_async_copy(ypad_ref.at[pl.ds(0, TOP_K * rows)], bufs[old], sem.at[old]).wait()

        @pl.when(s < COMBINE_LAG)
        def _():
            lax.fori_loop(0, n_groups, lambda g, carry: (issue_group(g), carry)[1], 0)

        @pl.when(jnp.logical_and(s >= COMBINE_LAG, s < n_steps))
        def _():
            wait_old()
            for g in range(n_groups):
                issue_group(g)
            compute()

        @pl.when(s >= n_steps)
        def _():
            wait_old()
            compute()

    for new in range(COMBINE_BUFFERS):
        pl.when(s % COMBINE_BUFFERS == new)(
            functools.partial(step, new, (new + 1) % COMBINE_BUFFERS))


def _combine(ypad, dest, gates, h1, final_norm_w, bsz, nc):
    m, d = h1.shape
    per = nc - 1
    n_steps = bsz * per
    eye = jnp.eye(BLOCK, dtype=F32)

    def tok_blk(s):
        s = jnp.clip(s, 0, n_steps - 1)
        return (s // per) * nc + 1 + s % per

    return pl.pallas_call(
        functools.partial(_combine_kernel, n_steps=n_steps),
        grid=(n_steps + COMBINE_LAG,),
        in_specs=[
            pl.BlockSpec((SUBLANES, BLOCK), lambda s: (0, tok_blk(s)), memory_space=pltpu.SMEM),
            pl.BlockSpec((SUBLANES, BLOCK), lambda s: (0, tok_blk(s - COMBINE_LAG))),
            pl.BlockSpec((BLOCK, d), lambda s: (tok_blk(s - COMBINE_LAG), 0)),
            pl.BlockSpec((1, d), lambda s: (0, 0)),
            pl.BlockSpec((BLOCK, BLOCK), lambda s: (0, 0)),
            pl.BlockSpec(memory_space=pl.ANY),
        ],
        out_specs=pl.BlockSpec((BLOCK, d), lambda s: (jnp.clip(s - COMBINE_LAG, 0, n_steps - 1), 0)),
        out_shape=jax.ShapeDtypeStruct((n_steps * BLOCK, d), F32),
        scratch_shapes=(
            [pltpu.VMEM((TOP_K * BLOCK * SUBLANES, LANES), F32) for _ in range(COMBINE_BUFFERS)]
            + [pltpu.SemaphoreType.DMA((COMBINE_BUFFERS,))]),
        compiler_params=pltpu.CompilerParams(
            dimension_semantics=("arbitrary",), vmem_limit_bytes=VMEM_LIMIT),
        name="combine",
    )(dest, gates, h1, final_norm_w[None, :], eye, ypad)


def _layer(x, lead, nc, mix_norm_w, w_in, conv_w, conv_b, dt_bias, a_log, d_skip, ssd_norm_w,
           w_ssd_out, w_att_out, w_out, ffn_norm_w, w_router, b_router, w_gate_up, b_gate_up,
           w_down, b_down):
    bsz, _, d = x.shape
    seq = nc * BLOCK
    m = bsz * seq
    n_heads = a_log.shape[0]
    d_inner = n_heads * SSD_HEADDIM
    conv_dim = d_inner + 2 * SSD_GROUPS * SSD_STATE
    att_dim = w_att_out.shape[0]
    n_exp = w_router.shape[1]

    o_xbc = d_inner
    o_dt = o_xbc + conv_dim
    o_q = o_dt + n_heads
    w_main = jnp.concatenate([w_in[:, :o_dt], w_in[:, o_q:]], axis=1).astype(BF16)
    w_dt = jnp.pad(w_in[:, o_dt:o_q], ((0, 0), (0, LANES - n_heads))).astype(BF16)
    q_col = o_dt
    k_col = q_col + att_dim
    v_col = k_col + att_dim
    gs_col = v_col + att_dim
    ga_col = gs_col + d

    proj, dt_raw, h = _inproj(x, lead, mix_norm_w[None, :], w_main, w_dt)
    yssd = _ssd(proj, dt_raw, conv_w, conv_b, dt_bias, a_log, d_skip, ssd_norm_w, bsz, nc, d_inner)
    att = _attention(proj, bsz, nc, q_col, k_col, v_col, att_dim)
    h1 = _merge(yssd, att, proj, h, w_ssd_out.astype(BF16), w_att_out.astype(BF16),
                w_out.astype(BF16), bsz, seq, gs_col, ga_col)

    ut, idx, gates, rank, cnt = _router(h1, ffn_norm_w[None, :], w_router, b_router)

    counts = cnt[:, 0].astype(jnp.int32)
    padded = (counts + MOE_BLOCK - 1) // MOE_BLOCK * MOE_BLOCK
    padded_ends = jnp.cumsum(padded)
    padded_starts = padded_ends - padded
    n_blocks = (m * TOP_K + MOE_BLOCK - 1) // MOE_BLOCK + n_exp
    dest = rank + jnp.sum(
        jnp.where(idx[None] == jnp.arange(n_exp, dtype=jnp.int32)[:, None, None],
                  padded_starts[:, None, None], 0), axis=0)
    block_first = jnp.arange(n_blocks, dtype=jnp.int32) * MOE_BLOCK
    block_expert = jnp.minimum(
        jnp.sum((padded_ends[None, :] <= block_first[:, None]).astype(jnp.int32), axis=1), n_exp - 1)
    n_used = (padded_ends[-1:] // MOE_BLOCK).astype(jnp.int32)
    zstart = (padded_starts + counts).astype(jnp.int32)

    zpad = (padded - counts).astype(jnp.int32)
    xs = _dispatch(ut, dest, zstart, zpad, n_used, n_blocks * MOE_BLOCK)
    b_gate = b_gate_up[:, None, 0::2]
    b_lin = b_gate_up[:, None, 1::2]
    experts = jnp.arange(n_exp, dtype=jnp.int32)
    nonempty = padded > 0
    order = jnp.cumsum(nonempty.astype(jnp.int32)) - 1
    later = jnp.where(nonempty[None, :] & (experts[None, :] > experts[:, None]), experts[None, :], n_exp)
    next_nonempty = jnp.min(later, axis=1)
    next_nonempty = jnp.where(next_nonempty >= n_exp, -1, next_nonempty)
    of_block = block_expert[:, None] == experts[None, :]
    block_slot = jnp.sum(jnp.where(of_block, order[None, :] % 2, 0), axis=1).astype(jnp.int32)
    block_next = jnp.sum(jnp.where(of_block, next_nonempty[None, :], 0), axis=1).astype(jnp.int32)
    ypad = _experts(xs, block_expert, n_used, block_slot, block_next, w_gate_up, b_gate, b_lin, w_down,
                    b_down[:, None, :], n_blocks)
    return h1, ypad, dest, gates


def kernel(x, meta_tokens, mix_norm_w, w_in, conv_w, conv_b, dt_bias, a_log, d_skip, ssd_norm_w,
           w_ssd_out, w_att_out, w_out, ffn_norm_w, w_router, b_router, w_gate_up, b_gate_up,
           w_down, b_down, final_norm_w):
    bsz, seq_x, d = x.shape
    depth = mix_norm_w.shape[0]
    assert depth == 1 and seq_x % BLOCK == 0 and (seq_x + N_LEAD) % ROW_TILE == 0
    nc = (seq_x + N_LEAD) // BLOCK
    lead = jnp.concatenate([jnp.zeros((N_PAD, d), x.dtype), meta_tokens.astype(x.dtype)], axis=0)
    layer = 0
    h1, ypad, dest, gates = _layer(
        x, lead, nc, mix_norm_w[layer], w_in[layer], conv_w[layer], conv_b[layer], dt_bias[layer],
        a_log[layer], d_skip[layer], ssd_norm_w[layer], w_ssd_out[layer], w_att_out[layer],
        w_out[layer], ffn_norm_w[layer], w_router[layer], b_router[layer], w_gate_up[layer],
        b_gate_up[layer], w_down[layer], b_down[layer])
    out = _combine(ypad, dest, gates, h1, final_norm_w, bsz, nc)
    return out.reshape(bsz, seq_x, d)
```

```python
import functools

import jax
import jax.numpy as jnp
from jax import lax
from jax.experimental import pallas as pl
from jax.experimental.pallas import tpu as pltpu

F32 = jnp.float32
BF16 = jnp.bfloat16
HIGHEST = lax.Precision.HIGHEST

N_META = 16
BLOCK = 128
N_LEAD = BLOCK
N_PAD = N_LEAD - N_META
EPS = 1e-5
SSD_HEADDIM = 64
SSD_GROUPS = 4
SSD_STATE = 128
CONV_K = 4
ATT_HEADDIM = 64
TOP_K = 4
SWIGLU_LIMIT = 7.0
SWIGLU_ALPHA = 1.702
MOE_BLOCK = 512

LANES = 128
SUBLANES = 8
ROW_TILE = 640
INPROJ_ROWS = 1664
COMBINE_LAG = 2
COMBINE_BUFFERS = COMBINE_LAG + 1
EXP_UNDERFLOW = -88.0
MASKED_SCORE = -1e30
VMEM_LIMIT = 56 * 1024 * 1024


def _nt_dot(a, b, precision=None):
    return lax.dot_general(a, b, (((1,), (1,)), ((), ())),
                           preferred_element_type=F32, precision=precision)


def _tn_dot(a, b):
    return lax.dot_general(a, b, (((0,), (0,)), ((), ())), preferred_element_type=F32)


def _silu(x):
    half = 0.5 * x
    return half + half * jnp.tanh(half)


def _inproj_kernel(x_ref, lead_ref, nw_ref, w_ref, wdt_ref, out_ref, dt_ref, h_ref, u_scr,
                   *, tiles_per_seq):
    tm = h_ref.shape[0]

    @pl.when(pl.program_id(1) == 0)
    def _():
        @pl.when(pl.program_id(0) % tiles_per_seq == 0)
        def _():
            h_ref[0:N_LEAD, :] = lead_ref[...]
            if tm > N_LEAD:
                h_ref[N_LEAD:, :] = x_ref[0:tm - N_LEAD, :]

        @pl.when(pl.program_id(0) % tiles_per_seq != 0)
        def _():
            h_ref[...] = x_ref[...]

        x = h_ref[...]
        u = x * lax.rsqrt(jnp.mean(x * x, axis=-1, keepdims=True) + EPS) * nw_ref[...]
        ub = u.astype(BF16)
        u_scr[...] = ub
        dt_ref[...] = jnp.dot(ub, wdt_ref[...], preferred_element_type=F32)

    out_ref[...] = jnp.dot(u_scr[...], w_ref[...], preferred_element_type=F32).astype(out_ref.dtype)


def _largest_row_tile(m, cap):
    return max(t for t in range(BLOCK, cap + 1, BLOCK) if m % t == 0)


def _inproj(x, lead, norm_w, w_main, w_dt, tn=1024):
    bsz, seq_x, d = x.shape
    seq = seq_x + N_LEAD
    m = bsz * seq
    n = w_main.shape[1]
    tm = _largest_row_tile(seq, min(INPROJ_ROWS, seq_x))
    per = seq // tm

    def x_start(i, j):
        row = (i // per) * seq_x + jnp.maximum((i % per) * tm - N_LEAD, 0)
        return (pl.multiple_of(row, BLOCK), 0)

    return pl.pallas_call(
        functools.partial(_inproj_kernel, tiles_per_seq=per),
        grid=(m // tm, n // tn),
        in_specs=[
            pl.BlockSpec((pl.Element(tm), pl.Element(d)), x_start),
            pl.BlockSpec((N_LEAD, d), lambda i, j: (0, 0)),
            pl.BlockSpec((1, d), lambda i, j: (0, 0)),
            pl.BlockSpec((d, tn), lambda i, j: (0, j)),
            pl.BlockSpec((d, LANES), lambda i, j: (0, 0)),
        ],
        out_specs=[
            pl.BlockSpec((tm, tn), lambda i, j: (i, j)),
            pl.BlockSpec((tm, LANES), lambda i, j: (i, 0)),
            pl.BlockSpec((tm, d), lambda i, j: (i, 0)),
        ],
        out_shape=[jax.ShapeDtypeStruct((m, n), BF16), jax.ShapeDtypeStruct((m, LANES), F32),
                   jax.ShapeDtypeStruct((m, d), F32)],
        scratch_shapes=[pltpu.VMEM((tm, d), BF16)],
        compiler_params=pltpu.CompilerParams(
            dimension_semantics=("parallel", "arbitrary"), vmem_limit_bytes=VMEM_LIMIT),
        name="inproj",
    )(x.reshape(bsz * seq_x, d), lead, norm_w, w_main, w_dt)


def _ssd_kernel(z_ref, xs_ref, bc_ref, dt_ref, cwx_ref, cbx_ref, cwb_ref, cbb_ref,
                dtb_ref, alog_ref, dskip_ref, nw_ref, expand_ref, tril_ref,
                out_ref, tailx_scr, tailb_scr, state_scr, *, n_heads):
    c = pl.program_id(1)
    d_inner = xs_ref.shape[1]
    gw = d_inner // SSD_GROUPS
    hpg = n_heads // SSD_GROUPS

    @pl.when(c == 0)
    def _():
        tailx_scr[...] = jnp.zeros(tailx_scr.shape, F32)
        tailb_scr[...] = jnp.zeros(tailb_scr.shape, F32)
        state_scr[...] = jnp.zeros(state_scr.shape, F32)

    row = lax.broadcasted_iota(jnp.int32, (BLOCK, 1), 0)
    vmask = (c * BLOCK + row >= N_PAD).astype(F32)
    first_row = lax.broadcasted_iota(jnp.int32, (SUBLANES, 1), 0) == 0

    def conv(x_ref, tail_scr, w_ref, b_ref):
        x = x_ref[...].astype(F32)
        tail = tail_scr[...]
        tail_scr[...] = x[BLOCK - SUBLANES:, :]
        groups = [x[r * SUBLANES:(r + 1) * SUBLANES, :] for r in range(BLOCK // SUBLANES)]
        accs = [w_ref[0:1, :] * g for g in groups]
        tacc = w_ref[0:1, :] * tail
        for i in range(1, CONV_K):
            trot = pltpu.roll(tacc, 1, axis=0)
            rots = [pltpu.roll(a, 1, axis=0) for a in accs]
            befores = [trot] + rots[:-1]
            accs = [w_ref[i:i + 1, :] * g + jnp.where(first_row, before, rot)
                    for g, before, rot in zip(groups, befores, rots)]
            tacc = w_ref[i:i + 1, :] * tail + trot
        return _silu(jnp.concatenate(accs, axis=0) + b_ref[...]) * vmask

    xs = conv(xs_ref, tailx_scr, cwx_ref, cbx_ref)
    bc = conv(bc_ref, tailb_scr, cwb_ref, cbb_ref)
    gn = SSD_GROUPS * SSD_STATE
    b_all = bc[:, :gn].astype(BF16)
    c_all = bc[:, gn:].astype(BF16)

    dt = jax.nn.softplus(dt_ref[:, :n_heads] + dtb_ref[...]) * vmask
    a = -jnp.exp(alog_ref[...]) * dt
    a_cs = jnp.dot(tril_ref[...], a, preferred_element_type=F32, precision=HIGHEST)
    a_cs_t = a_cs.T
    a_last = a_cs[BLOCK - 1:BLOCK, :]
    decay_states = jnp.exp(a_last - a_cs)
    decay_out = jnp.exp(a_cs)
    chunk_decay = jnp.broadcast_to(jnp.exp(a_last), (SUBLANES, n_heads))
    stacked = jnp.concatenate([dt, decay_states, decay_out, chunk_decay], axis=0)
    st_hi = stacked.astype(BF16)
    st_lo = (stacked - st_hi.astype(F32)).astype(BF16)
    expanded = (jnp.dot(st_hi, expand_ref[...], preferred_element_type=F32)
                + jnp.dot(st_lo, expand_ref[...], preferred_element_type=F32))
    dt_x = expanded[0:BLOCK]
    ds_x = expanded[BLOCK:2 * BLOCK]
    do_x = expanded[2 * BLOCK:3 * BLOCK]
    cd_x = expanded[3 * BLOCK:3 * BLOCK + 1]

    x_dt = xs * dt_x
    x_dt_b = x_dt.astype(BF16)
    x_ds_b = (x_dt * ds_x).astype(BF16)

    li = lax.broadcasted_iota(jnp.int32, (BLOCK, BLOCK), 0)
    si = lax.broadcasted_iota(jnp.int32, (BLOCK, BLOCK), 1)
    causal = si <= li
    first_half = lax.broadcasted_iota(jnp.int32, (BLOCK, 2 * SSD_HEADDIM), 1) < SSD_HEADDIM

    y_groups = []
    for g in range(SSD_GROUPS):
        bg = b_all[:, g * SSD_STATE:(g + 1) * SSD_STATE]
        cg = c_all[:, g * SSD_STATE:(g + 1) * SSD_STATE]
        cols = slice(g * gw, (g + 1) * gw)
        cb = _nt_dot(cg, bg)
        prev = state_scr[g]
        y_off = jnp.dot(cg, prev.astype(BF16), preferred_element_type=F32) * do_x[:, cols]
        new_state = _tn_dot(bg, x_ds_b[:, cols])
        state_scr[g] = cd_x[:, cols] * prev + new_state
        pairs = []
        for jp in range(hpg // 2):
            h0 = g * hpg + 2 * jp
            pc = slice(g * gw + jp * 2 * SSD_HEADDIM, g * gw + (jp + 1) * 2 * SSD_HEADDIM)
            xp = x_dt_b[:, pc]
            ys = []
            for h in (h0, h0 + 1):
                seg = a_cs[:, h:h + 1] - a_cs_t[h:h + 1, :]
                m = (cb * jnp.exp(jnp.where(causal, seg, -jnp.inf))).astype(BF16)
                ys.append(jnp.dot(m, xp, preferred_element_type=F32))
            pairs.append(jnp.where(first_half, ys[0], ys[1]))
        y_groups.append(jnp.concatenate(pairs, axis=1) + y_off)
    y = jnp.concatenate(y_groups, axis=1) + dskip_ref[...] * xs

    gated = y * _silu(z_ref[...].astype(F32))
    outs = []
    for g in range(SSD_GROUPS):
        gg = gated[:, g * gw:(g + 1) * gw]
        outs.append(gg * lax.rsqrt(jnp.mean(gg * gg, axis=-1, keepdims=True) + EPS))
    out_ref[...] = (jnp.concatenate(outs, axis=1) * nw_ref[...]).astype(out_ref.dtype)


def _ssd(proj, dt_raw, conv_w, conv_b, dt_bias, a_log, d_skip, norm_w, bsz, nc, d_inner):
    m = proj.shape[0]
    n_heads = a_log.shape[0]
    bc_w = 2 * SSD_GROUPS * SSD_STATE
    gw = d_inner // SSD_GROUPS
    cwx, cwb = conv_w[:, :d_inner], conv_w[:, d_inner:]
    cbx, cbb = conv_b[None, :d_inner], conv_b[None, d_inner:]
    expand = jnp.repeat(jnp.eye(n_heads, dtype=BF16), SSD_HEADDIM, axis=1)
    tril = jnp.tril(jnp.ones((BLOCK, BLOCK), F32))
    dskip_x = jnp.repeat(d_skip.astype(F32), SSD_HEADDIM)[None, :]
    full = lambda shape: pl.BlockSpec(shape, lambda b, c: (0,) * len(shape))
    row_blk = lambda b, c: b * nc + c
    return pl.pallas_call(
        functools.partial(_ssd_kernel, n_heads=n_heads),
        grid=(bsz, nc),
        in_specs=[
            pl.BlockSpec((BLOCK, d_inner), lambda b, c: (row_blk(b, c), 0)),
            pl.BlockSpec((BLOCK, d_inner), lambda b, c: (row_blk(b, c), 1)),
            pl.BlockSpec((BLOCK, bc_w), lambda b, c: (row_blk(b, c), 2 * d_inner // bc_w)),
            pl.BlockSpec((BLOCK, LANES), lambda b, c: (row_blk(b, c), 0)),
            full((CONV_K, d_inner)), full((1, d_inner)), full((CONV_K, bc_w)), full((1, bc_w)),
            full((1, n_heads)), full((1, n_heads)), full((1, d_inner)), full((1, d_inner)),
            full((n_heads, d_inner)), full((BLOCK, BLOCK)),
        ],
        out_specs=pl.BlockSpec((BLOCK, d_inner), lambda b, c: (row_blk(b, c), 0)),
        out_shape=jax.ShapeDtypeStruct((m, d_inner), BF16),
        scratch_shapes=[
            pltpu.VMEM((SUBLANES, d_inner), F32),
            pltpu.VMEM((SUBLANES, bc_w), F32),
            pltpu.VMEM((SSD_GROUPS, SSD_STATE, gw), F32),
        ],
        compiler_params=pltpu.CompilerParams(
            dimension_semantics=("parallel", "arbitrary"), vmem_limit_bytes=VMEM_LIMIT),
        name="ssd",
    )(proj, proj, proj, dt_raw, cwx, cbx, cwb, cbb, dt_bias[None, :].astype(F32),
      a_log[None, :].astype(F32), dskip_x, norm_w[None, :].astype(F32), expand, tril)


def _attn_kernel(q_ref, k_ref, v_ref, upper_ref, out_ref, *, n_pairs):
    i = pl.program_id(2)
    pw = 2 * ATT_HEADDIM
    lane = lax.broadcasted_iota(jnp.int32, (BLOCK, pw), 1)
    first = lane < ATT_HEADDIM
    row2 = lax.broadcasted_iota(jnp.int32, (2 * BLOCK, BLOCK), 0)
    qpos = i * BLOCK + jnp.where(row2 < BLOCK, row2, row2 - BLOCK)
    kin = lax.broadcasted_iota(jnp.int32, (2 * BLOCK, BLOCK), 1)
    upper = upper_ref[...]
    scale = ATT_HEADDIM ** -0.5
    q2 = []
    for p in range(n_pairs):
        q = q_ref[:, p * pw:(p + 1) * pw] * jnp.asarray(scale, BF16)
        zero = jnp.zeros_like(q)
        q2.append(jnp.concatenate([jnp.where(first, q, zero), jnp.where(first, zero, q)], axis=0))

    def process(kbs, runs, accs):
        starts = [pl.multiple_of(jnp.maximum(kb, 0) * BLOCK, BLOCK) for kb in kbs]
        alloweds = []
        for kb in kbs:
            kpos = kb * BLOCK + kin
            alloweds.append(jnp.logical_and(kpos < qpos, kpos >= N_PAD))
        zs = [[_nt_dot(q2[p], k_ref[pl.ds(start, BLOCK), p * pw:(p + 1) * pw])
               for p in range(n_pairs)] for start in starts]
        log_betas, laters, sums = [], [], []
        for j in range(len(kbs)):
            log_betas.append([])
            laters.append([])
            sums.append([])
            for p in range(n_pairs):
                z = jnp.where(alloweds[j], zs[j][p], MASKED_SCORE)
                l1p = jnp.log(1.0 + jnp.exp(-jnp.abs(z)))
                log_beta = jnp.minimum(z, 0.0) - l1p
                log_stay = log_beta - z
                hi = log_stay.astype(BF16)
                lo = (log_stay - hi.astype(F32)).astype(BF16)
                log_betas[j].append(log_beta)
                laters[j].append(jnp.dot(hi, upper, preferred_element_type=F32)
                                 + jnp.dot(lo, upper, preferred_element_type=F32))
                sums[j].append(jnp.sum(log_stay, axis=1, keepdims=True))
        runs, accs = list(runs), list(accs)
        for j, start in enumerate(starts):
            for p in range(n_pairs):
                vblk = v_ref[pl.ds(start, BLOCK), p * pw:(p + 1) * pw]
                w = jnp.exp(log_betas[j][p] + laters[j][p] + runs[p]).astype(BF16)
                vzero = jnp.zeros_like(vblk)
                accs[p] = (accs[p]
                           + jnp.dot(w[:BLOCK], jnp.where(first, vblk, vzero),
                                     preferred_element_type=F32)
                           + jnp.dot(w[BLOCK:], jnp.where(first, vzero, vblk),
                                     preferred_element_type=F32))
                runs[p] = runs[p] + sums[j][p]
        top = jnp.max(runs[0])
        for r in runs[1:]:
            top = jnp.maximum(top, jnp.max(r))
        return top, runs, accs

    def cond(carry):
        kb, top = carry[0], carry[1]
        return jnp.logical_and(kb >= 0, top > EXP_UNDERFLOW)

    def body(carry):
        kb = carry[0]
        top, runs, accs = process([kb], carry[2:2 + n_pairs], carry[2 + n_pairs:])
        return (kb - 1, top, *runs, *accs)

    zero_runs = [jnp.zeros((2 * BLOCK, 1), F32) for _ in range(n_pairs)]
    zero_accs = [jnp.zeros((BLOCK, pw), F32) for _ in range(n_pairs)]
    top, runs, accs = process([i, i - 1], zero_runs, zero_accs)
    final = lax.while_loop(cond, body, (i - 2, top, *runs, *accs))
    for p in range(n_pairs):
        out_ref[:, p * pw:(p + 1) * pw] = final[2 + n_pairs + p].astype(out_ref.dtype)


def _attention(proj, bsz, nc, q_col, k_col, v_col, att_dim, pairs_per_step=4):
    m = proj.shape[0]
    seq = nc * BLOCK
    sw = pairs_per_step * 2 * ATT_HEADDIM
    upper = jnp.triu(jnp.ones((BLOCK, BLOCK), F32), 1).T.astype(BF16)
    return pl.pallas_call(
        functools.partial(_attn_kernel, n_pairs=pairs_per_step),
        grid=(bsz, att_dim // sw, nc),
        in_specs=[
            pl.BlockSpec((BLOCK, sw), lambda b, p, i: (b * nc + i, q_col // sw + p)),
            pl.BlockSpec((seq, sw), lambda b, p, i: (b, k_col // sw + p)),
            pl.BlockSpec((seq, sw), lambda b, p, i: (b, v_col // sw + p)),
            pl.BlockSpec((BLOCK, BLOCK), lambda b, p, i: (0, 0)),
        ],
        out_specs=pl.BlockSpec((BLOCK, sw), lambda b, p, i: (b * nc + i, p)),
        out_shape=jax.ShapeDtypeStruct((m, att_dim), BF16),
        compiler_params=pltpu.CompilerParams(
            dimension_semantics=("parallel", "parallel", "arbitrary"),
            vmem_limit_bytes=VMEM_LIMIT),
        name="attention",
    )(proj, proj, proj, upper)


def _merge_kernel(yssd_ref, att_ref, gs_ref, ga_ref, h_ref, wso_ref, wao_ref, wo_ref, out_ref):
    t = pl.program_id(1)
    y_ssd = jnp.dot(yssd_ref[...], wso_ref[...], preferred_element_type=F32)
    y_att = jnp.dot(att_ref[...], wao_ref[...], preferred_element_type=F32)
    merged = (jax.nn.sigmoid(gs_ref[...].astype(F32)) * y_ssd
              + jax.nn.sigmoid(ga_ref[...].astype(F32)) * y_att)
    mixed = jnp.dot(merged.astype(BF16), wo_ref[...], preferred_element_type=F32)
    row = lax.broadcasted_iota(jnp.int32, (ROW_TILE, 1), 0)
    valid = (t * ROW_TILE + row >= N_PAD).astype(F32)
    out_ref[...] = h_ref[...] + mixed * valid


def _merge(yssd, att, proj, h, w_ssd_out, w_att_out, w_out, bsz, seq, gs_col, ga_col):
    m, d = h.shape
    d_inner = yssd.shape[1]
    att_dim = att.shape[1]
    per = seq // ROW_TILE
    rb = lambda b, t: b * per + t
    full = lambda shape: pl.BlockSpec(shape, lambda b, t: (0,) * len(shape))
    return pl.pallas_call(
        _merge_kernel,
        grid=(bsz, per),
        in_specs=[
            pl.BlockSpec((ROW_TILE, d_inner), lambda b, t: (rb(b, t), 0)),
            pl.BlockSpec((ROW_TILE, att_dim), lambda b, t: (rb(b, t), 0)),
            pl.BlockSpec((ROW_TILE, d), lambda b, t: (rb(b, t), gs_col // d)),
            pl.BlockSpec((ROW_TILE, d), lambda b, t: (rb(b, t), ga_col // d)),
            pl.BlockSpec((ROW_TILE, d), lambda b, t: (rb(b, t), 0)),
            full((d_inner, d)), full((att_dim, d)), full((d, d)),
        ],
        out_specs=pl.BlockSpec((ROW_TILE, d), lambda b, t: (rb(b, t), 0)),
        out_shape=jax.ShapeDtypeStruct((m, d), F32),
        compiler_params=pltpu.CompilerParams(
            dimension_semantics=("parallel", "parallel"), vmem_limit_bytes=VMEM_LIMIT),
        name="merge",
    )(yssd, att, proj, proj, h, w_ssd_out, w_att_out, w_out)


def _router_kernel(h_ref, nw_ref, wrh_ref, wrl_ref, br_ref, before_ref,
                   ut_ref, idx_ref, gate_ref, rank_ref, cnt_ref, base_scr):
    step = pl.program_id(0)
    n_exp = br_ref.shape[0]
    tm = h_ref.shape[0]

    @pl.when(step == 0)
    def _():
        base_scr[...] = jnp.zeros(base_scr.shape, F32)

    x = h_ref[...]
    u = x * lax.rsqrt(jnp.mean(x * x, axis=-1, keepdims=True) + EPS) * nw_ref[...]
    for c in range(SUBLANES):
        ut_ref[pl.ds(c, tm, stride=SUBLANES), :] = u[:, c * LANES:(c + 1) * LANES]

    u_hi = u.astype(BF16)
    u_lo = (u - u_hi.astype(F32)).astype(BF16)
    logits_tm = (jnp.dot(u_hi, wrh_ref[...], preferred_element_type=F32)
                 + jnp.dot(u_lo, wrh_ref[...], preferred_element_type=F32)
                 + jnp.dot(u_hi, wrl_ref[...], preferred_element_type=F32))
    logits = logits_tm.T[:n_exp] + br_ref[...]
    eidx = lax.broadcasted_iota(jnp.int32, (n_exp, tm), 0)
    work = logits
    tops, idxs, onehots = [], [], []
    for _ in range(TOP_K):
        top = jnp.max(work, axis=0, keepdims=True)
        idx = jnp.min(jnp.where(work == top, eidx, n_exp), axis=0, keepdims=True)
        hot = eidx == idx
        work = jnp.where(hot, -jnp.inf, work)
        tops.append(top)
        idxs.append(idx)
        onehots.append(hot.astype(F32))
    exps = [jnp.exp(v - tops[0]) for v in tops]
    denom = exps[0] + exps[1] + exps[2] + exps[3]
    cnt = onehots[0] + onehots[1] + onehots[2] + onehots[3]
    before = jnp.dot(cnt.astype(BF16), before_ref[...], preferred_element_type=F32) + base_scr[...]
    ranks = [jnp.sum(hot * before, axis=0, keepdims=True) for hot in onehots]
    pad_rows = SUBLANES - TOP_K
    idx_ref[...] = jnp.concatenate(idxs + [jnp.zeros((pad_rows, tm), jnp.int32)], axis=0)
    gate_ref[...] = jnp.concatenate([e / denom for e in exps] + [jnp.zeros((pad_rows, tm), F32)], axis=0)
    rank_ref[...] = jnp.concatenate(
        [r.astype(jnp.int32) for r in ranks] + [jnp.zeros((pad_rows, tm), jnp.int32)], axis=0)
    base_scr[...] = base_scr[...] + jnp.sum(cnt, axis=1, keepdims=True)
    cnt_ref[...] = jnp.broadcast_to(base_scr[...], cnt_ref.shape)


def _router(h1, norm_w, w_router, b_router):
    m, d = h1.shape
    n_exp = w_router.shape[1]
    before = jnp.triu(jnp.ones((ROW_TILE, ROW_TILE), F32), 1).astype(BF16)
    wr = jnp.pad(w_router.astype(F32), ((0, 0), (0, LANES - n_exp)))
    wr_hi = wr.astype(BF16)
    wr_lo = (wr - wr_hi.astype(F32)).astype(BF16)
    full = lambda shape: pl.BlockSpec(shape, lambda i: (0,) * len(shape))
    return pl.pallas_call(
        _router_kernel,
        grid=(m // ROW_TILE,),
        in_specs=[
            pl.BlockSpec((ROW_TILE, d), lambda i: (i, 0)),
            full((1, d)), full((d, LANES)), full((d, LANES)), full((n_exp, 1)),
            full((ROW_TILE, ROW_TILE)),
        ],
        out_specs=[
            pl.BlockSpec((ROW_TILE * SUBLANES, LANES), lambda i: (i, 0)),
            pl.BlockSpec((SUBLANES, ROW_TILE), lambda i: (0, i)),
            pl.BlockSpec((SUBLANES, ROW_TILE), lambda i: (0, i)),
            pl.BlockSpec((SUBLANES, ROW_TILE), lambda i: (0, i)),
            full((n_exp, LANES)),
        ],
        out_shape=[
            jax.ShapeDtypeStruct((m * SUBLANES, LANES), F32),
            jax.ShapeDtypeStruct((SUBLANES, m), jnp.int32),
            jax.ShapeDtypeStruct((SUBLANES, m), F32),
            jax.ShapeDtypeStruct((SUBLANES, m), jnp.int32),
            jax.ShapeDtypeStruct((n_exp, LANES), F32),
        ],
        scratch_shapes=[pltpu.VMEM((n_exp, 1), F32)],
        compiler_params=pltpu.CompilerParams(
            dimension_semantics=("arbitrary",), vmem_limit_bytes=VMEM_LIMIT),
        name="router",
    )(h1, norm_w, wr_hi, wr_lo, b_router[:, None], before)


def _dispatch_kernel(zstart_ref, zpad_ref, nused_ref, dest_ref, ut_ref, xs_ref, zero_scr, sem, zsem,
                     *, n_exp):
    step = pl.program_id(0)
    tb = dest_ref.shape[1]
    slot_rows = MOE_BLOCK * SUBLANES
    n_blocks = xs_ref.shape[0] // slot_rows

    @pl.when(step == 0)
    def _():
        zero_scr[...] = jnp.zeros(zero_scr.shape, F32)

        def zero_rows(row_start, n_rows):
            return pltpu.make_async_copy(zero_scr.at[pl.ds(0, n_rows)],
                                         xs_ref.at[pl.ds(row_start, n_rows)], zsem)

        def pad_fill(wait):
            def per_expert(e, carry):
                n_pad = zpad_ref[e]
                pos = zstart_ref[e]
                size = MOE_BLOCK // 2
                while size >= 1:
                    take = n_pad & size

                    @pl.when(take != 0)
                    def _(pos=pos, size=size):
                        copy = zero_rows(pl.multiple_of(pos * SUBLANES, SUBLANES), size * SUBLANES)
                        copy.wait() if wait else copy.start()

                    pos = pos + take
                    size //= 2
                return carry

            lax.fori_loop(0, n_exp, per_expert, 0)

        def tail_fill(wait):
            def per_block(b, carry):
                copy = zero_rows(pl.multiple_of(b * slot_rows, slot_rows), slot_rows)
                copy.wait() if wait else copy.start()
                return carry

            lax.fori_loop(nused_ref[0], n_blocks, per_block, 0)

        pad_fill(False)
        tail_fill(False)
        pad_fill(True)
        tail_fill(True)

    def issue(t, carry):
        src = pl.multiple_of(t * SUBLANES, SUBLANES)
        for k in range(TOP_K):
            dst = pl.multiple_of(dest_ref[k, t] * SUBLANES, SUBLANES)
            pltpu.make_async_copy(ut_ref.at[pl.ds(src, SUBLANES)],
                                  xs_ref.at[pl.ds(dst, SUBLANES)], sem).start(priority=k % 2)
        return carry

    lax.fori_loop(0, tb, issue, 0, unroll=4)
    for _ in range(TOP_K):
        pltpu.make_async_copy(ut_ref, xs_ref.at[pl.ds(0, tb * SUBLANES)], sem).wait()


def _dispatch(ut, dest, zstart, zpad, n_used, n_slots):
    m = dest.shape[1]
    n_exp = zstart.shape[0]
    return pl.pallas_call(
        functools.partial(_dispatch_kernel, n_exp=n_exp),
        grid_spec=pltpu.PrefetchScalarGridSpec(
            num_scalar_prefetch=3,
            grid=(m // ROW_TILE,),
            in_specs=[
                pl.BlockSpec((SUBLANES, ROW_TILE), lambda i, *_: (0, i), memory_space=pltpu.SMEM),
                pl.BlockSpec((ROW_TILE * SUBLANES, LANES), lambda i, *_: (i, 0)),
            ],
            out_specs=pl.BlockSpec(memory_space=pl.ANY),
            scratch_shapes=[
                pltpu.VMEM((MOE_BLOCK * SUBLANES, LANES), F32),
                pltpu.SemaphoreType.DMA(()),
                pltpu.SemaphoreType.DMA(()),
            ],
        ),
        out_shape=jax.ShapeDtypeStruct((n_slots * SUBLANES, LANES), F32),
        compiler_params=pltpu.CompilerParams(
            dimension_semantics=("arbitrary",), vmem_limit_bytes=VMEM_LIMIT),
        name="dispatch",
    )(zstart, zpad, n_used, dest, ut)


def _expert_kernel(be_ref, nused_ref, slot_ref, next_ref, xs_ref, wgu_hbm, bg_ref, bl_ref, wd_hbm,
                   bd_ref, perm_ref, y_ref, wgu_buf, wd_buf, wg_scr, wl_scr, wd_scr, sem):
    b = pl.program_id(0)
    pair = 2 * LANES
    changed = jnp.logical_or(b == 0, be_ref[b] != be_ref[jnp.maximum(b - 1, 0)])

    def fetch(expert, slot):
        return (pltpu.make_async_copy(wgu_hbm.at[expert], wgu_buf.at[slot], sem.at[0, slot]),
                pltpu.make_async_copy(wd_hbm.at[expert], wd_buf.at[slot], sem.at[1, slot]))

    @pl.when(b == 0)
    def _():
        for copy in fetch(be_ref[0], slot_ref[0]):
            copy.start()

    @pl.when(jnp.logical_and(changed, b < nused_ref[0]))
    def _():
        slot = slot_ref[b]
        for copy in fetch(be_ref[b], slot):
            copy.wait()

        @pl.when(next_ref[b] >= 0)
        def _():
            for copy in fetch(next_ref[b], 1 - slot):
                copy.start()

        for cb in range(wgu_buf.shape[2] // pair):
            wb = wgu_buf[slot, :, cb * pair:(cb + 1) * pair].astype(BF16)
            sep = jnp.dot(wb, perm_ref[...], preferred_element_type=F32).astype(BF16)
            wg_scr[:, cb * LANES:(cb + 1) * LANES] = sep[:, :LANES]
            wl_scr[:, cb * LANES:(cb + 1) * LANES] = sep[:, LANES:]
        wd_scr[...] = wd_buf[slot].astype(BF16)

    @pl.when(b < nused_ref[0])
    def _():
        x = jnp.concatenate(
            [xs_ref[pl.ds(c, MOE_BLOCK, stride=SUBLANES), :] for c in range(SUBLANES)],
            axis=1).astype(BF16)
        gate = jnp.dot(x, wg_scr[...], preferred_element_type=F32) + bg_ref[0]
        lin = jnp.dot(x, wl_scr[...], preferred_element_type=F32) + bl_ref[0]
        gate = jnp.minimum(gate, SWIGLU_LIMIT)
        lin = jnp.clip(lin, -SWIGLU_LIMIT, SWIGLU_LIMIT)
        act = gate * jax.nn.sigmoid(SWIGLU_ALPHA * gate) * (lin + 1.0)
        y = jnp.dot(act.astype(BF16), wd_scr[...], preferred_element_type=F32) + bd_ref[0]
        for c in range(SUBLANES):
            y_ref[pl.ds(c, MOE_BLOCK, stride=SUBLANES), :] = y[:, c * LANES:(c + 1) * LANES]

    @pl.when(b >= nused_ref[0])
    def _():
        y_ref[...] = jnp.zeros(y_ref.shape, F32)


def _experts(xs, block_expert, n_used, block_slot, block_next, w_gate_up, b_gate, b_lin, w_down,
             b_down, n_blocks):
    n_exp, d, d_ff2 = w_gate_up.shape
    d_ff = d_ff2 // 2
    rows = MOE_BLOCK * SUBLANES
    blk = lambda b, be, nu, *_: (jnp.minimum(b, nu[0] - 1), 0)
    wsel = lambda b, be, *_: (be[b], 0, 0)
    r = jnp.arange(2 * LANES)[:, None]
    c = jnp.arange(2 * LANES)[None, :]
    perm = (r == jnp.where(c < LANES, 2 * c, 2 * (c - LANES) + 1)).astype(BF16)
    return pl.pallas_call(
        _expert_kernel,
        grid_spec=pltpu.PrefetchScalarGridSpec(
            num_scalar_prefetch=4,
            grid=(n_blocks,),
            in_specs=[
                pl.BlockSpec((rows, LANES), blk),
                pl.BlockSpec(memory_space=pl.ANY),
                pl.BlockSpec((1, 1, d_ff), wsel), pl.BlockSpec((1, 1, d_ff), wsel),
                pl.BlockSpec(memory_space=pl.ANY), pl.BlockSpec((1, 1, d), wsel),
                pl.BlockSpec((2 * LANES, 2 * LANES), lambda b, *_: (0, 0)),
            ],
            out_specs=pl.BlockSpec((rows, LANES), lambda b, *_: (b, 0)),
            scratch_shapes=[
                pltpu.VMEM((2, d, d_ff2), F32), pltpu.VMEM((2, d_ff, d), F32),
                pltpu.VMEM((d, d_ff), BF16), pltpu.VMEM((d, d_ff), BF16), pltpu.VMEM((d_ff, d), BF16),
                pltpu.SemaphoreType.DMA((2, 2)),
            ],
        ),
        out_shape=jax.ShapeDtypeStruct((n_blocks * rows, LANES), F32),
        compiler_params=pltpu.CompilerParams(
            dimension_semantics=("arbitrary",), vmem_limit_bytes=VMEM_LIMIT),
        name="experts",
    )(block_expert, n_used, block_slot, block_next, xs, w_gate_up, b_gate, b_lin, w_down, b_down,
      perm)


def _combine_kernel(dest_ref, gate_ref, h_ref, nw_ref, eye_ref, ypad_ref, out_ref, *scratch, n_steps):
    bufs = scratch[:COMBINE_BUFFERS]
    sem = scratch[COMBINE_BUFFERS]
    s = pl.program_id(0)
    rows = BLOCK * SUBLANES
    n_groups = BLOCK // SUBLANES

    def step(new, old):
        def issue_group(g):
            for j in range(SUBLANES):
                t = g * SUBLANES + j
                for k in range(TOP_K):
                    src = pl.multiple_of(dest_ref[k, t] * SUBLANES, SUBLANES)
                    dst = pl.multiple_of(k * rows + t * SUBLANES, SUBLANES)
                    pltpu.make_async_copy(ypad_ref.at[pl.ds(src, SUBLANES)],
                                          bufs[new].at[pl.ds(dst, SUBLANES)],
                                          sem.at[new]).start(priority=k % 2)

        def compute():
            gates_t = _nt_dot(eye_ref[...], gate_ref[...], precision=HIGHEST)
            acc = h_ref[...]
            for k in range(TOP_K):
                yk = jnp.concatenate(
                    [bufs[old][pl.ds(k * rows + c, BLOCK, stride=SUBLANES), :]
                     for c in range(SUBLANES)], axis=1)
                acc = acc + gates_t[:, k:k + 1] * yk
            scale = lax.rsqrt(jnp.mean(acc * acc, axis=-1, keepdims=True) + EPS)
            out_ref[...] = acc * scale * nw_ref[...]

        def wait_old():
            pltpu.make_async_copy(ypad_ref.at[pl.ds(0, TOP_K * rows)], bufs[old], sem.at[old]).wait()

        @pl.when(s < COMBINE_LAG)
        def _():
            lax.fori_loop(0, n_groups, lambda g, carry: (issue_group(g), carry)[1], 0)

        @pl.when(jnp.logical_and(s >= COMBINE_LAG, s < n_steps))
        def _():
            wait_old()
            for g in range(n_groups):
                issue_group(g)
            compute()

        @pl.when(s >= n_steps)
        def _():
            wait_old()
            compute()

    for new in range(COMBINE_BUFFERS):
        pl.when(s % COMBINE_BUFFERS == new)(
            functools.partial(step, new, (new + 1) % COMBINE_BUFFERS))


def _combine(ypad, dest, gates, h1, final_norm_w, bsz, nc):
    m, d = h1.shape
    per = nc - 1
    n_steps = bsz * per
    eye = jnp.eye(BLOCK, dtype=F32)

    def tok_blk(s):
        s = jnp.clip(s, 0, n_steps - 1)
        return (s // per) * nc + 1 + s % per

    return pl.pallas_call(
        functools.partial(_combine_kernel, n_steps=n_steps),
        grid=(n_steps + COMBINE_LAG,),
        in_specs=[
            pl.BlockSpec((SUBLANES, BLOCK), lambda s: (0, tok_blk(s)), memory_space=pltpu.SMEM),
            pl.BlockSpec((SUBLANES, BLOCK), lambda s: (0, tok_blk(s - COMBINE_LAG))),
            pl.BlockSpec((BLOCK, d), lambda s: (tok_blk(s - COMBINE_LAG), 0)),
            pl.BlockSpec((1, d), lambda s: (0, 0)),
            pl.BlockSpec((BLOCK, BLOCK), lambda s: (0, 0)),
            pl.BlockSpec(memory_space=pl.ANY),
        ],
        out_specs=pl.BlockSpec((BLOCK, d), lambda s: (jnp.clip(s - COMBINE_LAG, 0, n_steps - 1), 0)),
        out_shape=jax.ShapeDtypeStruct((n_steps * BLOCK, d), F32),
        scratch_shapes=(
            [pltpu.VMEM((TOP_K * BLOCK * SUBLANES, LANES), F32) for _ in range(COMBINE_BUFFERS)]
            + [pltpu.SemaphoreType.DMA((COMBINE_BUFFERS,))]),
        compiler_params=pltpu.CompilerParams(
            dimension_semantics=("arbitrary",), vmem_limit_bytes=VMEM_LIMIT),
        name="combine",
    )(dest, gates, h1, final_norm_w[None, :], eye, ypad)


def _layer(x, lead, nc, mix_norm_w, w_in, conv_w, conv_b, dt_bias, a_log, d_skip, ssd_norm_w,
           w_ssd_out, w_att_out, w_out, ffn_norm_w, w_router, b_router, w_gate_up, b_gate_up,
           w_down, b_down):
    bsz, _, d = x.shape
    seq = nc * BLOCK
    m = bsz * seq
    n_heads = a_log.shape[0]
    d_inner = n_heads * SSD_HEADDIM
    conv_dim = d_inner + 2 * SSD_GROUPS * SSD_STATE
    att_dim = w_att_out.shape[0]
    n_exp = w_router.shape[1]

    o_xbc = d_inner
    o_dt = o_xbc + conv_dim
    o_q = o_dt + n_heads
    w_main = jnp.concatenate([w_in[:, :o_dt], w_in[:, o_q:]], axis=1).astype(BF16)
    w_dt = jnp.pad(w_in[:, o_dt:o_q], ((0, 0), (0, LANES - n_heads))).astype(BF16)
    q_col = o_dt
    k_col = q_col + att_dim
    v_col = k_col + att_dim
    gs_col = v_col + att_dim
    ga_col = gs_col + d

    proj, dt_raw, h = _inproj(x, lead, mix_norm_w[None, :], w_main, w_dt)
    yssd = _ssd(proj, dt_raw, conv_w, conv_b, dt_bias, a_log, d_skip, ssd_norm_w, bsz, nc, d_inner)
    att = _attention(proj, bsz, nc, q_col, k_col, v_col, att_dim)
    h1 = _merge(yssd, att, proj, h, w_ssd_out.astype(BF16), w_att_out.astype(BF16),
                w_out.astype(BF16), bsz, seq, gs_col, ga_col)

    ut, idx, gates, rank, cnt = _router(h1, ffn_norm_w[None, :], w_router, b_router)

    counts = cnt[:, 0].astype(jnp.int32)
    padded = (counts + MOE_BLOCK - 1) // MOE_BLOCK * MOE_BLOCK
    padded_ends = jnp.cumsum(padded)
    padded_starts = padded_ends - padded
    n_blocks = (m * TOP_K + MOE_BLOCK - 1) // MOE_BLOCK + n_exp
    dest = rank + jnp.sum(
        jnp.where(idx[None] == jnp.arange(n_exp, dtype=jnp.int32)[:, None, None],
                  padded_starts[:, None, None], 0), axis=0)
    block_first = jnp.arange(n_blocks, dtype=jnp.int32) * MOE_BLOCK
    block_expert = jnp.minimum(
        jnp.sum((padded_ends[None, :] <= block_first[:, None]).astype(jnp.int32), axis=1), n_exp - 1)
    n_used = (padded_ends[-1:] // MOE_BLOCK).astype(jnp.int32)
    zstart = (padded_starts + counts).astype(jnp.int32)

    zpad = (padded - counts).astype(jnp.int32)
    xs = _dispatch(ut, dest, zstart, zpad, n_used, n_blocks * MOE_BLOCK)
    b_gate = b_gate_up[:, None, 0::2]
    b_lin = b_gate_up[:, None, 1::2]
    experts = jnp.arange(n_exp, dtype=jnp.int32)
    nonempty = padded > 0
    order = jnp.cumsum(nonempty.astype(jnp.int32)) - 1
    later = jnp.where(nonempty[None, :] & (experts[None, :] > experts[:, None]), experts[None, :], n_exp)
    next_nonempty = jnp.min(later, axis=1)
    next_nonempty = jnp.where(next_nonempty >= n_exp, -1, next_nonempty)
    of_block = block_expert[:, None] == experts[None, :]
    block_slot = jnp.sum(jnp.where(of_block, order[None, :] % 2, 0), axis=1).astype(jnp.int32)
    block_next = jnp.sum(jnp.where(of_block, next_nonempty[None, :], 0), axis=1).astype(jnp.int32)
    ypad = _experts(xs, block_expert, n_used, block_slot, block_next, w_gate_up, b_gate, b_lin, w_down,
                    b_down[:, None, :], n_blocks)
    return h1, ypad, dest, gates


def kernel(x, meta_tokens, mix_norm_w, w_in, conv_w, conv_b, dt_bias, a_log, d_skip, ssd_norm_w,
           w_ssd_out, w_att_out, w_out, ffn_norm_w, w_router, b_router, w_gate_up, b_gate_up,
           w_down, b_down, final_norm_w):
    bsz, seq_x, d = x.shape
    depth = mix_norm_w.shape[0]
    assert depth == 1 and seq_x % BLOCK == 0 and (seq_x + N_LEAD) % ROW_TILE == 0
    nc = (seq_x + N_LEAD) // BLOCK
    lead = jnp.concatenate([jnp.zeros((N_PAD, d), x.dtype), meta_tokens.astype(x.dtype)], axis=0)
    layer = 0
    h1, ypad, dest, gates = _layer(
        x, lead, nc, mix_norm_w[layer], w_in[layer], conv_w[layer], conv_b[layer], dt_bias[layer],
        a_log[layer], d_skip[layer], ssd_norm_w[layer], w_ssd_out[layer], w_att_out[layer],
        w_out[layer], ffn_norm_w[layer], w_router[layer], b_router[layer], w_gate_up[layer],
        b_gate_up[layer], w_down[layer], b_down[layer])
    out = _combine(ypad, dest, gates, h1, final_norm_w, bsz, nc)
    return out.reshape(bsz, seq_x, d)
```

```python
import functools

import jax
import jax.numpy as jnp
from jax import lax
from jax.experimental import pallas as pl
from jax.experimental.pallas import tpu as pltpu

F32 = jnp.float32
BF16 = jnp.bfloat16
HIGHEST = lax.Precision.HIGHEST

N_META = 16
BLOCK = 128
N_LEAD = BLOCK
N_PAD = N_LEAD - N_META
EPS = 1e-5
SSD_HEADDIM = 64
SSD_GROUPS = 4
SSD_STATE = 128
CONV_K = 4
ATT_HEADDIM = 64
TOP_K = 4
SWIGLU_LIMIT = 7.0
SWIGLU_ALPHA = 1.702
MOE_BLOCK = 512

LANES = 128
SUBLANES = 8
ROW_TILE = 640
INPROJ_ROWS = 1664
DISPATCH_ROWS = 1664
COMBINE_LAG = 2
COMBINE_BUFFERS = COMBINE_LAG + 1
EXP_UNDERFLOW = -88.0
MASKED_SCORE = -1e30
VMEM_LIMIT = 56 * 1024 * 1024


def _nt_dot(a, b, precision=None):
    return lax.dot_general(a, b, (((1,), (1,)), ((), ())),
                           preferred_element_type=F32, precision=precision)


def _tn_dot(a, b):
    return lax.dot_general(a, b, (((0,), (0,)), ((), ())), preferred_element_type=F32)


def _silu(x):
    half = 0.5 * x
    return half + half * jnp.tanh(half)


def _inproj_kernel(x_ref, lead_ref, nw_ref, w_ref, wdt_ref, out_ref, dt_ref, h_ref, u_scr,
                   *, tiles_per_seq):
    tm = h_ref.shape[0]

    @pl.when(pl.program_id(1) == 0)
    def _():
        @pl.when(pl.program_id(0) % tiles_per_seq == 0)
        def _():
            h_ref[0:N_LEAD, :] = lead_ref[...]
            if tm > N_LEAD:
                h_ref[N_LEAD:, :] = x_ref[0:tm - N_LEAD, :]

        @pl.when(pl.program_id(0) % tiles_per_seq != 0)
        def _():
            h_ref[...] = x_ref[...]

        x = h_ref[...]
        u = x * lax.rsqrt(jnp.mean(x * x, axis=-1, keepdims=True) + EPS) * nw_ref[...]
        ub = u.astype(BF16)
        u_scr[...] = ub
        dt_ref[...] = jnp.dot(ub, wdt_ref[...], preferred_element_type=F32)

    out_ref[...] = jnp.dot(u_scr[...], w_ref[...], preferred_element_type=F32).astype(out_ref.dtype)


def _largest_row_tile(m, cap):
    return max(t for t in range(BLOCK, cap + 1, BLOCK) if m % t == 0)


def _inproj(x, lead, norm_w, w_main, w_dt, tn=1024):
    bsz, seq_x, d = x.shape
    seq = seq_x + N_LEAD
    m = bsz * seq
    n = w_main.shape[1]
    tm = _largest_row_tile(seq, min(INPROJ_ROWS, seq_x))
    per = seq // tm

    def x_start(i, j):
        row = (i // per) * seq_x + jnp.maximum((i % per) * tm - N_LEAD, 0)
        return (pl.multiple_of(row, BLOCK), 0)

    return pl.pallas_call(
        functools.partial(_inproj_kernel, tiles_per_seq=per),
        grid=(m // tm, n // tn),
        in_specs=[
            pl.BlockSpec((pl.Element(tm), pl.Element(d)), x_start),
            pl.BlockSpec((N_LEAD, d), lambda i, j: (0, 0)),
            pl.BlockSpec((1, d), lambda i, j: (0, 0)),
            pl.BlockSpec((d, tn), lambda i, j: (0, j)),
            pl.BlockSpec((d, LANES), lambda i, j: (0, 0)),
        ],
        out_specs=[
            pl.BlockSpec((tm, tn), lambda i, j: (i, j)),
            pl.BlockSpec((tm, LANES), lambda i, j: (i, 0)),
            pl.BlockSpec((tm, d), lambda i, j: (i, 0)),
        ],
        out_shape=[jax.ShapeDtypeStruct((m, n), BF16), jax.ShapeDtypeStruct((m, LANES), F32),
                   jax.ShapeDtypeStruct((m, d), F32)],
        scratch_shapes=[pltpu.VMEM((tm, d), BF16)],
        compiler_params=pltpu.CompilerParams(
            dimension_semantics=("parallel", "arbitrary"), vmem_limit_bytes=VMEM_LIMIT),
        name="inproj",
    )(x.reshape(bsz * seq_x, d), lead, norm_w, w_main, w_dt)


def _ssd_kernel(z_ref, xs_ref, bc_ref, dt_ref, cwx_ref, cbx_ref, cwb_ref, cbb_ref,
                dtb_ref, alog_ref, dskip_ref, nw_ref, expand_ref, tril_ref,
                out_ref, tailx_scr, tailb_scr, state_scr, *, n_heads):
    c = pl.program_id(1)
    d_inner = xs_ref.shape[1]
    gw = d_inner // SSD_GROUPS
    hpg = n_heads // SSD_GROUPS

    @pl.when(c == 0)
    def _():
        tailx_scr[...] = jnp.zeros(tailx_scr.shape, F32)
        tailb_scr[...] = jnp.zeros(tailb_scr.shape, F32)
        state_scr[...] = jnp.zeros(state_scr.shape, F32)

    row = lax.broadcasted_iota(jnp.int32, (BLOCK, 1), 0)
    vmask = (c * BLOCK + row >= N_PAD).astype(F32)
    first_row = lax.broadcasted_iota(jnp.int32, (SUBLANES, 1), 0) == 0

    def conv(x_ref, tail_scr, w_ref, b_ref):
        x = x_ref[...].astype(F32)
        tail = tail_scr[...]
        tail_scr[...] = x[BLOCK - SUBLANES:, :]
        groups = [x[r * SUBLANES:(r + 1) * SUBLANES, :] for r in range(BLOCK // SUBLANES)]
        accs = [w_ref[0:1, :] * g for g in groups]
        tacc = w_ref[0:1, :] * tail
        for i in range(1, CONV_K):
            trot = pltpu.roll(tacc, 1, axis=0)
            rots = [pltpu.roll(a, 1, axis=0) for a in accs]
            befores = [trot] + rots[:-1]
            accs = [w_ref[i:i + 1, :] * g + jnp.where(first_row, before, rot)
                    for g, before, rot in zip(groups, befores, rots)]
            tacc = w_ref[i:i + 1, :] * tail + trot
        return _silu(jnp.concatenate(accs, axis=0) + b_ref[...]) * vmask

    xs = conv(xs_ref, tailx_scr, cwx_ref, cbx_ref)
    bc = conv(bc_ref, tailb_scr, cwb_ref, cbb_ref)
    gn = SSD_GROUPS * SSD_STATE
    b_all = bc[:, :gn].astype(BF16)
    c_all = bc[:, gn:].astype(BF16)

    dt = jax.nn.softplus(dt_ref[:, :n_heads] + dtb_ref[...]) * vmask
    a = -jnp.exp(alog_ref[...]) * dt
    a_cs = jnp.dot(tril_ref[...], a, preferred_element_type=F32, precision=HIGHEST)
    a_cs_t = a_cs.T
    a_last = a_cs[BLOCK - 1:BLOCK, :]
    decay_states = jnp.exp(a_last - a_cs)
    decay_out = jnp.exp(a_cs)
    chunk_decay = jnp.broadcast_to(jnp.exp(a_last), (SUBLANES, n_heads))
    stacked = jnp.concatenate([dt, decay_states, decay_out, chunk_decay], axis=0)
    st_hi = stacked.astype(BF16)
    st_lo = (stacked - st_hi.astype(F32)).astype(BF16)
    expanded = (jnp.dot(st_hi, expand_ref[...], preferred_element_type=F32)
                + jnp.dot(st_lo, expand_ref[...], preferred_element_type=F32))
    dt_x = expanded[0:BLOCK]
    ds_x = expanded[BLOCK:2 * BLOCK]
    do_x = expanded[2 * BLOCK:3 * BLOCK]
    cd_x = expanded[3 * BLOCK:3 * BLOCK + 1]

    x_dt = xs * dt_x
    x_dt_b = x_dt.astype(BF16)
    x_ds_b = (x_dt * ds_x).astype(BF16)

    li = lax.broadcasted_iota(jnp.int32, (BLOCK, BLOCK), 0)
    si = lax.broadcasted_iota(jnp.int32, (BLOCK, BLOCK), 1)
    causal = si <= li
    first_half = lax.broadcasted_iota(jnp.int32, (BLOCK, 2 * SSD_HEADDIM), 1) < SSD_HEADDIM

    y_groups = []
    for g in range(SSD_GROUPS):
        bg = b_all[:, g * SSD_STATE:(g + 1) * SSD_STATE]
        cg = c_all[:, g * SSD_STATE:(g + 1) * SSD_STATE]
        cols = slice(g * gw, (g + 1) * gw)
        cb = _nt_dot(cg, bg)
        prev = state_scr[g]
        y_off = jnp.dot(cg, prev.astype(BF16), preferred_element_type=F32) * do_x[:, cols]
        new_state = _tn_dot(bg, x_ds_b[:, cols])
        state_scr[g] = cd_x[:, cols] * prev + new_state
        pairs = []
        for jp in range(hpg // 2):
            h0 = g * hpg + 2 * jp
            pc = slice(g * gw + jp * 2 * SSD_HEADDIM, g * gw + (jp + 1) * 2 * SSD_HEADDIM)
            xp = x_dt_b[:, pc]
            ys = []
            for h in (h0, h0 + 1):
                seg = a_cs[:, h:h + 1] - a_cs_t[h:h + 1, :]
                m = (cb * jnp.exp(jnp.where(causal, seg, -jnp.inf))).astype(BF16)
                ys.append(jnp.dot(m, xp, preferred_element_type=F32))
            pairs.append(jnp.where(first_half, ys[0], ys[1]))
        y_groups.append(jnp.concatenate(pairs, axis=1) + y_off)
    y = jnp.concatenate(y_groups, axis=1) + dskip_ref[...] * xs

    gated = y * _silu(z_ref[...].astype(F32))
    outs = []
    for g in range(SSD_GROUPS):
        gg = gated[:, g * gw:(g + 1) * gw]
        outs.append(gg * lax.rsqrt(jnp.mean(gg * gg, axis=-1, keepdims=True) + EPS))
    out_ref[...] = (jnp.concatenate(outs, axis=1) * nw_ref[...]).astype(out_ref.dtype)


def _ssd(proj, dt_raw, conv_w, conv_b, dt_bias, a_log, d_skip, norm_w, bsz, nc, d_inner):
    m = proj.shape[0]
    n_heads = a_log.shape[0]
    bc_w = 2 * SSD_GROUPS * SSD_STATE
    gw = d_inner // SSD_GROUPS
    cwx, cwb = conv_w[:, :d_inner], conv_w[:, d_inner:]
    cbx, cbb = conv_b[None, :d_inner], conv_b[None, d_inner:]
    expand = jnp.repeat(jnp.eye(n_heads, dtype=BF16), SSD_HEADDIM, axis=1)
    tril = jnp.tril(jnp.ones((BLOCK, BLOCK), F32))
    dskip_x = jnp.repeat(d_skip.astype(F32), SSD_HEADDIM)[None, :]
    full = lambda shape: pl.BlockSpec(shape, lambda b, c: (0,) * len(shape))
    row_blk = lambda b, c: b * nc + c
    return pl.pallas_call(
        functools.partial(_ssd_kernel, n_heads=n_heads),
        grid=(bsz, nc),
        in_specs=[
            pl.BlockSpec((BLOCK, d_inner), lambda b, c: (row_blk(b, c), 0)),
            pl.BlockSpec((BLOCK, d_inner), lambda b, c: (row_blk(b, c), 1)),
            pl.BlockSpec((BLOCK, bc_w), lambda b, c: (row_blk(b, c), 2 * d_inner // bc_w)),
            pl.BlockSpec((BLOCK, LANES), lambda b, c: (row_blk(b, c), 0)),
            full((CONV_K, d_inner)), full((1, d_inner)), full((CONV_K, bc_w)), full((1, bc_w)),
            full((1, n_heads)), full((1, n_heads)), full((1, d_inner)), full((1, d_inner)),
            full((n_heads, d_inner)), full((BLOCK, BLOCK)),
        ],
        out_specs=pl.BlockSpec((BLOCK, d_inner), lambda b, c: (row_blk(b, c), 0)),
        out_shape=jax.ShapeDtypeStruct((m, d_inner), BF16),
        scratch_shapes=[
            pltpu.VMEM((SUBLANES, d_inner), F32),
            pltpu.VMEM((SUBLANES, bc_w), F32),
            pltpu.VMEM((SSD_GROUPS, SSD_STATE, gw), F32),
        ],
        compiler_params=pltpu.CompilerParams(
            dimension_semantics=("parallel", "arbitrary"), vmem_limit_bytes=VMEM_LIMIT),
        name="ssd",
    )(proj, proj, proj, dt_raw, cwx, cbx, cwb, cbb, dt_bias[None, :].astype(F32),
      a_log[None, :].astype(F32), dskip_x, norm_w[None, :].astype(F32), expand, tril)


def _attn_kernel(q_ref, k_ref, v_ref, upper_ref, out_ref, *, n_pairs):
    i = pl.program_id(2)
    pw = 2 * ATT_HEADDIM
    lane = lax.broadcasted_iota(jnp.int32, (BLOCK, pw), 1)
    first = lane < ATT_HEADDIM
    row2 = lax.broadcasted_iota(jnp.int32, (2 * BLOCK, BLOCK), 0)
    qpos = i * BLOCK + jnp.where(row2 < BLOCK, row2, row2 - BLOCK)
    kin = lax.broadcasted_iota(jnp.int32, (2 * BLOCK, BLOCK), 1)
    upper = upper_ref[...]
    scale = ATT_HEADDIM ** -0.5
    q2 = []
    for p in range(n_pairs):
        q = q_ref[:, p * pw:(p + 1) * pw] * jnp.asarray(scale, BF16)
        zero = jnp.zeros_like(q)
        q2.append(jnp.concatenate([jnp.where(first, q, zero), jnp.where(first, zero, q)], axis=0))

    def process(kbs, runs, accs):
        starts = [pl.multiple_of(jnp.maximum(kb, 0) * BLOCK, BLOCK) for kb in kbs]
        alloweds = []
        for kb in kbs:
            kpos = kb * BLOCK + kin
            alloweds.append(jnp.logical_and(kpos < qpos, kpos >= N_PAD))
        zs = [[_nt_dot(q2[p], k_ref[pl.ds(start, BLOCK), p * pw:(p + 1) * pw])
               for p in range(n_pairs)] for start in starts]
        log_betas, laters, sums = [], [], []
        for j in range(len(kbs)):
            log_betas.append([])
            laters.append([])
            sums.append([])
            for p in range(n_pairs):
                z = jnp.where(alloweds[j], zs[j][p], MASKED_SCORE)
                l1p = jnp.log(1.0 + jnp.exp(-jnp.abs(z)))
                log_beta = jnp.minimum(z, 0.0) - l1p
                log_stay = log_beta - z
                hi = log_stay.astype(BF16)
                lo = (log_stay - hi.astype(F32)).astype(BF16)
                log_betas[j].append(log_beta)
                laters[j].append(jnp.dot(hi, upper, preferred_element_type=F32)
                                 + jnp.dot(lo, upper, preferred_element_type=F32))
                sums[j].append(jnp.sum(log_stay, axis=1, keepdims=True))
        runs, accs = list(runs), list(accs)
        for j, start in enumerate(starts):
            for p in range(n_pairs):
                vblk = v_ref[pl.ds(start, BLOCK), p * pw:(p + 1) * pw]
                w = jnp.exp(log_betas[j][p] + laters[j][p] + runs[p]).astype(BF16)
                vzero = jnp.zeros_like(vblk)
                accs[p] = (accs[p]
                           + jnp.dot(w[:BLOCK], jnp.where(first, vblk, vzero),
                                     preferred_element_type=F32)
                           + jnp.dot(w[BLOCK:], jnp.where(first, vzero, vblk),
                                     preferred_element_type=F32))
                runs[p] = runs[p] + sums[j][p]
        top = jnp.max(runs[0])
        for r in runs[1:]:
            top = jnp.maximum(top, jnp.max(r))
        return top, runs, accs

    def cond(carry):
        kb, top = carry[0], carry[1]
        return jnp.logical_and(kb >= 0, top > EXP_UNDERFLOW)

    def body(carry):
        kb = carry[0]
        top, runs, accs = process([kb], carry[2:2 + n_pairs], carry[2 + n_pairs:])
        return (kb - 1, top, *runs, *accs)

    zero_runs = [jnp.zeros((2 * BLOCK, 1), F32) for _ in range(n_pairs)]
    zero_accs = [jnp.zeros((BLOCK, pw), F32) for _ in range(n_pairs)]
    top, runs, accs = process([i, i - 1], zero_runs, zero_accs)
    final = lax.while_loop(cond, body, (i - 2, top, *runs, *accs))
    for p in range(n_pairs):
        out_ref[:, p * pw:(p + 1) * pw] = final[2 + n_pairs + p].astype(out_ref.dtype)


def _attention(proj, bsz, nc, q_col, k_col, v_col, att_dim, pairs_per_step=4):
    m = proj.shape[0]
    seq = nc * BLOCK
    sw = pairs_per_step * 2 * ATT_HEADDIM
    upper = jnp.triu(jnp.ones((BLOCK, BLOCK), F32), 1).T.astype(BF16)
    return pl.pallas_call(
        functools.partial(_attn_kernel, n_pairs=pairs_per_step),
        grid=(bsz, att_dim // sw, nc),
        in_specs=[
            pl.BlockSpec((BLOCK, sw), lambda b, p, i: (b * nc + i, q_col // sw + p)),
            pl.BlockSpec((seq, sw), lambda b, p, i: (b, k_col // sw + p)),
            pl.BlockSpec((seq, sw), lambda b, p, i: (b, v_col // sw + p)),
            pl.BlockSpec((BLOCK, BLOCK), lambda b, p, i: (0, 0)),
        ],
        out_specs=pl.BlockSpec((BLOCK, sw), lambda b, p, i: (b * nc + i, p)),
        out_shape=jax.ShapeDtypeStruct((m, att_dim), BF16),
        compiler_params=pltpu.CompilerParams(
            dimension_semantics=("parallel", "parallel", "arbitrary"),
            vmem_limit_bytes=VMEM_LIMIT),
        name="attention",
    )(proj, proj, proj, upper)


def _merge_kernel(yssd_ref, att_ref, gs_ref, ga_ref, h_ref, wso_ref, wao_ref, wo_ref, out_ref):
    t = pl.program_id(1)
    y_ssd = jnp.dot(yssd_ref[...], wso_ref[...], preferred_element_type=F32)
    y_att = jnp.dot(att_ref[...], wao_ref[...], preferred_element_type=F32)
    merged = (jax.nn.sigmoid(gs_ref[...].astype(F32)) * y_ssd
              + jax.nn.sigmoid(ga_ref[...].astype(F32)) * y_att)
    mixed = jnp.dot(merged.astype(BF16), wo_ref[...], preferred_element_type=F32)
    row = lax.broadcasted_iota(jnp.int32, (ROW_TILE, 1), 0)
    valid = (t * ROW_TILE + row >= N_PAD).astype(F32)
    out_ref[...] = h_ref[...] + mixed * valid


def _merge(yssd, att, proj, h, w_ssd_out, w_att_out, w_out, bsz, seq, gs_col, ga_col):
    m, d = h.shape
    d_inner = yssd.shape[1]
    att_dim = att.shape[1]
    per = seq // ROW_TILE
    rb = lambda b, t: b * per + t
    full = lambda shape: pl.BlockSpec(shape, lambda b, t: (0,) * len(shape))
    return pl.pallas_call(
        _merge_kernel,
        grid=(bsz, per),
        in_specs=[
            pl.BlockSpec((ROW_TILE, d_inner), lambda b, t: (rb(b, t), 0)),
            pl.BlockSpec((ROW_TILE, att_dim), lambda b, t: (rb(b, t), 0)),
            pl.BlockSpec((ROW_TILE, d), lambda b, t: (rb(b, t), gs_col // d)),
            pl.BlockSpec((ROW_TILE, d), lambda b, t: (rb(b, t), ga_col // d)),
            pl.BlockSpec((ROW_TILE, d), lambda b, t: (rb(b, t), 0)),
            full((d_inner, d)), full((att_dim, d)), full((d, d)),
        ],
        out_specs=pl.BlockSpec((ROW_TILE, d), lambda b, t: (rb(b, t), 0)),
        out_shape=jax.ShapeDtypeStruct((m, d), F32),
        compiler_params=pltpu.CompilerParams(
            dimension_semantics=("parallel", "parallel"), vmem_limit_bytes=VMEM_LIMIT),
        name="merge",
    )(yssd, att, proj, proj, h, w_ssd_out, w_att_out, w_out)


def _router_kernel(h_ref, nw_ref, wrh_ref, wrl_ref, br_ref, before_ref,
                   ut_ref, idx_ref, gate_ref, rank_ref, cnt_ref, base_scr):
    step = pl.program_id(0)
    n_exp = br_ref.shape[0]
    tm = h_ref.shape[0]

    @pl.when(step == 0)
    def _():
        base_scr[...] = jnp.zeros(base_scr.shape, F32)

    x = h_ref[...]
    u = x * lax.rsqrt(jnp.mean(x * x, axis=-1, keepdims=True) + EPS) * nw_ref[...]
    for c in range(SUBLANES):
        ut_ref[pl.ds(c, tm, stride=SUBLANES), :] = u[:, c * LANES:(c + 1) * LANES]

    u_hi = u.astype(BF16)
    u_lo = (u - u_hi.astype(F32)).astype(BF16)
    logits_tm = (jnp.dot(u_hi, wrh_ref[...], preferred_element_type=F32)
                 + jnp.dot(u_lo, wrh_ref[...], preferred_element_type=F32)
                 + jnp.dot(u_hi, wrl_ref[...], preferred_element_type=F32))
    logits = logits_tm.T[:n_exp] + br_ref[...]
    eidx = lax.broadcasted_iota(jnp.int32, (n_exp, tm), 0)
    work = logits
    tops, idxs, onehots = [], [], []
    for _ in range(TOP_K):
        top = jnp.max(work, axis=0, keepdims=True)
        idx = jnp.min(jnp.where(work == top, eidx, n_exp), axis=0, keepdims=True)
        hot = eidx == idx
        work = jnp.where(hot, -jnp.inf, work)
        tops.append(top)
        idxs.append(idx)
        onehots.append(hot.astype(F32))
    exps = [jnp.exp(v - tops[0]) for v in tops]
    denom = exps[0] + exps[1] + exps[2] + exps[3]
    cnt = onehots[0] + onehots[1] + onehots[2] + onehots[3]
    before = jnp.dot(cnt.astype(BF16), before_ref[...], preferred_element_type=F32) + base_scr[...]
    ranks = [jnp.sum(hot * before, axis=0, keepdims=True) for hot in onehots]
    pad_rows = SUBLANES - TOP_K
    idx_ref[...] = jnp.concatenate(idxs + [jnp.zeros((pad_rows, tm), jnp.int32)], axis=0)
    gate_ref[...] = jnp.concatenate([e / denom for e in exps] + [jnp.zeros((pad_rows, tm), F32)], axis=0)
    rank_ref[...] = jnp.concatenate(
        [r.astype(jnp.int32) for r in ranks] + [jnp.zeros((pad_rows, tm), jnp.int32)], axis=0)
    base_scr[...] = base_scr[...] + jnp.sum(cnt, axis=1, keepdims=True)
    cnt_ref[...] = jnp.broadcast_to(base_scr[...], cnt_ref.shape)


def _router(h1, norm_w, w_router, b_router):
    m, d = h1.shape
    n_exp = w_router.shape[1]
    before = jnp.triu(jnp.ones((ROW_TILE, ROW_TILE), F32), 1).astype(BF16)
    wr = jnp.pad(w_router.astype(F32), ((0, 0), (0, LANES - n_exp)))
    wr_hi = wr.astype(BF16)
    wr_lo = (wr - wr_hi.astype(F32)).astype(BF16)
    full = lambda shape: pl.BlockSpec(shape, lambda i: (0,) * len(shape))
    return pl.pallas_call(
        _router_kernel,
        grid=(m // ROW_TILE,),
        in_specs=[
            pl.BlockSpec((ROW_TILE, d), lambda i: (i, 0)),
            full((1, d)), full((d, LANES)), full((d, LANES)), full((n_exp, 1)),
            full((ROW_TILE, ROW_TILE)),
        ],
        out_specs=[
            pl.BlockSpec((ROW_TILE * SUBLANES, LANES), lambda i: (i, 0)),
            pl.BlockSpec((SUBLANES, ROW_TILE), lambda i: (0, i)),
            pl.BlockSpec((SUBLANES, ROW_TILE), lambda i: (0, i)),
            pl.BlockSpec((SUBLANES, ROW_TILE), lambda i: (0, i)),
            full((n_exp, LANES)),
        ],
        out_shape=[
            jax.ShapeDtypeStruct((m * SUBLANES, LANES), F32),
            jax.ShapeDtypeStruct((SUBLANES, m), jnp.int32),
            jax.ShapeDtypeStruct((SUBLANES, m), F32),
            jax.ShapeDtypeStruct((SUBLANES, m), jnp.int32),
            jax.ShapeDtypeStruct((n_exp, LANES), F32),
        ],
        scratch_shapes=[pltpu.VMEM((n_exp, 1), F32)],
        compiler_params=pltpu.CompilerParams(
            dimension_semantics=("arbitrary",), vmem_limit_bytes=VMEM_LIMIT),
        name="router",
    )(h1, norm_w, wr_hi, wr_lo, b_router[:, None], before)


def _dispatch_kernel(zstart_ref, zpad_ref, nused_ref, dest_ref, ut_ref, xs_ref, zero_scr, sem, zsem,
                     *, n_exp):
    step = pl.program_id(0)
    tb = dest_ref.shape[1]
    slot_rows = MOE_BLOCK * SUBLANES
    n_blocks = xs_ref.shape[0] // slot_rows

    @pl.when(step == 0)
    def _():
        zero_scr[...] = jnp.zeros(zero_scr.shape, F32)

        def zero_rows(row_start, n_rows):
            return pltpu.make_async_copy(zero_scr.at[pl.ds(0, n_rows)],
                                         xs_ref.at[pl.ds(row_start, n_rows)], zsem)

        def pad_fill(wait):
            def per_expert(e, carry):
                n_pad = zpad_ref[e]
                pos = zstart_ref[e]
                size = MOE_BLOCK // 2
                while size >= 1:
                    take = n_pad & size

                    @pl.when(take != 0)
                    def _(pos=pos, size=size):
                        copy = zero_rows(pl.multiple_of(pos * SUBLANES, SUBLANES), size * SUBLANES)
                        copy.wait() if wait else copy.start()

                    pos = pos + take
                    size //= 2
                return carry

            lax.fori_loop(0, n_exp, per_expert, 0)

        def tail_fill(wait):
            def per_block(b, carry):
                copy = zero_rows(pl.multiple_of(b * slot_rows, slot_rows), slot_rows)
                copy.wait() if wait else copy.start()
                return carry

            lax.fori_loop(nused_ref[0], n_blocks, per_block, 0)

        pad_fill(False)
        tail_fill(False)
        pad_fill(True)
        tail_fill(True)

    def issue(t, carry):
        src = pl.multiple_of(t * SUBLANES, SUBLANES)
        for k in range(TOP_K):
            dst = pl.multiple_of(dest_ref[k, t] * SUBLANES, SUBLANES)
            pltpu.make_async_copy(ut_ref.at[pl.ds(src, SUBLANES)],
                                  xs_ref.at[pl.ds(dst, SUBLANES)], sem).start(priority=k % 2)
        return carry

    lax.fori_loop(0, tb, issue, 0, unroll=4)
    for _ in range(TOP_K):
        pltpu.make_async_copy(ut_ref, xs_ref.at[pl.ds(0, tb * SUBLANES)], sem).wait()


def _dispatch(ut, dest, zstart, zpad, n_used, n_slots):
    m = dest.shape[1]
    n_exp = zstart.shape[0]
    tb = _largest_row_tile(m, DISPATCH_ROWS)
    return pl.pallas_call(
        functools.partial(_dispatch_kernel, n_exp=n_exp),
        grid_spec=pltpu.PrefetchScalarGridSpec(
            num_scalar_prefetch=3,
            grid=(m // tb,),
            in_specs=[
                pl.BlockSpec((SUBLANES, tb), lambda i, *_: (0, i), memory_space=pltpu.SMEM),
                pl.BlockSpec((tb * SUBLANES, LANES), lambda i, *_: (i, 0)),
            ],
            out_specs=pl.BlockSpec(memory_space=pl.ANY),
            scratch_shapes=[
                pltpu.VMEM((MOE_BLOCK * SUBLANES, LANES), F32),
                pltpu.SemaphoreType.DMA(()),
                pltpu.SemaphoreType.DMA(()),
            ],
        ),
        out_shape=jax.ShapeDtypeStruct((n_slots * SUBLANES, LANES), F32),
        compiler_params=pltpu.CompilerParams(
            dimension_semantics=("arbitrary",), vmem_limit_bytes=VMEM_LIMIT),
        name="dispatch",
    )(zstart, zpad, n_used, dest, ut)


def _expert_kernel(be_ref, nused_ref, slot_ref, next_ref, xs_ref, wgu_hbm, bg_ref, bl_ref, wd_hbm,
                   bd_ref, perm_ref, y_ref, wgu_buf, wd_buf, wg_scr, wl_scr, wd_scr, sem):
    b = pl.program_id(0)
    pair = 2 * LANES
    changed = jnp.logical_or(b == 0, be_ref[b] != be_ref[jnp.maximum(b - 1, 0)])

    def fetch(expert, slot):
        return (pltpu.make_async_copy(wgu_hbm.at[expert], wgu_buf.at[slot], sem.at[0, slot]),
                pltpu.make_async_copy(wd_hbm.at[expert], wd_buf.at[slot], sem.at[1, slot]))

    @pl.when(b == 0)
    def _():
        for copy in fetch(be_ref[0], slot_ref[0]):
            copy.start()

    @pl.when(jnp.logical_and(changed, b < nused_ref[0]))
    def _():
        slot = slot_ref[b]
        for copy in fetch(be_ref[b], slot):
            copy.wait()

        @pl.when(next_ref[b] >= 0)
        def _():
            for copy in fetch(next_ref[b], 1 - slot):
                copy.start()

        for cb in range(wgu_buf.shape[2] // pair):
            wb = wgu_buf[slot, :, cb * pair:(cb + 1) * pair].astype(BF16)
            sep = jnp.dot(wb, perm_ref[...], preferred_element_type=F32).astype(BF16)
            wg_scr[:, cb * LANES:(cb + 1) * LANES] = sep[:, :LANES]
            wl_scr[:, cb * LANES:(cb + 1) * LANES] = sep[:, LANES:]
        wd_scr[...] = wd_buf[slot].astype(BF16)

    @pl.when(b < nused_ref[0])
    def _():
        x = jnp.concatenate(
            [xs_ref[pl.ds(c, MOE_BLOCK, stride=SUBLANES), :] for c in range(SUBLANES)],
            axis=1).astype(BF16)
        gate = jnp.dot(x, wg_scr[...], preferred_element_type=F32) + bg_ref[0]
        lin = jnp.dot(x, wl_scr[...], preferred_element_type=F32) + bl_ref[0]
        gate = jnp.minimum(gate, SWIGLU_LIMIT)
        lin = jnp.clip(lin, -SWIGLU_LIMIT, SWIGLU_LIMIT)
        act = gate * jax.nn.sigmoid(SWIGLU_ALPHA * gate) * (lin + 1.0)
        y = jnp.dot(act.astype(BF16), wd_scr[...], preferred_element_type=F32) + bd_ref[0]
        for c in range(SUBLANES):
            y_ref[pl.ds(c, MOE_BLOCK, stride=SUBLANES), :] = y[:, c * LANES:(c + 1) * LANES]

    @pl.when(b >= nused_ref[0])
    def _():
        y_ref[...] = jnp.zeros(y_ref.shape, F32)


def _experts(xs, block_expert, n_used, block_slot, block_next, w_gate_up, b_gate, b_lin, w_down,
             b_down, n_blocks):
    n_exp, d, d_ff2 = w_gate_up.shape
    d_ff = d_ff2 // 2
    rows = MOE_BLOCK * SUBLANES
    blk = lambda b, be, nu, *_: (jnp.minimum(b, nu[0] - 1), 0)
    wsel = lambda b, be, *_: (be[b], 0, 0)
    r = jnp.arange(2 * LANES)[:, None]
    c = jnp.arange(2 * LANES)[None, :]
    perm = (r == jnp.where(c < LANES, 2 * c, 2 * (c - LANES) + 1)).astype(BF16)
    return pl.pallas_call(
        _expert_kernel,
        grid_spec=pltpu.PrefetchScalarGridSpec(
            num_scalar_prefetch=4,
            grid=(n_blocks,),
            in_specs=[
                pl.BlockSpec((rows, LANES), blk),
                pl.BlockSpec(memory_space=pl.ANY),
                pl.BlockSpec((1, 1, d_ff), wsel), pl.BlockSpec((1, 1, d_ff), wsel),
                pl.BlockSpec(memory_space=pl.ANY), pl.BlockSpec((1, 1, d), wsel),
                pl.BlockSpec((2 * LANES, 2 * LANES), lambda b, *_: (0, 0)),
            ],
            out_specs=pl.BlockSpec((rows, LANES), lambda b, *_: (b, 0)),
            scratch_shapes=[
                pltpu.VMEM((2, d, d_ff2), F32), pltpu.VMEM((2, d_ff, d), F32),
                pltpu.VMEM((d, d_ff), BF16), pltpu.VMEM((d, d_ff), BF16), pltpu.VMEM((d_ff, d), BF16),
                pltpu.SemaphoreType.DMA((2, 2)),
            ],
        ),
        out_shape=jax.ShapeDtypeStruct((n_blocks * rows, LANES), F32),
        compiler_params=pltpu.CompilerParams(
            dimension_semantics=("arbitrary",), vmem_limit_bytes=VMEM_LIMIT),
        name="experts",
    )(block_expert, n_used, block_slot, block_next, xs, w_gate_up, b_gate, b_lin, w_down, b_down,
      perm)


def _combine_kernel(dest_ref, gate_ref, h_ref, nw_ref, eye_ref, ypad_ref, out_ref, *scratch, n_steps):
    bufs = scratch[:COMBINE_BUFFERS]
    sem = scratch[COMBINE_BUFFERS]
    s = pl.program_id(0)
    rows = BLOCK * SUBLANES
    n_groups = BLOCK // SUBLANES

    def step(new, old):
        def issue_group(g):
            for j in range(SUBLANES):
                t = g * SUBLANES + j
                for k in range(TOP_K):
                    src = pl.multiple_of(dest_ref[k, t] * SUBLANES, SUBLANES)
                    dst = pl.multiple_of(k * rows + t * SUBLANES, SUBLANES)
                    pltpu.make_async_copy(ypad_ref.at[pl.ds(src, SUBLANES)],
                                          bufs[new].at[pl.ds(dst, SUBLANES)],
                                          sem.at[new]).start(priority=k % 2)

        def compute():
            gates_t = _nt_dot(eye_ref[...], gate_ref[...], precision=HIGHEST)
            acc = h_ref[...]
            for k in range(TOP_K):
                yk = jnp.concatenate(
                    [bufs[old][pl.ds(k * rows + c, BLOCK, stride=SUBLANES), :]
                     for c in range(SUBLANES)], axis=1)
                acc = acc + gates_t[:, k:k + 1] * yk
            scale = lax.rsqrt(jnp.mean(acc * acc, axis=-1, keepdims=True) + EPS)
            out_ref[...] = acc * scale * nw_ref[...]

        def wait_old():
            pltpu.make_async_copy(ypad_ref.at[pl.ds(0, TOP_K * rows)], bufs[old], sem.at[old]).wait()

        @pl.when(s < COMBINE_LAG)
        def _():
            lax.fori_loop(0, n_groups, lambda g, carry: (issue_group(g), carry)[1], 0)

        @pl.when(jnp.logical_and(s >= COMBINE_LAG, s < n_steps))
        def _():
            wait_old()
            for g in range(n_groups):
                issue_group(g)
            compute()

        @pl.when(s >= n_steps)
        def _():
            wait_old()
            compute()

    for new in range(COMBINE_BUFFERS):
        pl.when(s % COMBINE_BUFFERS == new)(
            functools.partial(step, new, (new + 1) % COMBINE_BUFFERS))


def _combine(ypad, dest, gates, h1, final_norm_w, bsz, nc):
    m, d = h1.shape
    per = nc - 1
    n_steps = bsz * per
    eye = jnp.eye(BLOCK, dtype=F32)

    def tok_blk(s):
        s = jnp.clip(s, 0, n_steps - 1)
        return (s // per) * nc + 1 + s % per

    return pl.pallas_call(
        functools.partial(_combine_kernel, n_steps=n_steps),
        grid=(n_steps + COMBINE_LAG,),
        in_specs=[
            pl.BlockSpec((SUBLANES, BLOCK), lambda s: (0, tok_blk(s)), memory_space=pltpu.SMEM),
            pl.BlockSpec((SUBLANES, BLOCK), lambda s: (0, tok_blk(s - COMBINE_LAG))),
            pl.BlockSpec((BLOCK, d), lambda s: (tok_blk(s - COMBINE_LAG), 0)),
            pl.BlockSpec((1, d), lambda s: (0, 0)),
            pl.BlockSpec((BLOCK, BLOCK), lambda s: (0, 0)),
            pl.BlockSpec(memory_space=pl.ANY),
        ],
        out_specs=pl.BlockSpec((BLOCK, d), lambda s: (jnp.clip(s - COMBINE_LAG, 0, n_steps - 1), 0)),
        out_shape=jax.ShapeDtypeStruct((n_steps * BLOCK, d), F32),
        scratch_shapes=(
            [pltpu.VMEM((TOP_K * BLOCK * SUBLANES, LANES), F32) for _ in range(COMBINE_BUFFERS)]
            + [pltpu.SemaphoreType.DMA((COMBINE_BUFFERS,))]),
        compiler_params=pltpu.CompilerParams(
            dimension_semantics=("arbitrary",), vmem_limit_bytes=VMEM_LIMIT),
        name="combine",
    )(dest, gates, h1, final_norm_w[None, :], eye, ypad)


def _layer(x, lead, nc, mix_norm_w, w_in, conv_w, conv_b, dt_bias, a_log, d_skip, ssd_norm_w,
           w_ssd_out, w_att_out, w_out, ffn_norm_w, w_router, b_router, w_gate_up, b_gate_up,
           w_down, b_down):
    bsz, _, d = x.shape
    seq = nc * BLOCK
    m = bsz * seq
    n_heads = a_log.shape[0]
    d_inner = n_heads * SSD_HEADDIM
    conv_dim = d_inner + 2 * SSD_GROUPS * SSD_STATE
    att_dim = w_att_out.shape[0]
    n_exp = w_router.shape[1]

    o_xbc = d_inner
    o_dt = o_xbc + conv_dim
    o_q = o_dt + n_heads
    w_main = jnp.concatenate([w_in[:, :o_dt], w_in[:, o_q:]], axis=1).astype(BF16)
    w_dt = jnp.pad(w_in[:, o_dt:o_q], ((0, 0), (0, LANES - n_heads))).astype(BF16)
    q_col = o_dt
    k_col = q_col + att_dim
    v_col = k_col + att_dim
    gs_col = v_col + att_dim
    ga_col = gs_col + d

    proj, dt_raw, h = _inproj(x, lead, mix_norm_w[None, :], w_main, w_dt)
    yssd = _ssd(proj, dt_raw, conv_w, conv_b, dt_bias, a_log, d_skip, ssd_norm_w, bsz, nc, d_inner)
    att = _attention(proj, bsz, nc, q_col, k_col, v_col, att_dim)
    h1 = _merge(yssd, att, proj, h, w_ssd_out.astype(BF16), w_att_out.astype(BF16),
                w_out.astype(BF16), bsz, seq, gs_col, ga_col)

    ut, idx, gates, rank, cnt = _router(h1, ffn_norm_w[None, :], w_router, b_router)

    counts = cnt[:, 0].astype(jnp.int32)
    padded = (counts + MOE_BLOCK - 1) // MOE_BLOCK * MOE_BLOCK
    padded_ends = jnp.cumsum(padded)
    padded_starts = padded_ends - padded
    n_blocks = (m * TOP_K + MOE_BLOCK - 1) // MOE_BLOCK + n_exp
    dest = rank + jnp.sum(
        jnp.where(idx[None] == jnp.arange(n_exp, dtype=jnp.int32)[:, None, None],
                  padded_starts[:, None, None], 0), axis=0)
    block_first = jnp.arange(n_blocks, dtype=jnp.int32) * MOE_BLOCK
    block_expert = jnp.minimum(
        jnp.sum((padded_ends[None, :] <= block_first[:, None]).astype(jnp.int32), axis=1), n_exp - 1)
    n_used = (padded_ends[-1:] // MOE_BLOCK).astype(jnp.int32)
    zstart = (padded_starts + counts).astype(jnp.int32)

    zpad = (padded - counts).astype(jnp.int32)
    xs = _dispatch(ut, dest, zstart, zpad, n_used, n_blocks * MOE_BLOCK)
    b_gate = b_gate_up[:, None, 0::2]
    b_lin = b_gate_up[:, None, 1::2]
    experts = jnp.arange(n_exp, dtype=jnp.int32)
    nonempty = padded > 0
    order = jnp.cumsum(nonempty.astype(jnp.int32)) - 1
    later = jnp.where(nonempty[None, :] & (experts[None, :] > experts[:, None]), experts[None, :], n_exp)
    next_nonempty = jnp.min(later, axis=1)
    next_nonempty = jnp.where(next_nonempty >= n_exp, -1, next_nonempty)
    of_block = block_expert[:, None] == experts[None, :]
    block_slot = jnp.sum(jnp.where(of_block, order[None, :] % 2, 0), axis=1).astype(jnp.int32)
    block_next = jnp.sum(jnp.where(of_block, next_nonempty[None, :], 0), axis=1).astype(jnp.int32)
    ypad = _experts(xs, block_expert, n_used, block_slot, block_next, w_gate_up, b_gate, b_lin, w_down,
                    b_down[:, None, :], n_blocks)
    return h1, ypad, dest, gates


def kernel(x, meta_tokens, mix_norm_w, w_in, conv_w, conv_b, dt_bias, a_log, d_skip, ssd_norm_w,
           w_ssd_out, w_att_out, w_out, ffn_norm_w, w_router, b_router, w_gate_up, b_gate_up,
           w_down, b_down, final_norm_w):
    bsz, seq_x, d = x.shape
    depth = mix_norm_w.shape[0]
    assert depth == 1 and seq_x % BLOCK == 0 and (seq_x + N_LEAD) % ROW_TILE == 0
    nc = (seq_x + N_LEAD) // BLOCK
    lead = jnp.concatenate([jnp.zeros((N_PAD, d), x.dtype), meta_tokens.astype(x.dtype)], axis=0)
    layer = 0
    h1, ypad, dest, gates = _layer(
        x, lead, nc, mix_norm_w[layer], w_in[layer], conv_w[layer], conv_b[layer], dt_bias[layer],
        a_log[layer], d_skip[layer], ssd_norm_w[layer], w_ssd_out[layer], w_att_out[layer],
        w_out[layer], ffn_norm_w[layer], w_router[layer], b_router[layer], w_gate_up[layer],
        b_gate_up[layer], w_down[layer], b_down[layer])
    out = _combine(ypad, dest, gates, h1, final_norm_w, bsz, nc)
    return out.reshape(bsz, seq_x, d)
```

```python
import functools

import jax
import jax.numpy as jnp
from jax import lax
from jax.experimental import pallas as pl
from jax.experimental.pallas import tpu as pltpu

F32 = jnp.float32
BF16 = jnp.bfloat16
HIGHEST = lax.Precision.HIGHEST

N_META = 16
BLOCK = 128
N_LEAD = BLOCK
N_PAD = N_LEAD - N_META
EPS = 1e-5
SSD_HEADDIM = 64
SSD_GROUPS = 4
SSD_STATE = 128
CONV_K = 4
ATT_HEADDIM = 64
TOP_K = 4
SWIGLU_LIMIT = 7.0
SWIGLU_ALPHA = 1.702
MOE_BLOCK = 512

LANES = 128
SUBLANES = 8
ROW_TILE = 640
INPROJ_ROWS = 1664
DISPATCH_ROWS = 1664
SSD_CHUNKS_PER_STEP = 5
ATT_Q_BLOCKS_PER_STEP = 5
COMBINE_LAG = 2
COMBINE_BUFFERS = COMBINE_LAG + 1
EXP_UNDERFLOW = -88.0
MASKED_SCORE = -1e30
VMEM_LIMIT = 56 * 1024 * 1024


def _nt_dot(a, b, precision=None):
    return lax.dot_general(a, b, (((1,), (1,)), ((), ())),
                           preferred_element_type=F32, precision=precision)


def _tn_dot(a, b):
    return lax.dot_general(a, b, (((0,), (0,)), ((), ())), preferred_element_type=F32)


def _silu(x):
    half = 0.5 * x
    return half + half * jnp.tanh(half)


def _inproj_kernel(x_ref, lead_ref, nw_ref, w_ref, wdt_ref, out_ref, dt_ref, h_ref, u_scr,
                   *, tiles_per_seq):
    tm = h_ref.shape[0]

    @pl.when(pl.program_id(1) == 0)
    def _():
        @pl.when(pl.program_id(0) % tiles_per_seq == 0)
        def _():
            h_ref[0:N_LEAD, :] = lead_ref[...]
            if tm > N_LEAD:
                h_ref[N_LEAD:, :] = x_ref[0:tm - N_LEAD, :]

        @pl.when(pl.program_id(0) % tiles_per_seq != 0)
        def _():
            h_ref[...] = x_ref[...]

        x = h_ref[...]
        u = x * lax.rsqrt(jnp.mean(x * x, axis=-1, keepdims=True) + EPS) * nw_ref[...]
        ub = u.astype(BF16)
        u_scr[...] = ub
        dt_ref[...] = jnp.dot(ub, wdt_ref[...], preferred_element_type=F32)

    out_ref[...] = jnp.dot(u_scr[...], w_ref[...], preferred_element_type=F32).astype(out_ref.dtype)


def _largest_row_tile(m, cap):
    return max(t for t in range(BLOCK, cap + 1, BLOCK) if m % t == 0)


def _inproj(x, lead, norm_w, w_main, w_dt, tn=1024):
    bsz, seq_x, d = x.shape
    seq = seq_x + N_LEAD
    m = bsz * seq
    n = w_main.shape[1]
    tm = _largest_row_tile(seq, min(INPROJ_ROWS, seq_x))
    per = seq // tm

    def x_start(i, j):
        row = (i // per) * seq_x + jnp.maximum((i % per) * tm - N_LEAD, 0)
        return (pl.multiple_of(row, BLOCK), 0)

    return pl.pallas_call(
        functools.partial(_inproj_kernel, tiles_per_seq=per),
        grid=(m // tm, n // tn),
        in_specs=[
            pl.BlockSpec((pl.Element(tm), pl.Element(d)), x_start),
            pl.BlockSpec((N_LEAD, d), lambda i, j: (0, 0)),
            pl.BlockSpec((1, d), lambda i, j: (0, 0)),
            pl.BlockSpec((d, tn), lambda i, j: (0, j)),
            pl.BlockSpec((d, LANES), lambda i, j: (0, 0)),
        ],
        out_specs=[
            pl.BlockSpec((tm, tn), lambda i, j: (i, j)),
            pl.BlockSpec((tm, LANES), lambda i, j: (i, 0)),
            pl.BlockSpec((tm, d), lambda i, j: (i, 0)),
        ],
        out_shape=[jax.ShapeDtypeStruct((m, n), BF16), jax.ShapeDtypeStruct((m, LANES), F32),
                   jax.ShapeDtypeStruct((m, d), F32)],
        scratch_shapes=[pltpu.VMEM((tm, d), BF16)],
        compiler_params=pltpu.CompilerParams(
            dimension_semantics=("parallel", "arbitrary"), vmem_limit_bytes=VMEM_LIMIT),
        name="inproj",
    )(x.reshape(bsz * seq_x, d), lead, norm_w, w_main, w_dt)


def _ssd_kernel(z_ref, xs_ref, bc_ref, dt_ref, *rest, n_heads, chunks):
    *consts, out_ref, tailx_scr, tailb_scr, state_scr = rest
    for k in range(chunks):
        rows = pl.ds(k * BLOCK, BLOCK)
        _ssd_chunk(pl.program_id(1) * chunks + k, z_ref.at[rows], xs_ref.at[rows], bc_ref.at[rows],
                   dt_ref.at[rows], *consts, out_ref.at[rows], tailx_scr, tailb_scr, state_scr,
                   n_heads=n_heads, may_start_sequence=(k == 0))


def _ssd_chunk(c, z_ref, xs_ref, bc_ref, dt_ref, cwx_ref, cbx_ref, cwb_ref, cbb_ref,
               dtb_ref, alog_ref, dskip_ref, nw_ref, expand_ref, tril_ref,
               out_ref, tailx_scr, tailb_scr, state_scr, *, n_heads, may_start_sequence):
    d_inner = xs_ref.shape[1]
    gw = d_inner // SSD_GROUPS
    hpg = n_heads // SSD_GROUPS

    if may_start_sequence:
        @pl.when(c == 0)
        def _():
            tailx_scr[...] = jnp.zeros(tailx_scr.shape, F32)
            tailb_scr[...] = jnp.zeros(tailb_scr.shape, F32)
            state_scr[...] = jnp.zeros(state_scr.shape, F32)

    row = lax.broadcasted_iota(jnp.int32, (BLOCK, 1), 0)
    vmask = (c * BLOCK + row >= N_PAD).astype(F32)
    first_row = lax.broadcasted_iota(jnp.int32, (SUBLANES, 1), 0) == 0

    def conv(x_ref, tail_scr, w_ref, b_ref):
        x = x_ref[...].astype(F32)
        tail = tail_scr[...]
        tail_scr[...] = x[BLOCK - SUBLANES:, :]
        groups = [x[r * SUBLANES:(r + 1) * SUBLANES, :] for r in range(BLOCK // SUBLANES)]
        accs = [w_ref[0:1, :] * g for g in groups]
        tacc = w_ref[0:1, :] * tail
        for i in range(1, CONV_K):
            trot = pltpu.roll(tacc, 1, axis=0)
            rots = [pltpu.roll(a, 1, axis=0) for a in accs]
            befores = [trot] + rots[:-1]
            accs = [w_ref[i:i + 1, :] * g + jnp.where(first_row, before, rot)
                    for g, before, rot in zip(groups, befores, rots)]
            tacc = w_ref[i:i + 1, :] * tail + trot
        return _silu(jnp.concatenate(accs, axis=0) + b_ref[...]) * vmask

    xs = conv(xs_ref, tailx_scr, cwx_ref, cbx_ref)
    bc = conv(bc_ref, tailb_scr, cwb_ref, cbb_ref)
    gn = SSD_GROUPS * SSD_STATE
    b_all = bc[:, :gn].astype(BF16)
    c_all = bc[:, gn:].astype(BF16)

    dt = jax.nn.softplus(dt_ref[:, :n_heads] + dtb_ref[...]) * vmask
    a = -jnp.exp(alog_ref[...]) * dt
    a_cs = jnp.dot(tril_ref[...], a, preferred_element_type=F32, precision=HIGHEST)
    a_cs_t = a_cs.T
    a_last = a_cs[BLOCK - 1:BLOCK, :]
    decay_states = jnp.exp(a_last - a_cs)
    decay_out = jnp.exp(a_cs)
    chunk_decay = jnp.broadcast_to(jnp.exp(a_last), (SUBLANES, n_heads))
    stacked = jnp.concatenate([dt, decay_states, decay_out, chunk_decay], axis=0)
    st_hi = stacked.astype(BF16)
    st_lo = (stacked - st_hi.astype(F32)).astype(BF16)
    expanded = (jnp.dot(st_hi, expand_ref[...], preferred_element_type=F32)
                + jnp.dot(st_lo, expand_ref[...], preferred_element_type=F32))
    dt_x = expanded[0:BLOCK]
    ds_x = expanded[BLOCK:2 * BLOCK]
    do_x = expanded[2 * BLOCK:3 * BLOCK]
    cd_x = expanded[3 * BLOCK:3 * BLOCK + 1]

    x_dt = xs * dt_x
    x_dt_b = x_dt.astype(BF16)
    x_ds_b = (x_dt * ds_x).astype(BF16)

    li = lax.broadcasted_iota(jnp.int32, (BLOCK, BLOCK), 0)
    si = lax.broadcasted_iota(jnp.int32, (BLOCK, BLOCK), 1)
    causal = si <= li
    first_half = lax.broadcasted_iota(jnp.int32, (BLOCK, 2 * SSD_HEADDIM), 1) < SSD_HEADDIM

    y_groups = []
    for g in range(SSD_GROUPS):
        bg = b_all[:, g * SSD_STATE:(g + 1) * SSD_STATE]
        cg = c_all[:, g * SSD_STATE:(g + 1) * SSD_STATE]
        cols = slice(g * gw, (g + 1) * gw)
        cb = _nt_dot(cg, bg)
        prev = state_scr[g]
        y_off = jnp.dot(cg, prev.astype(BF16), preferred_element_type=F32) * do_x[:, cols]
        new_state = _tn_dot(bg, x_ds_b[:, cols])
        state_scr[g] = cd_x[:, cols] * prev + new_state
        pairs = []
        for jp in range(hpg // 2):
            h0 = g * hpg + 2 * jp
            pc = slice(g * gw + jp * 2 * SSD_HEADDIM, g * gw + (jp + 1) * 2 * SSD_HEADDIM)
            xp = x_dt_b[:, pc]
            ys = []
            for h in (h0, h0 + 1):
                seg = a_cs[:, h:h + 1] - a_cs_t[h:h + 1, :]
                m = (cb * jnp.exp(jnp.where(causal, seg, -jnp.inf))).astype(BF16)
                ys.append(jnp.dot(m, xp, preferred_element_type=F32))
            pairs.append(jnp.where(first_half, ys[0], ys[1]))
        y_groups.append(jnp.concatenate(pairs, axis=1) + y_off)
    y = jnp.concatenate(y_groups, axis=1) + dskip_ref[...] * xs

    gated = y * _silu(z_ref[...].astype(F32))
    outs = []
    for g in range(SSD_GROUPS):
        gg = gated[:, g * gw:(g + 1) * gw]
        outs.append(gg * lax.rsqrt(jnp.mean(gg * gg, axis=-1, keepdims=True) + EPS))
    out_ref[...] = (jnp.concatenate(outs, axis=1) * nw_ref[...]).astype(out_ref.dtype)


def _ssd(proj, dt_raw, conv_w, conv_b, dt_bias, a_log, d_skip, norm_w, bsz, nc, d_inner):
    m = proj.shape[0]
    n_heads = a_log.shape[0]
    bc_w = 2 * SSD_GROUPS * SSD_STATE
    gw = d_inner // SSD_GROUPS
    cwx, cwb = conv_w[:, :d_inner], conv_w[:, d_inner:]
    cbx, cbb = conv_b[None, :d_inner], conv_b[None, d_inner:]
    expand = jnp.repeat(jnp.eye(n_heads, dtype=BF16), SSD_HEADDIM, axis=1)
    tril = jnp.tril(jnp.ones((BLOCK, BLOCK), F32))
    dskip_x = jnp.repeat(d_skip.astype(F32), SSD_HEADDIM)[None, :]
    full = lambda shape: pl.BlockSpec(shape, lambda b, c: (0,) * len(shape))
    chunks = max(k for k in range(1, SSD_CHUNKS_PER_STEP + 1) if nc % k == 0)
    steps = nc // chunks
    rows = chunks * BLOCK
    row_blk = lambda b, c: b * steps + c
    return pl.pallas_call(
        functools.partial(_ssd_kernel, n_heads=n_heads, chunks=chunks),
        grid=(bsz, steps),
        in_specs=[
            pl.BlockSpec((rows, d_inner), lambda b, c: (row_blk(b, c), 0)),
            pl.BlockSpec((rows, d_inner), lambda b, c: (row_blk(b, c), 1)),
            pl.BlockSpec((rows, bc_w), lambda b, c: (row_blk(b, c), 2 * d_inner // bc_w)),
            pl.BlockSpec((rows, LANES), lambda b, c: (row_blk(b, c), 0)),
            full((CONV_K, d_inner)), full((1, d_inner)), full((CONV_K, bc_w)), full((1, bc_w)),
            full((1, n_heads)), full((1, n_heads)), full((1, d_inner)), full((1, d_inner)),
            full((n_heads, d_inner)), full((BLOCK, BLOCK)),
        ],
        out_specs=pl.BlockSpec((rows, d_inner), lambda b, c: (row_blk(b, c), 0)),
        out_shape=jax.ShapeDtypeStruct((m, d_inner), BF16),
        scratch_shapes=[
            pltpu.VMEM((SUBLANES, d_inner), F32),
            pltpu.VMEM((SUBLANES, bc_w), F32),
            pltpu.VMEM((SSD_GROUPS, SSD_STATE, gw), F32),
        ],
        compiler_params=pltpu.CompilerParams(
            dimension_semantics=("parallel", "arbitrary"), vmem_limit_bytes=VMEM_LIMIT),
        name="ssd",
    )(proj, proj, proj, dt_raw, cwx, cbx, cwb, cbb, dt_bias[None, :].astype(F32),
      a_log[None, :].astype(F32), dskip_x, norm_w[None, :].astype(F32), expand, tril)


def _attn_kernel(q_ref, k_ref, v_ref, upper_ref, out_ref, *, n_pairs, q_blocks):
    for k in range(q_blocks):
        rows = pl.ds(k * BLOCK, BLOCK)
        _attn_block(pl.program_id(2) * q_blocks + k, q_ref.at[rows], k_ref, v_ref, upper_ref,
                    out_ref.at[rows], n_pairs=n_pairs)


def _attn_block(i, q_ref, k_ref, v_ref, upper_ref, out_ref, *, n_pairs):
    pw = 2 * ATT_HEADDIM
    lane = lax.broadcasted_iota(jnp.int32, (BLOCK, pw), 1)
    first = lane < ATT_HEADDIM
    row2 = lax.broadcasted_iota(jnp.int32, (2 * BLOCK, BLOCK), 0)
    qpos = i * BLOCK + jnp.where(row2 < BLOCK, row2, row2 - BLOCK)
    kin = lax.broadcasted_iota(jnp.int32, (2 * BLOCK, BLOCK), 1)
    upper = upper_ref[...]
    scale = ATT_HEADDIM ** -0.5
    q2 = []
    for p in range(n_pairs):
        q = q_ref[:, p * pw:(p + 1) * pw] * jnp.asarray(scale, BF16)
        zero = jnp.zeros_like(q)
        q2.append(jnp.concatenate([jnp.where(first, q, zero), jnp.where(first, zero, q)], axis=0))

    def process(kbs, runs, accs):
        starts = [pl.multiple_of(jnp.maximum(kb, 0) * BLOCK, BLOCK) for kb in kbs]
        alloweds = []
        for kb in kbs:
            kpos = kb * BLOCK + kin
            alloweds.append(jnp.logical_and(kpos < qpos, kpos >= N_PAD))
        zs = [[_nt_dot(q2[p], k_ref[pl.ds(start, BLOCK), p * pw:(p + 1) * pw])
               for p in range(n_pairs)] for start in starts]
        log_betas, laters, sums = [], [], []
        for j in range(len(kbs)):
            log_betas.append([])
            laters.append([])
            sums.append([])
            for p in range(n_pairs):
                z = jnp.where(alloweds[j], zs[j][p], MASKED_SCORE)
                l1p = jnp.log(1.0 + jnp.exp(-jnp.abs(z)))
                log_beta = jnp.minimum(z, 0.0) - l1p
                log_stay = log_beta - z
                hi = log_stay.astype(BF16)
                lo = (log_stay - hi.astype(F32)).astype(BF16)
                log_betas[j].append(log_beta)
                laters[j].append(jnp.dot(hi, upper, preferred_element_type=F32)
                                 + jnp.dot(lo, upper, preferred_element_type=F32))
                sums[j].append(jnp.sum(log_stay, axis=1, keepdims=True))
        runs, accs = list(runs), list(accs)
        for j, start in enumerate(starts):
            for p in range(n_pairs):
                vblk = v_ref[pl.ds(start, BLOCK), p * pw:(p + 1) * pw]
                w = jnp.exp(log_betas[j][p] + laters[j][p] + runs[p]).astype(BF16)
                vzero = jnp.zeros_like(vblk)
                accs[p] = (accs[p]
                           + jnp.dot(w[:BLOCK], jnp.where(first, vblk, vzero),
                                     preferred_element_type=F32)
                           + jnp.dot(w[BLOCK:], jnp.where(first, vzero, vblk),
                                     preferred_element_type=F32))
                runs[p] = runs[p] + sums[j][p]
        top = jnp.max(runs[0])
        for r in runs[1:]:
            top = jnp.maximum(top, jnp.max(r))
        return top, runs, accs

    def cond(carry):
        kb, top = carry[0], carry[1]
        return jnp.logical_and(kb >= 0, top > EXP_UNDERFLOW)

    def body(carry):
        kb = carry[0]
        top, runs, accs = process([kb], carry[2:2 + n_pairs], carry[2 + n_pairs:])
        return (kb - 1, top, *runs, *accs)

    zero_runs = [jnp.zeros((2 * BLOCK, 1), F32) for _ in range(n_pairs)]
    zero_accs = [jnp.zeros((BLOCK, pw), F32) for _ in range(n_pairs)]
    top, runs, accs = process([i, i - 1], zero_runs, zero_accs)
    final = lax.while_loop(cond, body, (i - 2, top, *runs, *accs))
    for p in range(n_pairs):
        out_ref[:, p * pw:(p + 1) * pw] = final[2 + n_pairs + p].astype(out_ref.dtype)


def _attention(proj, bsz, nc, q_col, k_col, v_col, att_dim, pairs_per_step=4):
    m = proj.shape[0]
    seq = nc * BLOCK
    sw = pairs_per_step * 2 * ATT_HEADDIM
    upper = jnp.triu(jnp.ones((BLOCK, BLOCK), F32), 1).T.astype(BF16)
    q_blocks = max(k for k in range(1, ATT_Q_BLOCKS_PER_STEP + 1) if nc % k == 0)
    steps = nc // q_blocks
    rows = q_blocks * BLOCK
    return pl.pallas_call(
        functools.partial(_attn_kernel, n_pairs=pairs_per_step, q_blocks=q_blocks),
        grid=(bsz, att_dim // sw, steps),
        in_specs=[
            pl.BlockSpec((rows, sw), lambda b, p, i: (b * steps + i, q_col // sw + p)),
            pl.BlockSpec((seq, sw), lambda b, p, i: (b, k_col // sw + p)),
            pl.BlockSpec((seq, sw), lambda b, p, i: (b, v_col // sw + p)),
            pl.BlockSpec((BLOCK, BLOCK), lambda b, p, i: (0, 0)),
        ],
        out_specs=pl.BlockSpec((rows, sw), lambda b, p, i: (b * steps + i, p)),
        out_shape=jax.ShapeDtypeStruct((m, att_dim), BF16),
        compiler_params=pltpu.CompilerParams(
            dimension_semantics=("parallel", "parallel", "arbitrary"),
            vmem_limit_bytes=VMEM_LIMIT),
        name="attention",
    )(proj, proj, proj, upper)


def _merge_kernel(yssd_ref, att_ref, gs_ref, ga_ref, h_ref, wso_ref, wao_ref, wo_ref, out_ref):
    t = pl.program_id(1)
    y_ssd = jnp.dot(yssd_ref[...], wso_ref[...], preferred_element_type=F32)
    y_att = jnp.dot(att_ref[...], wao_ref[...], preferred_element_type=F32)
    merged = (jax.nn.sigmoid(gs_ref[...].astype(F32)) * y_ssd
              + jax.nn.sigmoid(ga_ref[...].astype(F32)) * y_att)
    mixed = jnp.dot(merged.astype(BF16), wo_ref[...], preferred_element_type=F32)
    row = lax.broadcasted_iota(jnp.int32, (ROW_TILE, 1), 0)
    valid = (t * ROW_TILE + row >= N_PAD).astype(F32)
    out_ref[...] = h_ref[...] + mixed * valid


def _merge(yssd, att, proj, h, w_ssd_out, w_att_out, w_out, bsz, seq, gs_col, ga_col):
    m, d = h.shape
    d_inner = yssd.shape[1]
    att_dim = att.shape[1]
    per = seq // ROW_TILE
    rb = lambda b, t: b * per + t
    full = lambda shape: pl.BlockSpec(shape, lambda b, t: (0,) * len(shape))
    return pl.pallas_call(
        _merge_kernel,
        grid=(bsz, per),
        in_specs=[
            pl.BlockSpec((ROW_TILE, d_inner), lambda b, t: (rb(b, t), 0)),
            pl.BlockSpec((ROW_TILE, att_dim), lambda b, t: (rb(b, t), 0)),
            pl.BlockSpec((ROW_TILE, d), lambda b, t: (rb(b, t), gs_col // d)),
            pl.BlockSpec((ROW_TILE, d), lambda b, t: (rb(b, t), ga_col // d)),
            pl.BlockSpec((ROW_TILE, d), lambda b, t: (rb(b, t), 0)),
            full((d_inner, d)), full((att_dim, d)), full((d, d)),
        ],
        out_specs=pl.BlockSpec((ROW_TILE, d), lambda b, t: (rb(b, t), 0)),
        out_shape=jax.ShapeDtypeStruct((m, d), F32),
        compiler_params=pltpu.CompilerParams(
            dimension_semantics=("parallel", "parallel"), vmem_limit_bytes=VMEM_LIMIT),
        name="merge",
    )(yssd, att, proj, proj, h, w_ssd_out, w_att_out, w_out)


def _router_kernel(h_ref, nw_ref, wrh_ref, wrl_ref, br_ref, before_ref,
                   ut_ref, idx_ref, gate_ref, rank_ref, cnt_ref, base_scr):
    step = pl.program_id(0)
    n_exp = br_ref.shape[0]
    tm = h_ref.shape[0]

    @pl.when(step == 0)
    def _():
        base_scr[...] = jnp.zeros(base_scr.shape, F32)

    x = h_ref[...]
    u = x * lax.rsqrt(jnp.mean(x * x, axis=-1, keepdims=True) + EPS) * nw_ref[...]
    for c in range(SUBLANES):
        ut_ref[pl.ds(c, tm, stride=SUBLANES), :] = u[:, c * LANES:(c + 1) * LANES]

    u_hi = u.astype(BF16)
    u_lo = (u - u_hi.astype(F32)).astype(BF16)
    logits_tm = (jnp.dot(u_hi, wrh_ref[...], preferred_element_type=F32)
                 + jnp.dot(u_lo, wrh_ref[...], preferred_element_type=F32)
                 + jnp.dot(u_hi, wrl_ref[...], preferred_element_type=F32))
    logits = logits_tm.T[:n_exp] + br_ref[...]
    eidx = lax.broadcasted_iota(jnp.int32, (n_exp, tm), 0)
    work = logits
    tops, idxs, onehots = [], [], []
    for _ in range(TOP_K):
        top = jnp.max(work, axis=0, keepdims=True)
        idx = jnp.min(jnp.where(work == top, eidx, n_exp), axis=0, keepdims=True)
        hot = eidx == idx
        work = jnp.where(hot, -jnp.inf, work)
        tops.append(top)
        idxs.append(idx)
        onehots.append(hot.astype(F32))
    exps = [jnp.exp(v - tops[0]) for v in tops]
    denom = exps[0] + exps[1] + exps[2] + exps[3]
    cnt = onehots[0] + onehots[1] + onehots[2] + onehots[3]
    before = jnp.dot(cnt.astype(BF16), before_ref[...], preferred_element_type=F32) + base_scr[...]
    ranks = [jnp.sum(hot * before, axis=0, keepdims=True) for hot in onehots]
    pad_rows = SUBLANES - TOP_K
    idx_ref[...] = jnp.concatenate(idxs + [jnp.zeros((pad_rows, tm), jnp.int32)], axis=0)
    gate_ref[...] = jnp.concatenate([e / denom for e in exps] + [jnp.zeros((pad_rows, tm), F32)], axis=0)
    rank_ref[...] = jnp.concatenate(
        [r.astype(jnp.int32) for r in ranks] + [jnp.zeros((pad_rows, tm), jnp.int32)], axis=0)
    base_scr[...] = base_scr[...] + jnp.sum(cnt, axis=1, keepdims=True)
    cnt_ref[...] = jnp.broadcast_to(base_scr[...], cnt_ref.shape)


def _router(h1, norm_w, w_router, b_router):
    m, d = h1.shape
    n_exp = w_router.shape[1]
    before = jnp.triu(jnp.ones((ROW_TILE, ROW_TILE), F32), 1).astype(BF16)
    wr = jnp.pad(w_router.astype(F32), ((0, 0), (0, LANES - n_exp)))
    wr_hi = wr.astype(BF16)
    wr_lo = (wr - wr_hi.astype(F32)).astype(BF16)
    full = lambda shape: pl.BlockSpec(shape, lambda i: (0,) * len(shape))
    return pl.pallas_call(
        _router_kernel,
        grid=(m // ROW_TILE,),
        in_specs=[
            pl.BlockSpec((ROW_TILE, d), lambda i: (i, 0)),
            full((1, d)), full((d, LANES)), full((d, LANES)), full((n_exp, 1)),
            full((ROW_TILE, ROW_TILE)),
        ],
        out_specs=[
            pl.BlockSpec((ROW_TILE * SUBLANES, LANES), lambda i: (i, 0)),
            pl.BlockSpec((SUBLANES, ROW_TILE), lambda i: (0, i)),
            pl.BlockSpec((SUBLANES, ROW_TILE), lambda i: (0, i)),
            pl.BlockSpec((SUBLANES, ROW_TILE), lambda i: (0, i)),
            full((n_exp, LANES)),
        ],
        out_shape=[
            jax.ShapeDtypeStruct((m * SUBLANES, LANES), F32),
            jax.ShapeDtypeStruct((SUBLANES, m), jnp.int32),
            jax.ShapeDtypeStruct((SUBLANES, m), F32),
            jax.ShapeDtypeStruct((SUBLANES, m), jnp.int32),
            jax.ShapeDtypeStruct((n_exp, LANES), F32),
        ],
        scratch_shapes=[pltpu.VMEM((n_exp, 1), F32)],
        compiler_params=pltpu.CompilerParams(
            dimension_semantics=("arbitrary",), vmem_limit_bytes=VMEM_LIMIT),
        name="router",
    )(h1, norm_w, wr_hi, wr_lo, b_router[:, None], before)


def _dispatch_kernel(zstart_ref, zpad_ref, nused_ref, dest_ref, ut_ref, xs_ref, zero_scr, sem, zsem,
                     *, n_exp):
    step = pl.program_id(0)
    tb = dest_ref.shape[1]
    slot_rows = MOE_BLOCK * SUBLANES
    n_blocks = xs_ref.shape[0] // slot_rows

    @pl.when(step == 0)
    def _():
        zero_scr[...] = jnp.zeros(zero_scr.shape, F32)

        def zero_rows(row_start, n_rows):
            return pltpu.make_async_copy(zero_scr.at[pl.ds(0, n_rows)],
                                         xs_ref.at[pl.ds(row_start, n_rows)], zsem)

        def pad_fill(wait):
            def per_expert(e, carry):
                n_pad = zpad_ref[e]
                pos = zstart_ref[e]
                size = MOE_BLOCK // 2
                while size >= 1:
                    take = n_pad & size

                    @pl.when(take != 0)
                    def _(pos=pos, size=size):
                        copy = zero_rows(pl.multiple_of(pos * SUBLANES, SUBLANES), size * SUBLANES)
                        copy.wait() if wait else copy.start()

                    pos = pos + take
                    size //= 2
                return carry

            lax.fori_loop(0, n_exp, per_expert, 0)

        def tail_fill(wait):
            def per_block(b, carry):
                copy = zero_rows(pl.multiple_of(b * slot_rows, slot_rows), slot_rows)
                copy.wait() if wait else copy.start()
                return carry

            lax.fori_loop(nused_ref[0], n_blocks, per_block, 0)

        pad_fill(False)
        tail_fill(False)
        pad_fill(True)
        tail_fill(True)

    def issue(t, carry):
        src = pl.multiple_of(t * SUBLANES, SUBLANES)
        for k in range(TOP_K):
            dst = pl.multiple_of(dest_ref[k, t] * SUBLANES, SUBLANES)
            pltpu.make_async_copy(ut_ref.at[pl.ds(src, SUBLANES)],
                                  xs_ref.at[pl.ds(dst, SUBLANES)], sem).start(priority=k % 2)
        return carry

    lax.fori_loop(0, tb, issue, 0, unroll=4)
    for _ in range(TOP_K):
        pltpu.make_async_copy(ut_ref, xs_ref.at[pl.ds(0, tb * SUBLANES)], sem).wait()


def _dispatch(ut, dest, zstart, zpad, n_used, n_slots):
    m = dest.shape[1]
    n_exp = zstart.shape[0]
    tb = _largest_row_tile(m, DISPATCH_ROWS)
    return pl.pallas_call(
        functools.partial(_dispatch_kernel, n_exp=n_exp),
        grid_spec=pltpu.PrefetchScalarGridSpec(
            num_scalar_prefetch=3,
            grid=(m // tb,),
            in_specs=[
                pl.BlockSpec((SUBLANES, tb), lambda i, *_: (0, i), memory_space=pltpu.SMEM),
                pl.BlockSpec((tb * SUBLANES, LANES), lambda i, *_: (i, 0)),
            ],
            out_specs=pl.BlockSpec(memory_space=pl.ANY),
            scratch_shapes=[
                pltpu.VMEM((MOE_BLOCK * SUBLANES, LANES), F32),
                pltpu.SemaphoreType.DMA(()),
                pltpu.SemaphoreType.DMA(()),
            ],
        ),
        out_shape=jax.ShapeDtypeStruct((n_slots * SUBLANES, LANES), F32),
        compiler_params=pltpu.CompilerParams(
            dimension_semantics=("arbitrary",), vmem_limit_bytes=VMEM_LIMIT),
        name="dispatch",
    )(zstart, zpad, n_used, dest, ut)


def _expert_kernel(be_ref, nused_ref, slot_ref, next_ref, xs_ref, wgu_hbm, bg_ref, bl_ref, wd_hbm,
                   bd_ref, perm_ref, y_ref, wgu_buf, wd_buf, wg_scr, wl_scr, wd_scr, sem):
    b = pl.program_id(0)
    pair = 2 * LANES
    changed = jnp.logical_or(b == 0, be_ref[b] != be_ref[jnp.maximum(b - 1, 0)])

    def fetch(expert, slot):
        return (pltpu.make_async_copy(wgu_hbm.at[expert], wgu_buf.at[slot], sem.at[0, slot]),
                pltpu.make_async_copy(wd_hbm.at[expert], wd_buf.at[slot], sem.at[1, slot]))

    @pl.when(b == 0)
    def _():
        for copy in fetch(be_ref[0], slot_ref[0]):
            copy.start()

    @pl.when(jnp.logical_and(changed, b < nused_ref[0]))
    def _():
        slot = slot_ref[b]
        for copy in fetch(be_ref[b], slot):
            copy.wait()

        @pl.when(next_ref[b] >= 0)
        def _():
            for copy in fetch(next_ref[b], 1 - slot):
                copy.start()

        for cb in range(wgu_buf.shape[2] // pair):
            wb = wgu_buf[slot, :, cb * pair:(cb + 1) * pair].astype(BF16)
            sep = jnp.dot(wb, perm_ref[...], preferred_element_type=F32).astype(BF16)
            wg_scr[:, cb * LANES:(cb + 1) * LANES] = sep[:, :LANES]
            wl_scr[:, cb * LANES:(cb + 1) * LANES] = sep[:, LANES:]
        wd_scr[...] = wd_buf[slot].astype(BF16)

    @pl.when(b < nused_ref[0])
    def _():
        x = jnp.concatenate(
            [xs_ref[pl.ds(c, MOE_BLOCK, stride=SUBLANES), :] for c in range(SUBLANES)],
            axis=1).astype(BF16)
        gate = jnp.dot(x, wg_scr[...], preferred_element_type=F32) + bg_ref[0]
        lin = jnp.dot(x, wl_scr[...], preferred_element_type=F32) + bl_ref[0]
        gate = jnp.minimum(gate, SWIGLU_LIMIT)
        lin = jnp.clip(lin, -SWIGLU_LIMIT, SWIGLU_LIMIT)
        act = gate * jax.nn.sigmoid(SWIGLU_ALPHA * gate) * (lin + 1.0)
        y = jnp.dot(act.astype(BF16), wd_scr[...], preferred_element_type=F32) + bd_ref[0]
        for c in range(SUBLANES):
            y_ref[pl.ds(c, MOE_BLOCK, stride=SUBLANES), :] = y[:, c * LANES:(c + 1) * LANES]

    @pl.when(b >= nused_ref[0])
    def _():
        y_ref[...] = jnp.zeros(y_ref.shape, F32)


def _experts(xs, block_expert, n_used, block_slot, block_next, w_gate_up, b_gate, b_lin, w_down,
             b_down, n_blocks):
    n_exp, d, d_ff2 = w_gate_up.shape
    d_ff = d_ff2 // 2
    rows = MOE_BLOCK * SUBLANES
    blk = lambda b, be, nu, *_: (jnp.minimum(b, nu[0] - 1), 0)
    wsel = lambda b, be, *_: (be[b], 0, 0)
    r = jnp.arange(2 * LANES)[:, None]
    c = jnp.arange(2 * LANES)[None, :]
    perm = (r == jnp.where(c < LANES, 2 * c, 2 * (c - LANES) + 1)).astype(BF16)
    return pl.pallas_call(
        _expert_kernel,
        grid_spec=pltpu.PrefetchScalarGridSpec(
            num_scalar_prefetch=4,
            grid=(n_blocks,),
            in_specs=[
                pl.BlockSpec((rows, LANES), blk),
                pl.BlockSpec(memory_space=pl.ANY),
                pl.BlockSpec((1, 1, d_ff), wsel), pl.BlockSpec((1, 1, d_ff), wsel),
                pl.BlockSpec(memory_space=pl.ANY), pl.BlockSpec((1, 1, d), wsel),
                pl.BlockSpec((2 * LANES, 2 * LANES), lambda b, *_: (0, 0)),
            ],
            out_specs=pl.BlockSpec((rows, LANES), lambda b, *_: (b, 0)),
            scratch_shapes=[
                pltpu.VMEM((2, d, d_ff2), F32), pltpu.VMEM((2, d_ff, d), F32),
                pltpu.VMEM((d, d_ff), BF16), pltpu.VMEM((d, d_ff), BF16), pltpu.VMEM((d_ff, d), BF16),
                pltpu.SemaphoreType.DMA((2, 2)),
            ],
        ),
        out_shape=jax.ShapeDtypeStruct((n_blocks * rows, LANES), F32),
        compiler_params=pltpu.CompilerParams(
            dimension_semantics=("arbitrary",), vmem_limit_bytes=VMEM_LIMIT),
        name="experts",
    )(block_expert, n_used, block_slot, block_next, xs, w_gate_up, b_gate, b_lin, w_down, b_down,
      perm)


def _combine_kernel(dest_ref, gate_ref, h_ref, nw_ref, eye_ref, ypad_ref, out_ref, *scratch, n_steps):
    bufs = scratch[:COMBINE_BUFFERS]
    sem = scratch[COMBINE_BUFFERS]
    s = pl.program_id(0)
    rows = BLOCK * SUBLANES
    n_groups = BLOCK // SUBLANES

    def step(new, old):
        def issue_group(g):
            for j in range(SUBLANES):
                t = g * SUBLANES + j
                for k in range(TOP_K):
                    src = pl.multiple_of(dest_ref[k, t] * SUBLANES, SUBLANES)
                    dst = pl.multiple_of(k * rows + t * SUBLANES, SUBLANES)
                    pltpu.make_async_copy(ypad_ref.at[pl.ds(src, SUBLANES)],
                                          bufs[new].at[pl.ds(dst, SUBLANES)],
                                          sem.at[new]).start(priority=k % 2)

        def compute():
            gates_t = _nt_dot(eye_ref[...], gate_ref[...], precision=HIGHEST)
            acc = h_ref[...]
            for k in range(TOP_K):
                yk = jnp.concatenate(
                    [bufs[old][pl.ds(k * rows + c, BLOCK, stride=SUBLANES), :]
                     for c in range(SUBLANES)], axis=1)
                acc = acc + gates_t[:, k:k + 1] * yk
            scale = lax.rsqrt(jnp.mean(acc * acc, axis=-1, keepdims=True) + EPS)
            out_ref[...] = acc * scale * nw_ref[...]

        def wait_old():
            pltpu.make_async_copy(ypad_ref.at[pl.ds(0, TOP_K * rows)], bufs[old], sem.at[old]).wait()

        @pl.when(s < COMBINE_LAG)
        def _():
            lax.fori_loop(0, n_groups, lambda g, carry: (issue_group(g), carry)[1], 0)

        @pl.when(jnp.logical_and(s >= COMBINE_LAG, s < n_steps))
        def _():
            wait_old()
            for g in range(n_groups):
                issue_group(g)
            compute()

        @pl.when(s >= n_steps)
        def _():
            wait_old()
            compute()

    for new in range(COMBINE_BUFFERS):
        pl.when(s % COMBINE_BUFFERS == new)(
            functools.partial(step, new, (new + 1) % COMBINE_BUFFERS))


def _combine(ypad, dest, gates, h1, final_norm_w, bsz, nc):
    m, d = h1.shape
    per = nc - 1
    n_steps = bsz * per
    eye = jnp.eye(BLOCK, dtype=F32)

    def tok_blk(s):
        s = jnp.clip(s, 0, n_steps - 1)
        return (s // per) * nc + 1 + s % per

    return pl.pallas_call(
        functools.partial(_combine_kernel, n_steps=n_steps),
        grid=(n_steps + COMBINE_LAG,),
        in_specs=[
            pl.BlockSpec((SUBLANES, BLOCK), lambda s: (0, tok_blk(s)), memory_space=pltpu.SMEM),
            pl.BlockSpec((SUBLANES, BLOCK), lambda s: (0, tok_blk(s - COMBINE_LAG))),
            pl.BlockSpec((BLOCK, d), lambda s: (tok_blk(s - COMBINE_LAG), 0)),
            pl.BlockSpec((1, d), lambda s: (0, 0)),
            pl.BlockSpec((BLOCK, BLOCK), lambda s: (0, 0)),
            pl.BlockSpec(memory_space=pl.ANY),
        ],
        out_specs=pl.BlockSpec((BLOCK, d), lambda s: (jnp.clip(s - COMBINE_LAG, 0, n_steps - 1), 0)),
        out_shape=jax.ShapeDtypeStruct((n_steps * BLOCK, d), F32),
        scratch_shapes=(
            [pltpu.VMEM((TOP_K * BLOCK * SUBLANES, LANES), F32) for _ in range(COMBINE_BUFFERS)]
            + [pltpu.SemaphoreType.DMA((COMBINE_BUFFERS,))]),
        compiler_params=pltpu.CompilerParams(
            dimension_semantics=("arbitrary",), vmem_limit_bytes=VMEM_LIMIT),
        name="combine",
    )(dest, gates, h1, final_norm_w[None, :], eye, ypad)


def _layer(x, lead, nc, mix_norm_w, w_in, conv_w, conv_b, dt_bias, a_log, d_skip, ssd_norm_w,
           w_ssd_out, w_att_out, w_out, ffn_norm_w, w_router, b_router, w_gate_up, b_gate_up,
           w_down, b_down):
    bsz, _, d = x.shape
    seq = nc * BLOCK
    m = bsz * seq
    n_heads = a_log.shape[0]
    d_inner = n_heads * SSD_HEADDIM
    conv_dim = d_inner + 2 * SSD_GROUPS * SSD_STATE
    att_dim = w_att_out.shape[0]
    n_exp = w_router.shape[1]

    o_xbc = d_inner
    o_dt = o_xbc + conv_dim
    o_q = o_dt + n_heads
    w_main = jnp.concatenate([w_in[:, :o_dt], w_in[:, o_q:]], axis=1).astype(BF16)
    w_dt = jnp.pad(w_in[:, o_dt:o_q], ((0, 0), (0, LANES - n_heads))).astype(BF16)
    q_col = o_dt
    k_col = q_col + att_dim
    v_col = k_col + att_dim
    gs_col = v_col + att_dim
    ga_col = gs_col + d

    proj, dt_raw, h = _inproj(x, lead, mix_norm_w[None, :], w_main, w_dt)
    yssd = _ssd(proj, dt_raw, conv_w, conv_b, dt_bias, a_log, d_skip, ssd_norm_w, bsz, nc, d_inner)
    att = _attention(proj, bsz, nc, q_col, k_col, v_col, att_dim)
    h1 = _merge(yssd, att, proj, h, w_ssd_out.astype(BF16), w_att_out.astype(BF16),
                w_out.astype(BF16), bsz, seq, gs_col, ga_col)

    ut, idx, gates, rank, cnt = _router(h1, ffn_norm_w[None, :], w_router, b_router)

    counts = cnt[:, 0].astype(jnp.int32)
    padded = (counts + MOE_BLOCK - 1) // MOE_BLOCK * MOE_BLOCK
    padded_ends = jnp.cumsum(padded)
    padded_starts = padded_ends - padded
    n_blocks = (m * TOP_K + MOE_BLOCK - 1) // MOE_BLOCK + n_exp
    dest = rank + jnp.sum(
        jnp.where(idx[None] == jnp.arange(n_exp, dtype=jnp.int32)[:, None, None],
                  padded_starts[:, None, None], 0), axis=0)
    block_first = jnp.arange(n_blocks, dtype=jnp.int32) * MOE_BLOCK
    block_expert = jnp.minimum(
        jnp.sum((padded_ends[None, :] <= block_first[:, None]).astype(jnp.int32), axis=1), n_exp - 1)
    n_used = (padded_ends[-1:] // MOE_BLOCK).astype(jnp.int32)
    zstart = (padded_starts + counts).astype(jnp.int32)

    zpad = (padded - counts).astype(jnp.int32)
    xs = _dispatch(ut, dest, zstart, zpad, n_used, n_blocks * MOE_BLOCK)
    b_gate = b_gate_up[:, None, 0::2]
    b_lin = b_gate_up[:, None, 1::2]
    experts = jnp.arange(n_exp, dtype=jnp.int32)
    nonempty = padded > 0
    order = jnp.cumsum(nonempty.astype(jnp.int32)) - 1
    later = jnp.where(nonempty[None, :] & (experts[None, :] > experts[:, None]), experts[None, :], n_exp)
    next_nonempty = jnp.min(later, axis=1)
    next_nonempty = jnp.where(next_nonempty >= n_exp, -1, next_nonempty)
    of_block = block_expert[:, None] == experts[None, :]
    block_slot = jnp.sum(jnp.where(of_block, order[None, :] % 2, 0), axis=1).astype(jnp.int32)
    block_next = jnp.sum(jnp.where(of_block, next_nonempty[None, :], 0), axis=1).astype(jnp.int32)
    ypad = _experts(xs, block_expert, n_used, block_slot, block_next, w_gate_up, b_gate, b_lin, w_down,
                    b_down[:, None, :], n_blocks)
    return h1, ypad, dest, gates


def kernel(x, meta_tokens, mix_norm_w, w_in, conv_w, conv_b, dt_bias, a_log, d_skip, ssd_norm_w,
           w_ssd_out, w_att_out, w_out, ffn_norm_w, w_router, b_router, w_gate_up, b_gate_up,
           w_down, b_down, final_norm_w):
    bsz, seq_x, d = x.shape
    depth = mix_norm_w.shape[0]
    assert depth == 1 and seq_x % BLOCK == 0 and (seq_x + N_LEAD) % ROW_TILE == 0
    nc = (seq_x + N_LEAD) // BLOCK
    lead = jnp.concatenate([jnp.zeros((N_PAD, d), x.dtype), meta_tokens.astype(x.dtype)], axis=0)
    layer = 0
    h1, ypad, dest, gates = _layer(
        x, lead, nc, mix_norm_w[layer], w_in[layer], conv_w[layer], conv_b[layer], dt_bias[layer],
        a_log[layer], d_skip[layer], ssd_norm_w[layer], w_ssd_out[layer], w_att_out[layer],
        w_out[layer], ffn_norm_w[layer], w_router[layer], b_router[layer], w_gate_up[layer],
        b_gate_up[layer], w_down[layer], b_down[layer])
    out = _combine(ypad, dest, gates, h1, final_norm_w, bsz, nc)
    return out.reshape(bsz, seq_x, d)
```
